```python
import math
import numpy as np
import jax
import jax.numpy as jnp
from jax import lax

D_MODEL = 1024
BATCH = 1
SEQ = 16384
DEPTH = 4

EPS = 1e-6
N_BRANCH = 3
BRANCH_WIDTH = 512
SSM_HEADS = 8
SSM_HEAD_DIM = 64
SSM_INNER = 512
SSM_GROUPS = 2
SSM_STATE = 128
SSM_CONV = 4
SSD_CHUNK = 128
SSM_XBC = SSM_INNER + 2 * SSM_GROUPS * SSM_STATE
SB_HEADS = 4
SB_HEAD_DIM = 128
SB_INNER = 512
SB_BLOCK = 128
GDN_HEADS = 4
GDN_HEAD_DIM = 128
GDN_INNER = 512
GDN_CONV = 4
GDN_CHUNK = 64
MOE_GROUPS = 4
EXPERTS_PER_GROUP = 8
N_EXPERTS = 32
MOE_TOP_K = 2
EXPERT_FF = 512
MOE_BLOCK = 128
IN_SPLITS = (SSM_INNER, SSM_XBC, SSM_HEADS, 3 * SB_INNER, 3 * GDN_INNER, GDN_HEADS, GDN_HEADS, GDN_INNER, N_BRANCH * D_MODEL)
IN_COLS = SSM_INNER + SSM_XBC + SSM_HEADS + 3 * SB_INNER + 3 * GDN_INNER + 2 * GDN_HEADS + GDN_INNER + N_BRANCH * D_MODEL

kernel_name = "hybrid_ssd_stickbreak_gdn_hmoe"


def rms_norm(x, g):
    xf = x.astype(jnp.float32)
    y = xf * lax.rsqrt(jnp.mean(xf * xf, axis=-1, keepdims=True) + EPS)
    return (y * g.astype(jnp.float32)).astype(x.dtype)


def l2_norm(x):
    xf = x.astype(jnp.float32)
    return xf * lax.rsqrt(jnp.sum(xf * xf, axis=-1, keepdims=True) + EPS)


def causal_conv(x, w, b=None):
    k, ch = w.shape
    y = lax.conv_general_dilated(x, w[:, None, :].astype(x.dtype), window_strides=(1,), padding=[(k - 1, 0)],
                                 dimension_numbers=("NWC", "WIO", "NWC"), feature_group_count=ch)
    if b is not None:
        y = y + b.astype(x.dtype)
    return y


def ssd_chunked(xs, dt, a, bm, cm):
    bsz, s, h, p = xs.shape
    n = bm.shape[-1]
    nc = s // SSD_CHUNK
    xdt = (xs * dt[..., None]).reshape(bsz, nc, SSD_CHUNK, h, p)
    adt = (dt * a).reshape(bsz, nc, SSD_CHUNK, h).transpose(0, 3, 1, 2)
    bm = bm.reshape(bsz, nc, SSD_CHUNK, h, n)
    cm = cm.reshape(bsz, nc, SSD_CHUNK, h, n)
    a_cum = jnp.cumsum(adt, axis=-1)
    tril = jnp.tril(jnp.ones((SSD_CHUNK, SSD_CHUNK), bool))
    seg = jnp.exp(jnp.where(tril, a_cum[..., :, None] - a_cum[..., None, :], -jnp.inf))
    scores = jnp.einsum("bclhn,bcshn->bhcls", cm, bm) * seg
    y_diag = jnp.einsum("bhcls,bcshp->bclhp", scores, xdt)
    decay_states = jnp.exp(a_cum[..., -1:] - a_cum).transpose(0, 2, 3, 1)
    states = jnp.einsum("bclhn,bclhp->bchpn", bm, xdt * decay_states[..., None])
    chunk_decay = jnp.exp(a_cum[..., -1])

    def step(hstate, inp):
        st, dec = inp
        return hstate * dec[..., None, None] + st, hstate

    h0 = jnp.zeros((bsz, h, p, n), xs.dtype)
    _, states_in = lax.scan(step, h0, (states.transpose(1, 0, 2, 3, 4), chunk_decay.transpose(2, 0, 1)))
    states_in = states_in.transpose(1, 0, 2, 3, 4)
    c_dec = cm * jnp.exp(a_cum).transpose(0, 2, 3, 1)[..., None]
    y_off = jnp.einsum("bclhn,bchpn->bclhp", c_dec, states_in)
    return (y_diag + y_off).reshape(bsz, s, h, p)


def mamba2_branch(z, xbc, dt_raw, conv_w, conv_b, dt_bias, a_log, d_skip, norm_g):
    bsz, s, _ = z.shape
    xbc = jax.nn.silu(causal_conv(xbc, conv_w, conv_b)).astype(jnp.float32)
    xs, bm, cm = jnp.split(xbc, [SSM_INNER, SSM_INNER + SSM_GROUPS * SSM_STATE], axis=-1)
    xs = xs.reshape(bsz, s, SSM_HEADS, SSM_HEAD_DIM)
    rep = SSM_HEADS // SSM_GROUPS
    bm = jnp.repeat(bm.reshape(bsz, s, SSM_GROUPS, SSM_STATE), rep, axis=2)
    cm = jnp.repeat(cm.reshape(bsz, s, SSM_GROUPS, SSM_STATE), rep, axis=2)
    dt = jax.nn.softplus(dt_raw.astype(jnp.float32) + dt_bias.astype(jnp.float32))
    a = -jnp.exp(a_log.astype(jnp.float32))
    y = ssd_chunked(xs, dt, a, bm, cm) + xs * d_skip.astype(jnp.float32)[:, None]
    gsz = SSM_INNER // SSM_GROUPS
    y = y.reshape(bsz, s, SSM_GROUPS, gsz) * jax.nn.silu(z.astype(jnp.float32)).reshape(bsz, s, SSM_GROUPS, gsz)
    y = rms_norm(y, norm_g.reshape(SSM_GROUPS, gsz))
    return y.reshape(bsz, s, SSM_INNER)


def stick_breaking_attention(q, k, v):
    bsz, s, h, d = q.shape
    nblk = s // SB_BLOCK
    blk = SB_BLOCK

    def to_blocks(t):
        return t.astype(jnp.float32).reshape(bsz, nblk, blk, h, d).transpose(0, 3, 1, 2, 4)

    qb = to_blocks(q) * (d ** -0.5)
    kb = to_blocks(k)
    vb = to_blocks(v)
    idx = jnp.arange(blk)
    suffix = (idx[:, None] > idx[None, :]).astype(jnp.float32)
    diag_mask = idx[None, :] < idx[:, None]
    acc = jnp.zeros((bsz, h, nblk, blk), jnp.float32)
    out = jnp.zeros((bsz, h, nblk, blk, d), jnp.float32)
    for o in range(nblk):
        n = nblk - o
        z = jnp.einsum("bhnqd,bhnkd->bhnqk", qb[:, :, o:], kb[:, :, :n])
        log_beta = jax.nn.log_sigmoid(z)
        log_keep = log_beta - z
        if o == 0:
            log_keep = jnp.where(diag_mask, log_keep, 0.0)
        after = jnp.einsum("bhnqj,js->bhnqs", log_keep, suffix) + acc[:, :, o:, :, None]
        att = jnp.exp(log_beta + after)
        if o == 0:
            att = jnp.where(diag_mask, att, 0.0)
        out = out.at[:, :, o:].add(jnp.einsum("bhnqk,bhnkd->bhnqd", att, vb[:, :, :n]))
        acc = acc.at[:, :, o:].add(jnp.sum(log_keep, axis=-1))
    return out.transpose(0, 2, 3, 1, 4).reshape(bsz, s, h, d)


def stick_breaking_branch(qkv, q_norm_g, k_norm_g):
    bsz, s, _ = qkv.shape
    q, k, v = [t.reshape(bsz, s, SB_HEADS, SB_HEAD_DIM) for t in jnp.split(qkv, 3, axis=-1)]
    q = rms_norm(q, q_norm_g)
    k = rms_norm(k, k_norm_g)
    return stick_breaking_attention(q, k, v).reshape(bsz, s, SB_INNER)


def gated_delta_chunked(q, k, v, g, beta):
    bsz, s, h, dk = q.shape
    dv = v.shape[-1]
    nc = s // GDN_CHUNK
    cs = GDN_CHUNK

    def to_chunks(t):
        return t.reshape(bsz, nc, cs, h, t.shape[-1]).transpose(0, 3, 1, 2, 4)

    q = to_chunks(q * dk ** -0.5)
    k = to_chunks(k)
    v = to_chunks(v)
    g = g.reshape(bsz, nc, cs, h).transpose(0, 3, 1, 2)
    beta = beta.reshape(bsz, nc, cs, h).transpose(0, 3, 1, 2)
    g_cum = jnp.cumsum(g, axis=-1)
    tril_incl = jnp.tril(jnp.ones((cs, cs), bool))
    strict = jnp.tril(jnp.ones((cs, cs), bool), k=-1)
    decay = jnp.exp(jnp.where(tril_incl, g_cum[..., :, None] - g_cum[..., None, :], -jnp.inf))
    k_beta = k * beta[..., None]
    v_beta = v * beta[..., None]
    m = jnp.where(strict, jnp.einsum("bhncd,bhnsd->bhncs", k_beta, k) * decay, 0.0)
    t_mat = m + jnp.eye(cs, dtype=m.dtype)
    rhs = jnp.concatenate([v_beta, k_beta * jnp.exp(g_cum)[..., None]], axis=-1)
    sol = lax.linalg.triangular_solve(t_mat, rhs, left_side=True, lower=True, unit_diagonal=True)
    u, w = sol[..., :dv], sol[..., dv:]
    attn = jnp.where(tril_incl, jnp.einsum("bhncd,bhnsd->bhncs", q, k) * decay, 0.0)
    q_dec = q * jnp.exp(g_cum)[..., None]
    k_dec = k * jnp.exp(g_cum[..., -1:] - g_cum)[..., None]
    chunk_dec = jnp.exp(g_cum[..., -1])

    def step(state, inp):
        qd, kd, u_c, w_c, a_c, dec = inp
        v_new = u_c - jnp.einsum("bhcd,bhde->bhce", w_c, state)
        o = jnp.einsum("bhcd,bhde->bhce", qd, state) + jnp.einsum("bhcs,bhse->bhce", a_c, v_new)
        state = state * dec[..., None, None] + jnp.einsum("bhcd,bhce->bhde", kd, v_new)
        return state, o

    front = lambda t: t.transpose(2, 0, 1, 3, 4)
    s0 = jnp.zeros((bsz, h, dk, dv), q.dtype)
    _, o = lax.scan(step, s0, (front(q_dec), front(k_dec), front(u), front(w), front(attn), chunk_dec.transpose(2, 0, 1)))
    return o.transpose(1, 0, 3, 2, 4).reshape(bsz, s, h, dv)


def gated_deltanet_branch(qkv, a_raw, b_raw, gate, conv_w, a_log, dt_bias, norm_g):
    bsz, s, _ = qkv.shape
    qkv = jax.nn.silu(causal_conv(qkv, conv_w)).astype(jnp.float32)
    q, k, v = [t.reshape(bsz, s, GDN_HEADS, GDN_HEAD_DIM) for t in jnp.split(qkv, 3, axis=-1)]
    q = l2_norm(q)
    k = l2_norm(k)
    beta = jax.nn.sigmoid(b_raw.astype(jnp.float32))
    g = -jnp.exp(a_log.astype(jnp.float32)) * jax.nn.softplus(a_raw.astype(jnp.float32) + dt_bias.astype(jnp.float32))
    o = gated_delta_chunked(q, k, v, g, beta)
    o = rms_norm(o, norm_g) * jax.nn.silu(gate.astype(jnp.float32).reshape(bsz, s, GDN_HEADS, GDN_HEAD_DIM))
    return o.reshape(bsz, s, GDN_INNER)


def hier_moe(h, w_group, b_group, w_router, b_router, w_gate, w_up, w_down):
    n, d = h.shape
    hf = h.astype(jnp.float32)
    group_logits = hf @ w_group.astype(jnp.float32) + b_group.astype(jnp.float32)
    group_prob = jax.nn.softmax(group_logits, axis=-1)
    g_sel = jnp.argmax(group_logits, axis=-1)
    p_group = jnp.take_along_axis(group_prob, g_sel[:, None], axis=1)[:, 0]
    exp_logits = (hf @ w_router.astype(jnp.float32) + b_router.astype(jnp.float32)).reshape(n, MOE_GROUPS, EXPERTS_PER_GROUP)
    in_group = jnp.take_along_axis(exp_logits, g_sel[:, None, None], axis=1)[:, 0]
    top_p, top_i = lax.top_k(jax.nn.softmax(in_group, axis=-1), MOE_TOP_K)
    weights = p_group[:, None] * top_p / jnp.sum(top_p, axis=-1, keepdims=True)
    expert_ids = (g_sel[:, None] * EXPERTS_PER_GROUP + top_i).astype(jnp.int32)
    m = n * MOE_TOP_K
    flat_e = expert_ids.reshape(m)
    flat_tok = jnp.repeat(jnp.arange(n, dtype=jnp.int32), MOE_TOP_K)
    flat_w = weights.reshape(m)
    order = jnp.argsort(flat_e)
    sorted_e = flat_e[order]
    counts = jnp.zeros((N_EXPERTS,), jnp.int32).at[flat_e].add(1)
    padded = (counts + MOE_BLOCK - 1) // MOE_BLOCK * MOE_BLOCK
    start = jnp.cumsum(counts) - counts
    pend = jnp.cumsum(padded)
    pstart = pend - padded
    dest = pstart[sorted_e] + (jnp.arange(m, dtype=jnp.int32) - start[sorted_e])
    n_blocks = -(-m // MOE_BLOCK) + N_EXPERTS
    slot_tok = jnp.full((n_blocks * MOE_BLOCK,), n, jnp.int32).at[dest].set(flat_tok[order])
    slot_w = jnp.zeros((n_blocks * MOE_BLOCK,), jnp.float32).at[dest].set(flat_w[order])
    block_expert = jnp.minimum(jnp.searchsorted(pend, jnp.arange(n_blocks, dtype=jnp.int32) * MOE_BLOCK, side="right"), N_EXPERTS - 1)
    h_pad = jnp.concatenate([h, jnp.zeros((1, d), h.dtype)], axis=0)
    xb = h_pad[slot_tok].reshape(n_blocks, MOE_BLOCK, d)

    def expert_block(args):
        xi, e = args
        return (jax.nn.silu(xi @ w_gate[e]) * (xi @ w_up[e])) @ w_down[e]

    yb = lax.map(expert_block, (xb, block_expert)).reshape(n_blocks * MOE_BLOCK, d)
    y = jax.ops.segment_sum(yb * slot_w[:, None].astype(yb.dtype), slot_tok, num_segments=n + 1)
    return y[:n]


def setup_inputs(seed: int = 0) -> dict:
    key = jax.random.key(seed)
    ks = jax.random.split(key, 32)
    f32 = jnp.float32
    L, D = DEPTH, D_MODEL

    def nrm(k, shape, scale):
        return jax.random.normal(k, shape, f32) * scale

    def gain(k, shape):
        return 1.0 + 0.02 * jax.random.normal(k, shape, f32)

    def dt_bias_init(k, shape):
        dt = jnp.exp(jax.random.uniform(k, shape, f32, math.log(1e-3), math.log(1e-1)))
        return dt + jnp.log(-jnp.expm1(-dt))

    return {
        "x": nrm(ks[0], (BATCH, SEQ, D), 1.0),
        "c": nrm(ks[1], (BATCH, D), 1.0),
        "w_ada": nrm(ks[2], (L, D, 6 * D), 0.5 * D ** -0.5),
        "b_ada": nrm(ks[3], (L, 6 * D), 0.02),
        "norm_mix": gain(ks[4], (L, D)),
        "norm_ffn": gain(ks[5], (L, D)),
        "w_in": nrm(ks[6], (L, D, IN_COLS), D ** -0.5),
        "ssm_conv_w": nrm(ks[7], (L, SSM_CONV, SSM_XBC), SSM_CONV ** -0.5),
        "ssm_conv_b": nrm(ks[8], (L, SSM_XBC), 0.01),
        "ssm_dt_bias": dt_bias_init(ks[9], (L, SSM_HEADS)),
        "ssm_a_log": jnp.log(jax.random.uniform(ks[10], (L, SSM_HEADS), f32, 1.0, 16.0)),
        "ssm_d": gain(ks[11], (L, SSM_HEADS)),
        "ssm_norm": gain(ks[12], (L, SSM_INNER)),
        "sb_q_norm": gain(ks[13], (L, SB_HEAD_DIM)),
        "sb_k_norm": gain(ks[14], (L, SB_HEAD_DIM)),
        "gdn_conv_w": nrm(ks[15], (L, GDN_CONV, 3 * GDN_INNER), GDN_CONV ** -0.5),
        "gdn_a_log": jnp.log(jax.random.uniform(ks[16], (L, GDN_HEADS), f32, 1.0, 16.0)),
        "gdn_dt_bias": dt_bias_init(ks[17], (L, GDN_HEADS)),
        "gdn_norm": gain(ks[18], (L, GDN_HEAD_DIM)),
        "w_branch": nrm(ks[19], (L, N_BRANCH, BRANCH_WIDTH, D), BRANCH_WIDTH ** -0.5),
        "w_out": nrm(ks[20], (L, D, D), D ** -0.5),
        "w_group": nrm(ks[21], (L, D, MOE_GROUPS), D ** -0.5),
        "b_group": nrm(ks[22], (L, MOE_GROUPS), 0.01),
        "w_router": nrm(ks[23], (L, D, N_EXPERTS), D ** -0.5),
        "b_router": nrm(ks[24], (L, N_EXPERTS), 0.01),
        "w_gate": nrm(ks[25], (L, N_EXPERTS, D, EXPERT_FF), D ** -0.5),
        "w_up": nrm(ks[26], (L, N_EXPERTS, D, EXPERT_FF), D ** -0.5),
        "w_down": nrm(ks[27], (L, N_EXPERTS, EXPERT_FF, D), EXPERT_FF ** -0.5),
    }


def reference(x, c, w_ada, b_ada, norm_mix, norm_ffn, w_in, ssm_conv_w, ssm_conv_b, ssm_dt_bias, ssm_a_log, ssm_d,
              ssm_norm, sb_q_norm, sb_k_norm, gdn_conv_w, gdn_a_log, gdn_dt_bias, gdn_norm, w_branch, w_out,
              w_group, b_group, w_router, b_router, w_gate, w_up, w_down):
    bsz, s, d = x.shape
    split_points = np.cumsum(IN_SPLITS)[:-1].tolist()
    c_act = jax.nn.silu(c)
    for l in range(DEPTH):
        mod = c_act @ w_ada[l] + b_ada[l]
        shift_m, scale_m, gate_m, shift_f, scale_f, gate_f = [t[:, None, :] for t in jnp.split(mod, 6, axis=-1)]
        h = rms_norm(x, norm_mix[l]) * (1.0 + scale_m) + shift_m
        proj = h @ w_in[l]
        m_z, m_xbc, m_dt, sb_qkv, g_qkv, g_a, g_b, g_gate, br_gate = jnp.split(proj, split_points, axis=-1)
        y_a = mamba2_branch(m_z, m_xbc, m_dt, ssm_conv_w[l], ssm_conv_b[l], ssm_dt_bias[l], ssm_a_log[l], ssm_d[l], ssm_norm[l])
        y_b = stick_breaking_branch(sb_qkv, sb_q_norm[l], sb_k_norm[l])
        y_c = gated_deltanet_branch(g_qkv, g_a, g_b, g_gate, gdn_conv_w[l], gdn_a_log[l], gdn_dt_bias[l], gdn_norm[l])
        branches = jnp.stack([y_a.astype(x.dtype), y_b.astype(x.dtype), y_c.astype(x.dtype)], axis=2)
        branch_proj = jnp.einsum("bsie,ied->bsid", branches, w_branch[l])
        gates = jax.nn.sigmoid(br_gate.astype(jnp.float32)).astype(x.dtype).reshape(bsz, s, N_BRANCH, d)
        merged = jnp.sum(gates * branch_proj, axis=2)
        x = x + gate_m * (merged @ w_out[l])
        h = rms_norm(x, norm_ffn[l]) * (1.0 + scale_f) + shift_f
        y = hier_moe(h.reshape(bsz * s, d), w_group[l], b_group[l], w_router[l], b_router[l], w_gate[l], w_up[l], w_down[l])
        x = x + gate_f * y.reshape(bsz, s, d)
    return x
```

```python
import functools

import jax
import jax.numpy as jnp
from jax import lax
from jax.experimental import pallas as pl
from jax.experimental.pallas import tpu as pltpu

F32 = jnp.float32
BF16 = jnp.bfloat16
EPS = 1e-6

D_MODEL = 1024
SSM_HEADS = 8
SSM_HEAD_DIM = 64
SSM_INNER = 512
SSM_GROUPS = 2
SSM_STATE = 128
SSM_XBC = 1024
SSD_CHUNK = 128
SB_HEADS = 4
SB_HEAD_DIM = 128
SB_BLOCK = 128
GDN_HEADS = 4
GDN_HEAD_DIM = 128
GDN_CHUNK = 64
MOE_GROUPS = 4
EXPERTS_PER_GROUP = 8
N_EXPERTS = 32
MOE_TOP_K = 2
EXPERT_FF = 512

LANES = 128
COL_BR = 0
COL_SB = 3072
COL_GQKV = 4608
COL_XBC = 6144
COL_Z = 7168
COL_GGATE = 7680
COL_DT = 8192
COL_GAB = 8320
PROJ_COLS = 8448


def _col_spec(tb, width, col):
    assert col % width == 0
    return pl.BlockSpec((tb, width), lambda i: (i, col // width))

VMEM_LIMIT = 48 * 1024 * 1024
SB_SKIP_LOG = -110.0


def _bf(x):
    return x.astype(BF16)


def _dot(a, b):
    return jnp.dot(a, b, preferred_element_type=F32)


def _dot_nt(a, b):
    return lax.dot_general(a, b, (((1,), (1,)), ((), ())), preferred_element_type=F32)


def _dot_tn(a, b):
    return lax.dot_general(a, b, (((0,), (0,)), ((), ())), preferred_element_type=F32)


def _dot_hp(a, b):
    return jnp.dot(a, b, preferred_element_type=F32, precision=lax.Precision.HIGHEST)


def _split2(x):
    hi = _bf(x)
    lo = _bf(x - hi.astype(F32))
    return hi, lo


def _dot3(a, b):
    ah, al = _split2(a)
    bh, bl = _split2(b)
    return _dot(ah, bh) + (_dot(ah, bl) + _dot(al, bh))


def _silu(x):
    return x * jax.nn.sigmoid(x)


def _softplus(x):
    return jnp.maximum(x, 0.0) + jnp.log1p(jnp.exp(-jnp.abs(x)))


def _iota2(shape, dim):
    return lax.broadcasted_iota(jnp.int32, shape, dim)


def _params(*sem):
    return pltpu.CompilerParams(dimension_semantics=sem, vmem_limit_bytes=VMEM_LIMIT)


def _mod_kernel(c_ref, w_ref, b_ref, o_ref):
    c = _silu(c_ref[...])
    o_ref[0] = _dot_hp(c, w_ref[0]) + b_ref[0]


def _adaln_mod(c, w_ada, b_ada):
    depth, d, cols = w_ada.shape
    tn = 1024
    c8 = jnp.broadcast_to(c, (8, d))
    out = pl.pallas_call(
        _mod_kernel,
        grid=(depth, cols // tn),
        in_specs=[
            pl.BlockSpec((8, d), lambda l, j: (0, 0)),
            pl.BlockSpec((1, d, tn), lambda l, j: (l, 0, j)),
            pl.BlockSpec((1, 1, tn), lambda l, j: (l, 0, j)),
        ],
        out_specs=pl.BlockSpec((1, 8, tn), lambda l, j: (l, 0, j)),
        out_shape=jax.ShapeDtypeStruct((depth, 8, cols), F32),
        compiler_params=_params("arbitrary", "arbitrary"),
        name="adaln_mod",
    )(c8, w_ada, b_ada.reshape(depth, 1, cols))
    return out[:, 0, :]


def _norm_mod(x, g, scale, shift):
    y = x * lax.rsqrt(jnp.mean(x * x, axis=-1, keepdims=True) + EPS)
    return (y * g) * (1.0 + scale) + shift


def _inproj_kernel(x_ref, g_ref, sc_ref, sh_ref, w_ref, o_ref, h_scr):
    @pl.when(pl.program_id(1) == 0)
    def _():
        h_scr[...] = _bf(_norm_mod(x_ref[...], g_ref[...], sc_ref[...], sh_ref[...]))

    o_ref[...] = _dot(h_scr[...], w_ref[...])


def _inproj(x2, g, scale, shift, w_bf):
    s, d = x2.shape
    cols = w_bf.shape[1]
    tm = min(512, s)
    tn = 1408
    row = lambda a: a.reshape(1, d)
    vec = pl.BlockSpec((1, d), lambda i, j: (0, 0))
    return pl.pallas_call(
        _inproj_kernel,
        grid=(s // tm, cols // tn),
        in_specs=[pl.BlockSpec((tm, d), lambda i, j: (i, 0)), vec, vec, vec,
                  pl.BlockSpec((d, tn), lambda i, j: (0, j))],
        out_specs=pl.BlockSpec((tm, tn), lambda i, j: (i, j)),
        out_shape=jax.ShapeDtypeStruct((s, cols), F32),
        scratch_shapes=[pltpu.VMEM((tm, d), BF16)],
        compiler_params=_params("arbitrary", "arbitrary"),
        name="inproj",
    )(x2, row(g), row(scale), row(shift), w_bf)


def _causal_conv4(x, tail, w_ref):
    tb = x.shape[0]
    ext = jnp.concatenate([tail, x], axis=0)
    y = x * w_ref[3:4, :]
    for k in (1, 2, 3):
        y = y + pltpu.roll(ext, k, axis=0)[8:8 + tb] * w_ref[3 - k:4 - k, :]
    return y


def _ssd_kernel(z_ref, xbc_ref, dt_ref, dtt_ref, cw_ref, cb_ref, dtb_ref, dtbt_ref, al_ref, alt_ref,
                dsk_ref, ng_ref, o_ref, tail_scr, act_scr, y_scr, st_scr):
    tb = xbc_ref.shape[0]
    L = SSD_CHUNK
    P = SSM_HEAD_DIM

    @pl.when(pl.program_id(0) == 0)
    def _():
        tail_scr[...] = jnp.zeros_like(tail_scr)
        st_scr[...] = jnp.zeros_like(st_scr)

    xbc = xbc_ref[...]
    act_scr[...] = _silu(_causal_conv4(xbc, tail_scr[...], cw_ref) + cb_ref[...])
    tail_scr[...] = xbc[tb - 8:tb]

    ri = _iota2((L, L), 0)
    ci = _iota2((L, L), 1)
    tril = (ri >= ci).astype(F32)
    causal = ri >= ci
    a_col = -jnp.exp(al_ref[...])
    a_row = -jnp.exp(alt_ref[...])

    for c in range(tb // L):
        rows = slice(c * L, (c + 1) * L)
        dt = _softplus(dt_ref[rows, :] + dtb_ref[...])
        dtt = _softplus(dtt_ref[:, rows] + dtbt_ref[...])
        acum = _dot_hp(tril, dt * a_col)
        acum_t = _dot_hp(dtt * a_row, tril.T)
        act = act_scr[rows, :]
        for g in range(SSM_GROUPS):
            bm = act[:, SSM_INNER + g * SSM_STATE:SSM_INNER + (g + 1) * SSM_STATE]
            cm = act[:, SSM_INNER + SSM_GROUPS * SSM_STATE + g * SSM_STATE:
                     SSM_INNER + SSM_GROUPS * SSM_STATE + (g + 1) * SSM_STATE]
            bm_b = _bf(bm)
            gmat = _dot_nt(_bf(cm), bm_b)
            for hh in range(SSM_HEADS // SSM_GROUPS):
                h = g * (SSM_HEADS // SSM_GROUPS) + hh
                ac = acum[:, h:h + 1]
                ar = acum_t[h:h + 1, :]
                a_last = acum_t[h:h + 1, L - 1:L]
                seg = jnp.exp(jnp.where(causal, ac - ar, -jnp.inf))
                xs = act[:, h * P:(h + 1) * P]
                xdt = xs * dt[:, h:h + 1]
                y = _dot(_bf(gmat * seg), _bf(xdt))
                st = st_scr[h]
                y = y + _dot_nt(_bf(cm * jnp.exp(ac)), _bf(st))
                new = _dot_tn(_bf(xdt * jnp.exp(a_last - ac)), bm_b)
                st_scr[h] = st * jnp.exp(a_last) + new
                y_scr[rows, h * P:(h + 1) * P] = y + xs * dsk_ref[:, h * P:(h + 1) * P]

    y = y_scr[...] * _silu(z_ref[...])
    gsz = SSM_INNER // SSM_GROUPS
    outs = []
    for g in range(SSM_GROUPS):
        yg = y[:, g * gsz:(g + 1) * gsz]
        yn = yg * lax.rsqrt(jnp.mean(yg * yg, axis=-1, keepdims=True) + EPS)
        outs.append(yn * ng_ref[:, g * gsz:(g + 1) * gsz])
    o_ref[...] = jnp.concatenate(outs, axis=-1)


def _ssd(proj, dtt, conv_w, conv_b, dt_bias, a_log, d_skip, norm_g):
    s = proj.shape[0]
    tb = min(256, s)
    pad8 = lambda v: jnp.pad(v, (0, LANES - v.shape[0])).reshape(1, LANES)
    colv = lambda v: v.reshape(SSM_HEADS, 1)
    full = lambda shape: pl.BlockSpec(shape, lambda i: (0,) * len(shape))
    return pl.pallas_call(
        _ssd_kernel,
        grid=(s // tb,),
        in_specs=[
            _col_spec(tb, SSM_INNER, COL_Z),
            _col_spec(tb, SSM_XBC, COL_XBC),
            _col_spec(tb, LANES, COL_DT),
            pl.BlockSpec((SSM_HEADS, tb), lambda i: (0, i)),
            full((4, SSM_XBC)), full((1, SSM_XBC)), full((1, LANES)), full((SSM_HEADS, 1)),
            full((1, LANES)), full((SSM_HEADS, 1)), full((1, SSM_INNER)), full((1, SSM_INNER)),
        ],
        out_specs=pl.BlockSpec((tb, SSM_INNER), lambda i: (i, 0)),
        out_shape=jax.ShapeDtypeStruct((s, SSM_INNER), F32),
        scratch_shapes=[
            pltpu.VMEM((8, SSM_XBC), F32),
            pltpu.VMEM((tb, SSM_XBC), F32),
            pltpu.VMEM((tb, SSM_INNER), F32),
            pltpu.VMEM((SSM_HEADS, SSM_HEAD_DIM, SSM_STATE), F32),
        ],
        compiler_params=_params("arbitrary"),
        name="ssd",
    )(proj, proj, proj, dtt, conv_w, conv_b.reshape(1, -1), pad8(dt_bias), colv(dt_bias),
      pad8(a_log), colv(a_log), jnp.repeat(d_skip, SSM_HEAD_DIM).reshape(1, -1), norm_g.reshape(1, -1))


def _layout_w_in(w):
    d = w.shape[0]
    z, xbc, dt, sb, gqkv, gab, ggate, br = jnp.split(w, [512, 1536, 1544, 3080, 4616, 4624, 5136], axis=1)
    pad = jnp.zeros((d, LANES - 8), w.dtype)
    return jnp.concatenate([br, sb, gqkv, xbc, z, ggate, dt, pad, gab, pad], axis=1).astype(BF16)


def _head_rms(x, g):
    return (x * lax.rsqrt(jnp.mean(x * x, axis=-1, keepdims=True) + EPS)) * g


def _sb_block(q, kn_b, v_b, acc, suffix, masked):
    z = _dot_nt(q, kn_b)
    sp = jnp.maximum(z, 0.0) + jnp.log1p(jnp.exp(-jnp.abs(z)))
    log_keep = -sp
    log_beta = z - sp
    if masked:
        blk = z.shape[0]
        strict = _iota2((blk, blk), 1) < _iota2((blk, blk), 0)
        log_keep = jnp.where(strict, log_keep, 0.0)
    hi, lo = _split2(log_keep)
    after = (_dot(hi, suffix) + _dot(lo, suffix)) + acc
    att = jnp.exp(log_beta + after)
    if masked:
        att = jnp.where(strict, att, 0.0)
    return _dot(_bf(att), v_b), jnp.sum(log_keep, axis=-1, keepdims=True)


def _sb_kernel(q_ref, k_ref, v_ref, qg_ref, kg_ref, o_ref, acc_scr, out_scr):
    blk = SB_BLOCK
    i = pl.program_id(1)
    q = _bf(_head_rms(q_ref[...], qg_ref[...]) * (SB_HEAD_DIM ** -0.5))
    suffix = (_iota2((blk, blk), 0) > _iota2((blk, blk), 1)).astype(BF16)

    def load_kv(j):
        rows = pl.ds(pl.multiple_of(j * blk, blk), blk)
        return _bf(_head_rms(k_ref[rows, :], kg_ref[...])), _bf(v_ref[rows, :])

    kn_b, v_b = load_kv(i)
    out0, rs0 = _sb_block(q, kn_b, v_b, jnp.zeros((blk, 1), F32), suffix, True)
    out_scr[...] = out0
    acc_scr[...] = rs0

    def cond(carry):
        j, live = carry
        return jnp.logical_and(j >= 0, live > 0)

    def body(carry):
        j, _ = carry
        kn_j, v_j = load_kv(j)
        acc = acc_scr[...]
        out_j, rs = _sb_block(q, kn_j, v_j, acc, suffix, False)
        out_scr[...] += out_j
        acc = acc + rs
        acc_scr[...] = acc
        return j - 1, (jnp.max(acc) > SB_SKIP_LOG).astype(jnp.int32)

    lax.while_loop(cond, body, (i - 1, (jnp.max(rs0) > SB_SKIP_LOG).astype(jnp.int32)))
    o_ref[...] = out_scr[...]


def _stick_breaking(proj, q_g, k_g):
    s = proj.shape[0]
    blk = SB_BLOCK
    dh = SB_HEAD_DIM
    cq = COL_SB // dh
    ck = cq + SB_HEADS
    cv = ck + SB_HEADS
    vec = pl.BlockSpec((1, dh), lambda h, i: (0, 0))
    return pl.pallas_call(
        _sb_kernel,
        grid=(SB_HEADS, s // blk),
        in_specs=[
            pl.BlockSpec((blk, dh), lambda h, i: (i, cq + h)),
            pl.BlockSpec((s, dh), lambda h, i: (0, ck + h)),
            pl.BlockSpec((s, dh), lambda h, i: (0, cv + h)),
            vec, vec,
        ],
        out_specs=pl.BlockSpec((blk, dh), lambda h, i: (i, h)),
        out_shape=jax.ShapeDtypeStruct((s, SB_HEADS * dh), F32),
        scratch_shapes=[pltpu.VMEM((blk, 1), F32), pltpu.VMEM((blk, dh), F32)],
        compiler_params=_params("arbitrary", "arbitrary"),
        name="stick_breaking",
    )(proj, proj, proj, q_g.reshape(1, dh), k_g.reshape(1, dh))


def _dot3_nt(a, b):
    ah, al = _split2(a)
    bh, bl = _split2(b)
    return _dot_nt(ah, bh) + (_dot_nt(ah, bl) + _dot_nt(al, bh))


def _unit_lower_inverse(m):
    n = m.shape[0]
    eye = (_iota2((n, n), 0) == _iota2((n, n), 1)).astype(F32)
    p = -m
    inv = eye + p
    steps = max(1, (n - 1).bit_length()) - 1
    for _ in range(steps):
        p = _dot3(p, p)
        inv = inv + _dot3(inv, p)
    return inv


def _gdn_kernel(qkv_ref, gab_ref, gabt_ref, gate_ref, cw_ref, al_ref, alt_ref, dtb_ref, dtbt_ref, ng_ref,
                o_ref, tail_scr, act_scr, st_scr):
    tb = qkv_ref.shape[0]
    C = GDN_CHUNK
    dh = GDN_HEAD_DIM
    inner = GDN_HEADS * dh

    @pl.when(pl.program_id(0) == 0)
    def _():
        tail_scr[...] = jnp.zeros_like(tail_scr)
        st_scr[...] = jnp.zeros_like(st_scr)

    raw = qkv_ref[...]
    act_scr[...] = _silu(_causal_conv4(raw, tail_scr[...], cw_ref))
    tail_scr[...] = raw[tb - 8:tb]

    ri = _iota2((C, C), 0)
    ci = _iota2((C, C), 1)
    tril = (ri >= ci).astype(F32)
    incl = ri >= ci
    strict = ri > ci
    gab = gab_ref[...]
    g_col = -jnp.exp(al_ref[...]) * _softplus(gab + dtb_ref[...])
    beta_col = jax.nn.sigmoid(gab)
    g_row = -jnp.exp(alt_ref[...]) * _softplus(gabt_ref[...] + dtbt_ref[...])

    for c in range(tb // C):
        rows = slice(c * C, (c + 1) * C)
        gc_col = _dot_hp(tril, g_col[rows, :])
        gc_row = _dot_hp(g_row[:, rows], tril.T)
        for h in range(GDN_HEADS):
            q = act_scr[rows, h * dh:(h + 1) * dh]
            k = act_scr[rows, inner + h * dh:inner + (h + 1) * dh]
            v = act_scr[rows, 2 * inner + h * dh:2 * inner + (h + 1) * dh]
            q = q * lax.rsqrt(jnp.sum(q * q, axis=-1, keepdims=True) + EPS) * (dh ** -0.5)
            k = k * lax.rsqrt(jnp.sum(k * k, axis=-1, keepdims=True) + EPS)
            beta = beta_col[rows, GDN_HEADS + h:GDN_HEADS + h + 1]
            gc = gc_col[:, h:h + 1]
            gr = gc_row[h:h + 1, :]
            g_last = gc_row[h:h + 1, C - 1:C]
            decay = jnp.exp(jnp.where(incl, gc - gr, -jnp.inf))
            kb = k * beta
            m = jnp.where(strict, _dot3_nt(kb, k) * decay, 0.0)
            inv = _unit_lower_inverse(m)
            sol = _dot3(inv, jnp.concatenate([v * beta, kb * jnp.exp(gc)], axis=-1))
            u = sol[:, :dh]
            w = sol[:, dh:]
            attn = jnp.where(incl, _dot_nt(_bf(q), _bf(k)) * decay, 0.0)
            st = st_scr[h]
            st_b = _bf(st)
            v_new = u - _dot(_bf(w), st_b)
            v_new_b = _bf(v_new)
            o = _dot(_bf(q * jnp.exp(gc)), st_b) + _dot(_bf(attn), v_new_b)
            st_scr[h] = st * jnp.exp(g_last) + _dot_tn(_bf(k * jnp.exp(g_last - gc)), v_new_b)
            o = _head_rms(o, ng_ref[...]) * _silu(gate_ref[rows, h * dh:(h + 1) * dh])
            o_ref[rows, h * dh:(h + 1) * dh] = o


def _gdn(proj, gabt, conv_w, a_log, dt_bias, norm_g):
    s = proj.shape[0]
    tb = min(256, s)
    inner = GDN_HEADS * GDN_HEAD_DIM
    pad_lane = lambda v: jnp.pad(v, (0, LANES - v.shape[0])).reshape(1, LANES)
    pad_col = lambda v: jnp.pad(v, (0, 8 - v.shape[0])).reshape(8, 1)
    full = lambda shape: pl.BlockSpec(shape, lambda i: (0,) * len(shape))
    return pl.pallas_call(
        _gdn_kernel,
        grid=(s // tb,),
        in_specs=[
            _col_spec(tb, 3 * inner, COL_GQKV),
            _col_spec(tb, LANES, COL_GAB),
            pl.BlockSpec((8, tb), lambda i: (0, i)),
            _col_spec(tb, inner, COL_GGATE),
            full((4, 3 * inner)), full((1, LANES)), full((8, 1)), full((1, LANES)), full((8, 1)),
            full((1, GDN_HEAD_DIM)),
        ],
        out_specs=pl.BlockSpec((tb, inner), lambda i: (i, 0)),
        out_shape=jax.ShapeDtypeStruct((s, inner), F32),
        scratch_shapes=[
            pltpu.VMEM((8, 3 * inner), F32),
            pltpu.VMEM((tb, 3 * inner), F32),
            pltpu.VMEM((GDN_HEADS, GDN_HEAD_DIM, GDN_HEAD_DIM), F32),
        ],
        compiler_params=_params("arbitrary"),
        name="gdn",
    )(proj, proj, gabt, proj, conv_w, pad_lane(a_log), pad_col(a_log), pad_lane(dt_bias), pad_col(dt_bias),
      norm_g.reshape(1, -1))


def _merge_kernel(ya_ref, yb_ref, yc_ref, br_ref, x_ref, wbr_ref, wout_ref, gm_ref, g_ref, sc_ref, sh_ref,
                  wrt_ref, brt_ref, xo_ref, h_ref, lg_ref):
    d = x_ref.shape[1]
    merged = None
    for i, y_ref in enumerate((ya_ref, yb_ref, yc_ref)):
        gate = jax.nn.sigmoid(br_ref[:, i * d:(i + 1) * d])
        term = gate * _dot(_bf(y_ref[...]), wbr_ref[i])
        merged = term if merged is None else merged + term
    x_new = x_ref[...] + gm_ref[...] * _dot(_bf(merged), wout_ref[...])
    xo_ref[...] = x_new
    h = _norm_mod(x_new, g_ref[...], sc_ref[...], sh_ref[...])
    h_ref[...] = h
    lg_ref[...] = _dot3(h, wrt_ref[...]) + brt_ref[...]


def _merge(ya, yb, yc, proj, x2, wbr_bf, wout_bf, gate_m, g, scale, shift, w_rt, b_rt):
    s, d = x2.shape
    tb = min(256, s)
    bw = ya.shape[1]
    row = lambda a: a.reshape(1, -1)
    vec = pl.BlockSpec((1, d), lambda i: (0, 0))
    blk = lambda w: pl.BlockSpec((tb, w), lambda i: (i, 0))
    return pl.pallas_call(
        _merge_kernel,
        grid=(s // tb,),
        in_specs=[blk(bw), blk(bw), blk(bw), _col_spec(tb, 3 * d, COL_BR), blk(d),
                  pl.BlockSpec((3, bw, d), lambda i: (0, 0, 0)), pl.BlockSpec((d, d), lambda i: (0, 0)),
                  vec, vec, vec, vec,
                  pl.BlockSpec((d, LANES), lambda i: (0, 0)), pl.BlockSpec((1, LANES), lambda i: (0, 0))],
        out_specs=[blk(d), blk(d), blk(LANES)],
        out_shape=[jax.ShapeDtypeStruct((s, d), F32), jax.ShapeDtypeStruct((s, d), F32),
                   jax.ShapeDtypeStruct((s, LANES), F32)],
        compiler_params=_params("arbitrary"),
        name="merge",
    )(ya, yb, yc, proj, x2, wbr_bf, wout_bf, row(gate_m), row(g), row(scale), row(shift), w_rt, row(b_rt))


ROUTE_E0 = MOE_GROUPS


def _route_kernel(lg_ref, ids_ref, wts_ref, cnt_ref, carry_scr):
    tb = lg_ref.shape[0]

    @pl.when(pl.program_id(0) == 0)
    def _():
        carry_scr[...] = jnp.zeros_like(carry_scr)

    lg = lg_ref[...]
    lane = _iota2((tb, LANES), 1)
    big = jnp.int32(LANES)
    neg = -jnp.inf
    gl = jnp.where(lane < MOE_GROUPS, lg, neg)
    gmax = jnp.max(gl, axis=-1, keepdims=True)
    g_sel = jnp.min(jnp.where(gl == gmax, lane, big), axis=-1, keepdims=True)
    p_group = 1.0 / jnp.sum(jnp.exp(gl - gmax), axis=-1, keepdims=True)
    lo = ROUTE_E0 + EXPERTS_PER_GROUP * g_sel
    el = jnp.where(jnp.logical_and(lane >= lo, lane < lo + EXPERTS_PER_GROUP), lg, neg)
    m1 = jnp.max(el, axis=-1, keepdims=True)
    i1 = jnp.min(jnp.where(el == m1, lane, big), axis=-1, keepdims=True)
    esum = jnp.sum(jnp.exp(el - m1), axis=-1, keepdims=True)
    el2 = jnp.where(lane == i1, neg, el)
    m2 = jnp.max(el2, axis=-1, keepdims=True)
    i2 = jnp.min(jnp.where(el2 == m2, lane, big), axis=-1, keepdims=True)
    p1 = 1.0 / esum
    p2 = jnp.exp(m2 - m1) / esum
    w1 = p_group * p1 / (p1 + p2)
    w2 = p_group * p2 / (p1 + p2)

    sel1 = lane == i1
    sel2 = lane == i2
    onehot = jnp.where(jnp.logical_or(sel1, sel2), 1.0, 0.0)
    before = (_iota2((tb, tb), 0) > _iota2((tb, tb), 1)).astype(BF16)
    seen = _dot(before, _bf(onehot)) + carry_scr[...]
    r1 = jnp.sum(jnp.where(sel1, seen, 0.0), axis=-1, keepdims=True)
    r2 = jnp.sum(jnp.where(sel2, seen, 0.0), axis=-1, keepdims=True)
    carry = carry_scr[...] + jnp.sum(onehot, axis=0, keepdims=True)
    carry_scr[...] = carry
    cnt_ref[...] = jnp.broadcast_to(carry, cnt_ref.shape)

    ids = jnp.where(lane == 0, i1 - ROUTE_E0, jnp.where(lane == 1, i2 - ROUTE_E0,
          jnp.where(lane == 2, r1.astype(jnp.int32), jnp.where(lane == 3, r2.astype(jnp.int32), 0))))
    ids_ref[...] = ids
    wts_ref[...] = jnp.where(lane == 0, w1, jnp.where(lane == 1, w2, 0.0))


def _route(logits):
    s = logits.shape[0]
    tb = min(256, s)
    blk = pl.BlockSpec((tb, LANES), lambda i: (i, 0))
    return pl.pallas_call(
        _route_kernel,
        grid=(s // tb,),
        in_specs=[blk],
        out_specs=[blk, blk, pl.BlockSpec((8, LANES), lambda i: (0, 0))],
        out_shape=[jax.ShapeDtypeStruct((s, LANES), jnp.int32), jax.ShapeDtypeStruct((s, LANES), F32),
                   jax.ShapeDtypeStruct((8, LANES), F32)],
        scratch_shapes=[pltpu.VMEM((1, LANES), F32)],
        compiler_params=_params("arbitrary"),
        name="route",
    )(logits)


EXPERT_BLOCK = 256
ROW_TB = 128


def _dispatch_kernel(dest_ref, h_ref, xb_in_ref, xb_ref, sem):
    del xb_in_ref
    tb = h_ref.shape[0]
    base = pl.program_id(0) * tb * MOE_TOP_K

    def row_copy(t, k, d):
        return pltpu.make_async_copy(h_ref.at[pl.ds(t, 1), :], xb_ref.at[pl.ds(d, 1), :], sem)

    def issue(t, carry):
        for k in range(MOE_TOP_K):
            row_copy(t, k, dest_ref[base + t * MOE_TOP_K + k]).start()
        return carry

    def drain(t, carry):
        for k in range(MOE_TOP_K):
            row_copy(t, k, dest_ref[base + t * MOE_TOP_K + k]).wait()
        return carry

    lax.fori_loop(0, tb, issue, 0)
    lax.fori_loop(0, tb, drain, 0)


def _dispatch(dest, h, n_slots):
    s, d = h.shape
    tb = min(ROW_TB, s)
    xb0 = jnp.zeros((n_slots, d), F32)
    return pl.pallas_call(
        _dispatch_kernel,
        grid_spec=pltpu.PrefetchScalarGridSpec(
            num_scalar_prefetch=1,
            grid=(s // tb,),
            in_specs=[pl.BlockSpec((tb, d), lambda i, dest: (i, 0)), pl.BlockSpec(memory_space=pl.ANY)],
            out_specs=pl.BlockSpec(memory_space=pl.ANY),
            scratch_shapes=[pltpu.SemaphoreType.DMA(())],
        ),
        out_shape=jax.ShapeDtypeStruct((n_slots, d), F32),
        input_output_aliases={2: 0},
        compiler_params=_params("arbitrary"),
        name="dispatch",
    )(dest, h, xb0)


def _expert_kernel(be_ref, nused_ref, x_ref, wg_ref, wu_ref, wd_ref, o_ref):
    b = pl.program_id(0)

    @pl.when(b < nused_ref[0])
    def _():
        x = _bf(x_ref[...])
        hid = _silu(_dot(x, wg_ref[0])) * _dot(x, wu_ref[0])
        o_ref[...] = _dot(_bf(hid), wd_ref[0])

    @pl.when(b >= nused_ref[0])
    def _():
        o_ref[...] = jnp.zeros_like(o_ref)


def _experts(block_expert, n_used, xb, wg_bf, wu_bf, wd_bf):
    n_slots, d = xb.shape
    ff = wg_bf.shape[2]
    nb = n_slots // EXPERT_BLOCK
    return pl.pallas_call(
        _expert_kernel,
        grid_spec=pltpu.PrefetchScalarGridSpec(
            num_scalar_prefetch=2,
            grid=(nb,),
            in_specs=[
                pl.BlockSpec((EXPERT_BLOCK, d), lambda b, be, nu: (b, 0)),
                pl.BlockSpec((1, d, ff), lambda b, be, nu: (be[b], 0, 0)),
                pl.BlockSpec((1, d, ff), lambda b, be, nu: (be[b], 0, 0)),
                pl.BlockSpec((1, ff, d), lambda b, be, nu: (be[b], 0, 0)),
            ],
            out_specs=pl.BlockSpec((EXPERT_BLOCK, d), lambda b, be, nu: (b, 0)),
        ),
        out_shape=jax.ShapeDtypeStruct((n_slots, d), F32),
        compiler_params=_params("arbitrary"),
        name="experts",
    )(block_expert, n_used, xb, wg_bf, wu_bf, wd_bf)


def _combine_kernel(dest_ref, yb_ref, wts_ref, x_ref, gf_ref, o_ref, buf, sem):
    tb = x_ref.shape[0]
    base = pl.program_id(0) * tb * MOE_TOP_K

    def row_copy(t, k, d):
        return pltpu.make_async_copy(yb_ref.at[pl.ds(d, 1), :], buf.at[k, pl.ds(t, 1), :], sem)

    def issue(t, carry):
        for k in range(MOE_TOP_K):
            row_copy(t, k, dest_ref[base + t * MOE_TOP_K + k]).start()
        return carry

    def drain(t, carry):
        for k in range(MOE_TOP_K):
            row_copy(t, k, dest_ref[base + t * MOE_TOP_K + k]).wait()
        return carry

    lax.fori_loop(0, tb, issue, 0)
    lax.fori_loop(0, tb, drain, 0)
    wts = wts_ref[...]
    y = wts[:, 0:1] * buf[0] + wts[:, 1:2] * buf[1]
    o_ref[...] = x_ref[...] + gf_ref[...] * y


def _combine(dest, yb, wts, x2, gate_f):
    s, d = x2.shape
    tb = min(ROW_TB, s)
    return pl.pallas_call(
        _combine_kernel,
        grid_spec=pltpu.PrefetchScalarGridSpec(
            num_scalar_prefetch=1,
            grid=(s // tb,),
            in_specs=[pl.BlockSpec(memory_space=pl.ANY),
                      pl.BlockSpec((tb, LANES), lambda i, dest: (i, 0)),
                      pl.BlockSpec((tb, d), lambda i, dest: (i, 0)),
                      pl.BlockSpec((1, d), lambda i, dest: (0, 0))],
            out_specs=pl.BlockSpec((tb, d), lambda i, dest: (i, 0)),
            scratch_shapes=[pltpu.VMEM((MOE_TOP_K, tb, d), F32), pltpu.SemaphoreType.DMA(())],
        ),
        out_shape=jax.ShapeDtypeStruct((s, d), F32),
        compiler_params=_params("arbitrary"),
        name="combine",
    )(dest, yb, wts, x2, gate_f.reshape(1, d))


def _moe(h, logits, x2, gate_f, wg_bf, wu_bf, wd_bf):
    s = h.shape[0]
    ids, wts, cnt = _route(logits)
    counts = cnt[0, ROUTE_E0:ROUTE_E0 + N_EXPERTS].astype(jnp.int32)
    padded = (counts + EXPERT_BLOCK - 1) // EXPERT_BLOCK * EXPERT_BLOCK
    pend = jnp.cumsum(padded)
    pstart = pend - padded
    dest = (pstart[ids[:, 0:MOE_TOP_K]] + ids[:, MOE_TOP_K:2 * MOE_TOP_K]).reshape(s * MOE_TOP_K)
    nb = (s * MOE_TOP_K) // EXPERT_BLOCK + N_EXPERTS
    block_start = jnp.arange(nb, dtype=jnp.int32) * EXPERT_BLOCK
    block_expert = jnp.minimum(jnp.searchsorted(pend, block_start, side="right"), N_EXPERTS - 1).astype(jnp.int32)
    n_used = (pend[-1:] // EXPERT_BLOCK).astype(jnp.int32)
    xb = _dispatch(dest, h, nb * EXPERT_BLOCK)
    yb = _experts(block_expert, n_used, xb, wg_bf, wu_bf, wd_bf)
    return _combine(dest, yb, wts, x2, gate_f)


def kernel(x, c, w_ada, b_ada, norm_mix, norm_ffn, w_in, ssm_conv_w, ssm_conv_b, ssm_dt_bias, ssm_a_log, ssm_d,
           ssm_norm, sb_q_norm, sb_k_norm, gdn_conv_w, gdn_a_log, gdn_dt_bias, gdn_norm, w_branch, w_out,
           w_group, b_group, w_router, b_router, w_gate, w_up, w_down):
    bsz, s, d = x.shape
    assert bsz == 1 and d == D_MODEL
    depth = w_in.shape[0]
    mod = _adaln_mod(c, w_ada, b_ada)
    x2 = x.reshape(s, d)
    for l in range(depth):
        shift_m, scale_m, gate_m, shift_f, scale_f, gate_f = jnp.split(mod[l], 6)
        proj = _inproj(x2, norm_mix[l], scale_m, shift_m, _layout_w_in(w_in[l]))
        dtt = proj[:, COL_DT:COL_DT + 8].T
        gabt = proj[:, COL_GAB:COL_GAB + 8].T
        ya = _ssd(proj, dtt, ssm_conv_w[l], ssm_conv_b[l], ssm_dt_bias[l], ssm_a_log[l], ssm_d[l], ssm_norm[l])
        yb = _stick_breaking(proj, sb_q_norm[l], sb_k_norm[l])
        yc = _gdn(proj, gabt, gdn_conv_w[l], gdn_a_log[l], gdn_dt_bias[l], gdn_norm[l])
        pad = jnp.zeros((d, LANES - MOE_GROUPS - N_EXPERTS), F32)
        w_rt = jnp.concatenate([w_group[l], w_router[l], pad], axis=1)
        b_rt = jnp.concatenate([b_group[l], b_router[l], pad[0]])
        x2, h, logits = _merge(ya, yb, yc, proj, x2, _bf(w_branch[l]), _bf(w_out[l]), gate_m,
                               norm_ffn[l], scale_f, shift_f, w_rt, b_rt)
        x2 = _moe(h, logits, x2, gate_f, _bf(w_gate[l]), _bf(w_up[l]), _bf(w_down[l]))
    return x2.reshape(bsz, s, d)
```

```python
import functools

import jax
import jax.numpy as jnp
from jax import lax
from jax.experimental import pallas as pl
from jax.experimental.pallas import tpu as pltpu

F32 = jnp.float32
BF16 = jnp.bfloat16
EPS = 1e-6

D_MODEL = 1024
SSM_HEADS = 8
SSM_HEAD_DIM = 64
SSM_INNER = 512
SSM_GROUPS = 2
SSM_STATE = 128
SSM_XBC = 1024
SSD_CHUNK = 128
SB_HEADS = 4
SB_HEAD_DIM = 128
SB_BLOCK = 128
GDN_HEADS = 4
GDN_HEAD_DIM = 128
GDN_CHUNK = 64
MOE_GROUPS = 4
EXPERTS_PER_GROUP = 8
N_EXPERTS = 32
MOE_TOP_K = 2
EXPERT_FF = 512

LANES = 128
COL_BR = 0
COL_SB = 3072
COL_GQKV = 4608
COL_XBC = 6144
COL_Z = 7168
COL_GGATE = 7680
WIDE_COLS = 8192
NCOL_DT = 0
NCOL_GAB = 128
NARROW_COLS = 256


def _col_spec(tb, width, col):
    assert col % width == 0
    return pl.BlockSpec((tb, width), lambda i: (i, col // width))

VMEM_LIMIT = 48 * 1024 * 1024
SB_SKIP_LOG = -110.0


def _bf(x):
    return x.astype(BF16)


def _dot(a, b):
    return jnp.dot(a, b, preferred_element_type=F32)


def _dot_nt(a, b):
    return lax.dot_general(a, b, (((1,), (1,)), ((), ())), preferred_element_type=F32)


def _dot_tn(a, b):
    return lax.dot_general(a, b, (((0,), (0,)), ((), ())), preferred_element_type=F32)


def _dot_hp(a, b):
    return jnp.dot(a, b, preferred_element_type=F32, precision=lax.Precision.HIGHEST)


def _split2(x):
    hi = _bf(x)
    lo = _bf(x - hi.astype(F32))
    return hi, lo


def _dot3(a, b):
    ah, al = _split2(a)
    bh, bl = _split2(b)
    return _dot(ah, bh) + (_dot(ah, bl) + _dot(al, bh))


def _silu(x):
    return x * jax.nn.sigmoid(x)


def _softplus(x):
    return jnp.maximum(x, 0.0) + jnp.log1p(jnp.exp(-jnp.abs(x)))


def _iota2(shape, dim):
    return lax.broadcasted_iota(jnp.int32, shape, dim)


def _params(*sem):
    return pltpu.CompilerParams(dimension_semantics=sem, vmem_limit_bytes=VMEM_LIMIT)


def _mod_kernel(c_ref, w_ref, b_ref, o_ref):
    c = _silu(c_ref[...])
    o_ref[0] = _dot_hp(c, w_ref[0]) + b_ref[0]


def _adaln_mod(c, w_ada, b_ada):
    depth, d, cols = w_ada.shape
    tn = 1024
    c8 = jnp.broadcast_to(c, (8, d))
    out = pl.pallas_call(
        _mod_kernel,
        grid=(depth, cols // tn),
        in_specs=[
            pl.BlockSpec((8, d), lambda l, j: (0, 0)),
            pl.BlockSpec((1, d, tn), lambda l, j: (l, 0, j)),
            pl.BlockSpec((1, 1, tn), lambda l, j: (l, 0, j)),
        ],
        out_specs=pl.BlockSpec((1, 8, tn), lambda l, j: (l, 0, j)),
        out_shape=jax.ShapeDtypeStruct((depth, 8, cols), F32),
        compiler_params=_params("arbitrary", "arbitrary"),
        name="adaln_mod",
    )(c8, w_ada, b_ada.reshape(depth, 1, cols))
    return out[:, 0, :]


def _norm_mod(x, g, scale, shift):
    y = x * lax.rsqrt(jnp.mean(x * x, axis=-1, keepdims=True) + EPS)
    return (y * g) * (1.0 + scale) + shift


def _inproj_kernel(x_ref, g_ref, sc_ref, sh_ref, w_ref, wn_ref, o_ref, on_ref, h_scr):
    @pl.when(pl.program_id(1) == 0)
    def _():
        h = _bf(_norm_mod(x_ref[...], g_ref[...], sc_ref[...], sh_ref[...]))
        h_scr[...] = h
        on_ref[...] = _dot(h, wn_ref[...])

    o_ref[...] = _bf(_dot(h_scr[...], w_ref[...]))


def _inproj(x2, g, scale, shift, w_wide, w_narrow):
    s, d = x2.shape
    tm = min(1024, s)
    tn = 1024
    row = lambda a: a.reshape(1, d)
    vec = pl.BlockSpec((1, d), lambda i, j: (0, 0))
    return pl.pallas_call(
        _inproj_kernel,
        grid=(s // tm, WIDE_COLS // tn),
        in_specs=[pl.BlockSpec((tm, d), lambda i, j: (i, 0)), vec, vec, vec,
                  pl.BlockSpec((d, tn), lambda i, j: (0, j)),
                  pl.BlockSpec((d, NARROW_COLS), lambda i, j: (0, 0))],
        out_specs=[pl.BlockSpec((tm, tn), lambda i, j: (i, j)),
                   pl.BlockSpec((tm, NARROW_COLS), lambda i, j: (i, 0))],
        out_shape=[jax.ShapeDtypeStruct((s, WIDE_COLS), BF16), jax.ShapeDtypeStruct((s, NARROW_COLS), F32)],
        scratch_shapes=[pltpu.VMEM((tm, d), BF16)],
        compiler_params=_params("arbitrary", "arbitrary"),
        name="inproj",
    )(x2, row(g), row(scale), row(shift), w_wide, w_narrow)


def _causal_conv4(x, tail, w_ref):
    tb = x.shape[0]
    ext = jnp.concatenate([tail, x], axis=0)
    y = x * w_ref[3:4, :]
    for k in (1, 2, 3):
        y = y + pltpu.roll(ext, k, axis=0)[8:8 + tb] * w_ref[3 - k:4 - k, :]
    return y


def _ssd_kernel(z_ref, xbc_ref, dt_ref, dtt_ref, cw_ref, cb_ref, dtb_ref, dtbt_ref, al_ref, alt_ref,
                dsk_ref, ng_ref, o_ref, tail_scr, act_scr, y_scr, st_scr):
    tb = xbc_ref.shape[0]
    L = SSD_CHUNK
    P = SSM_HEAD_DIM

    @pl.when(pl.program_id(0) == 0)
    def _():
        tail_scr[...] = jnp.zeros_like(tail_scr)
        st_scr[...] = jnp.zeros_like(st_scr)

    xbc = xbc_ref[...].astype(F32)
    act_scr[...] = _silu(_causal_conv4(xbc, tail_scr[...], cw_ref) + cb_ref[...])
    tail_scr[...] = xbc[tb - 8:tb]

    ri = _iota2((L, L), 0)
    ci = _iota2((L, L), 1)
    tril = (ri >= ci).astype(F32)
    causal = ri >= ci
    a_col = -jnp.exp(al_ref[...])
    a_row = -jnp.exp(alt_ref[...])

    for c in range(tb // L):
        rows = slice(c * L, (c + 1) * L)
        dt = _softplus(dt_ref[rows, :] + dtb_ref[...])
        dtt = _softplus(dtt_ref[:, rows] + dtbt_ref[...])
        acum = _dot_hp(tril, dt * a_col)
        acum_t = _dot_hp(dtt * a_row, tril.T)
        act = act_scr[rows, :]
        for g in range(SSM_GROUPS):
            bm = act[:, SSM_INNER + g * SSM_STATE:SSM_INNER + (g + 1) * SSM_STATE]
            cm = act[:, SSM_INNER + SSM_GROUPS * SSM_STATE + g * SSM_STATE:
                     SSM_INNER + SSM_GROUPS * SSM_STATE + (g + 1) * SSM_STATE]
            bm_b = _bf(bm)
            gmat = _dot_nt(_bf(cm), bm_b)
            for hh in range(SSM_HEADS // SSM_GROUPS):
                h = g * (SSM_HEADS // SSM_GROUPS) + hh
                ac = acum[:, h:h + 1]
                ar = acum_t[h:h + 1, :]
                a_last = acum_t[h:h + 1, L - 1:L]
                seg = jnp.exp(jnp.where(causal, ac - ar, -jnp.inf))
                xs = act[:, h * P:(h + 1) * P]
                xdt = xs * dt[:, h:h + 1]
                y = _dot(_bf(gmat * seg), _bf(xdt))
                st = st_scr[h]
                y = y + _dot_nt(_bf(cm * jnp.exp(ac)), _bf(st))
                new = _dot_tn(_bf(xdt * jnp.exp(a_last - ac)), bm_b)
                st_scr[h] = st * jnp.exp(a_last) + new
                y_scr[rows, h * P:(h + 1) * P] = y + xs * dsk_ref[:, h * P:(h + 1) * P]

    y = y_scr[...] * _silu(z_ref[...].astype(F32))
    gsz = SSM_INNER // SSM_GROUPS
    outs = []
    for g in range(SSM_GROUPS):
        yg = y[:, g * gsz:(g + 1) * gsz]
        yn = yg * lax.rsqrt(jnp.mean(yg * yg, axis=-1, keepdims=True) + EPS)
        outs.append(yn * ng_ref[:, g * gsz:(g + 1) * gsz])
    o_ref[...] = jnp.concatenate(outs, axis=-1)


def _ssd(proj, narrow, dtt, conv_w, conv_b, dt_bias, a_log, d_skip, norm_g):
    s = proj.shape[0]
    tb = min(256, s)
    pad8 = lambda v: jnp.pad(v, (0, LANES - v.shape[0])).reshape(1, LANES)
    colv = lambda v: v.reshape(SSM_HEADS, 1)
    full = lambda shape: pl.BlockSpec(shape, lambda i: (0,) * len(shape))
    return pl.pallas_call(
        _ssd_kernel,
        grid=(s // tb,),
        in_specs=[
            _col_spec(tb, SSM_INNER, COL_Z),
            _col_spec(tb, SSM_XBC, COL_XBC),
            _col_spec(tb, LANES, NCOL_DT),
            pl.BlockSpec((SSM_HEADS, tb), lambda i: (0, i)),
            full((4, SSM_XBC)), full((1, SSM_XBC)), full((1, LANES)), full((SSM_HEADS, 1)),
            full((1, LANES)), full((SSM_HEADS, 1)), full((1, SSM_INNER)), full((1, SSM_INNER)),
        ],
        out_specs=pl.BlockSpec((tb, SSM_INNER), lambda i: (i, 0)),
        out_shape=jax.ShapeDtypeStruct((s, SSM_INNER), F32),
        scratch_shapes=[
            pltpu.VMEM((8, SSM_XBC), F32),
            pltpu.VMEM((tb, SSM_XBC), F32),
            pltpu.VMEM((tb, SSM_INNER), F32),
            pltpu.VMEM((SSM_HEADS, SSM_HEAD_DIM, SSM_STATE), F32),
        ],
        compiler_params=_params("arbitrary"),
        name="ssd",
    )(proj, proj, narrow, dtt, conv_w, conv_b.reshape(1, -1), pad8(dt_bias), colv(dt_bias),
      pad8(a_log), colv(a_log), jnp.repeat(d_skip, SSM_HEAD_DIM).reshape(1, -1), norm_g.reshape(1, -1))


def _layout_w_in(w):
    d = w.shape[0]
    z, xbc, dt, sb, gqkv, gab, ggate, br = jnp.split(w, [512, 1536, 1544, 3080, 4616, 4624, 5136], axis=1)
    pad = jnp.zeros((d, LANES - 8), w.dtype)
    wide = jnp.concatenate([br, sb, gqkv, xbc, z, ggate], axis=1).astype(BF16)
    narrow = jnp.concatenate([dt, pad, gab, pad], axis=1).astype(BF16)
    return wide, narrow


def _head_rms(x, g):
    return (x * lax.rsqrt(jnp.mean(x * x, axis=-1, keepdims=True) + EPS)) * g


def _sb_blocks(qs, kns, vs, accs, suffix, masked):
    blk = qs[0].shape[0]
    strict = _iota2((blk, blk), 1) < _iota2((blk, blk), 0)
    zs = [_dot_nt(q, kn) for q, kn in zip(qs, kns)]
    sps = [jnp.maximum(z, 0.0) + jnp.log1p(jnp.exp(-jnp.abs(z))) for z in zs]
    log_keeps = [jnp.where(strict, -sp, 0.0) if masked else -sp for sp in sps]
    splits = [_split2(lk) for lk in log_keeps]
    afters = [(_dot(hi, suffix) + _dot(lo, suffix)) + acc for (hi, lo), acc in zip(splits, accs)]
    atts = [jnp.exp((z - sp) + after) for z, sp, after in zip(zs, sps, afters)]
    if masked:
        atts = [jnp.where(strict, att, 0.0) for att in atts]
    outs = [_dot(_bf(att), v) for att, v in zip(atts, vs)]
    return outs, [jnp.sum(lk, axis=-1, keepdims=True) for lk in log_keeps]


def _sb_kernel(q_ref, k_ref, v_ref, qg_ref, kg_ref, o_ref, acc_scr):
    blk = SB_BLOCK
    dh = SB_HEAD_DIM
    heads = range(SB_HEADS)
    i = pl.program_id(0)
    suffix = (_iota2((blk, blk), 0) > _iota2((blk, blk), 1)).astype(BF16)
    qs = [_bf(_head_rms(q_ref[:, h * dh:(h + 1) * dh].astype(F32), qg_ref[...]) * (dh ** -0.5)) for h in heads]

    def load_kv(j):
        rows = pl.ds(pl.multiple_of(j * blk, blk), blk)
        kns = [_bf(_head_rms(k_ref[rows, h * dh:(h + 1) * dh].astype(F32), kg_ref[...])) for h in heads]
        return kns, [v_ref[rows, h * dh:(h + 1) * dh] for h in heads]

    def live(accs):
        top = functools.reduce(jnp.maximum, accs)
        return (jnp.max(top) > SB_SKIP_LOG).astype(jnp.int32)

    kns, vs = load_kv(i)
    outs, sums = _sb_blocks(qs, kns, vs, [jnp.zeros((blk, 1), F32)] * SB_HEADS, suffix, True)
    for h in heads:
        o_ref[:, h * dh:(h + 1) * dh] = outs[h]
        acc_scr[h] = sums[h]

    def cond(carry):
        j, alive = carry
        return jnp.logical_and(j >= 0, alive > 0)

    def body(carry):
        j, _ = carry
        kns, vs = load_kv(j)
        accs = [acc_scr[h] for h in heads]
        outs, sums = _sb_blocks(qs, kns, vs, accs, suffix, False)
        accs = [acc + rs for acc, rs in zip(accs, sums)]
        for h in heads:
            o_ref[:, h * dh:(h + 1) * dh] += outs[h]
            acc_scr[h] = accs[h]
        return j - 1, live(accs)

    lax.while_loop(cond, body, (i - 1, live(sums)))


def _stick_breaking(proj, q_g, k_g):
    s = proj.shape[0]
    blk = SB_BLOCK
    dh = SB_HEAD_DIM
    width = SB_HEADS * dh
    vec = pl.BlockSpec((1, dh), lambda i: (0, 0))
    resident = lambda col: pl.BlockSpec((s, width), lambda i: (0, col // width), pipeline_mode=pl.Buffered(1))
    return pl.pallas_call(
        _sb_kernel,
        grid=(s // blk,),
        in_specs=[_col_spec(blk, width, COL_SB), resident(COL_SB + width), resident(COL_SB + 2 * width), vec, vec],
        out_specs=pl.BlockSpec((blk, width), lambda i: (i, 0)),
        out_shape=jax.ShapeDtypeStruct((s, width), F32),
        scratch_shapes=[pltpu.VMEM((SB_HEADS, blk, 1), F32)],
        compiler_params=_params("arbitrary"),
        name="stick_breaking",
    )(proj, proj, proj, q_g.reshape(1, dh), k_g.reshape(1, dh))


def _dot3_nt(a, b):
    ah, al = _split2(a)
    bh, bl = _split2(b)
    return _dot_nt(ah, bh) + (_dot_nt(ah, bl) + _dot_nt(al, bh))


def _chunk_lower_inverses(ms, chunk):
    n = ms[0].shape[0]
    eye = (_iota2((n, n), 0) == _iota2((n, n), 1)).astype(F32)
    ps = [-m for m in ms]
    invs = [eye + p for p in ps]
    for _ in range((chunk - 1).bit_length() - 1):
        ps = [_dot3(p, p) for p in ps]
        invs = [inv + _dot3(inv, p) for inv, p in zip(invs, ps)]
    return invs


def _gdn_kernel(qkv_ref, gab_ref, gabt_ref, gate_ref, cw_ref, al_ref, alt_ref, dtb_ref, dtbt_ref, ng_ref,
                o_ref, tail_scr, act_scr, st_scr):
    tb = qkv_ref.shape[0]
    C = GDN_CHUNK
    dh = GDN_HEAD_DIM
    inner = GDN_HEADS * dh
    heads = range(GDN_HEADS)

    @pl.when(pl.program_id(0) == 0)
    def _():
        tail_scr[...] = jnp.zeros_like(tail_scr)
        st_scr[...] = jnp.zeros_like(st_scr)

    raw = qkv_ref[...].astype(F32)
    act_scr[...] = _silu(_causal_conv4(raw, tail_scr[...], cw_ref))
    tail_scr[...] = raw[tb - 8:tb]

    ri = _iota2((tb, tb), 0)
    ci = _iota2((tb, tb), 1)
    same = (ri // C) == (ci // C)
    incl = jnp.logical_and(same, ri >= ci)
    strict = jnp.logical_and(same, ri > ci)
    tril = incl.astype(F32)
    gab = gab_ref[...]
    g_col = -jnp.exp(al_ref[...]) * _softplus(gab + dtb_ref[...])
    beta_col = jax.nn.sigmoid(gab)
    g_row = -jnp.exp(alt_ref[...]) * _softplus(gabt_ref[...] + dtbt_ref[...])
    gc_col = _dot_hp(tril, g_col)
    gc_row = _dot_hp(g_row, tril.T)

    qs, ks, kbs, gcs, decays, rhss = [], [], [], [], [], []
    for h in heads:
        q = act_scr[:, h * dh:(h + 1) * dh]
        k = act_scr[:, inner + h * dh:inner + (h + 1) * dh]
        v = act_scr[:, 2 * inner + h * dh:2 * inner + (h + 1) * dh]
        q = q * lax.rsqrt(jnp.sum(q * q, axis=-1, keepdims=True) + EPS) * (dh ** -0.5)
        k = k * lax.rsqrt(jnp.sum(k * k, axis=-1, keepdims=True) + EPS)
        beta = beta_col[:, GDN_HEADS + h:GDN_HEADS + h + 1]
        gc = gc_col[:, h:h + 1]
        kb = k * beta
        qs.append(q)
        ks.append(k)
        kbs.append(kb)
        gcs.append(gc)
        decays.append(jnp.exp(jnp.where(incl, gc - gc_row[h:h + 1, :], -jnp.inf)))
        rhss.append(jnp.concatenate([v * beta, kb * jnp.exp(gc)], axis=-1))

    ms = [jnp.where(strict, _dot3_nt(kbs[h], ks[h]) * decays[h], 0.0) for h in heads]
    invs = _chunk_lower_inverses(ms, C)
    sols = [_dot3(invs[h], rhss[h]) for h in heads]
    attns = [_bf(jnp.where(incl, _dot_nt(_bf(qs[h]), _bf(ks[h])) * decays[h], 0.0)) for h in heads]
    q_decs = [_bf(qs[h] * jnp.exp(gcs[h])) for h in heads]

    sts = [st_scr[h] for h in heads]
    v_news = [[] for _ in heads]
    o_inters = [[] for _ in heads]
    for c in range(tb // C):
        rows = slice(c * C, (c + 1) * C)
        for h in heads:
            st_b = _bf(sts[h])
            g_last = gc_row[h:h + 1, (c + 1) * C - 1:(c + 1) * C]
            v_new = sols[h][rows, :dh] - _dot(_bf(sols[h][rows, dh:]), st_b)
            v_new_b = _bf(v_new)
            o_inters[h].append(_dot(q_decs[h][rows, :], st_b))
            k_dec = ks[h][rows, :] * jnp.exp(g_last - gcs[h][rows, :])
            sts[h] = sts[h] * jnp.exp(g_last) + _dot_tn(_bf(k_dec), v_new_b)
            v_news[h].append(v_new_b)
    for h in heads:
        st_scr[h] = sts[h]
        o = jnp.concatenate(o_inters[h], axis=0) + _dot(attns[h], jnp.concatenate(v_news[h], axis=0))
        o = _head_rms(o, ng_ref[...]) * _silu(gate_ref[:, h * dh:(h + 1) * dh].astype(F32))
        o_ref[:, h * dh:(h + 1) * dh] = o


def _gdn(proj, narrow, gabt, conv_w, a_log, dt_bias, norm_g):
    s = proj.shape[0]
    tb = min(256, s)
    inner = GDN_HEADS * GDN_HEAD_DIM
    pad_lane = lambda v: jnp.pad(v, (0, LANES - v.shape[0])).reshape(1, LANES)
    pad_col = lambda v: jnp.pad(v, (0, 8 - v.shape[0])).reshape(8, 1)
    full = lambda shape: pl.BlockSpec(shape, lambda i: (0,) * len(shape))
    return pl.pallas_call(
        _gdn_kernel,
        grid=(s // tb,),
        in_specs=[
            _col_spec(tb, 3 * inner, COL_GQKV),
            _col_spec(tb, LANES, NCOL_GAB),
            pl.BlockSpec((8, tb), lambda i: (0, i)),
            _col_spec(tb, inner, COL_GGATE),
            full((4, 3 * inner)), full((1, LANES)), full((8, 1)), full((1, LANES)), full((8, 1)),
            full((1, GDN_HEAD_DIM)),
        ],
        out_specs=pl.BlockSpec((tb, inner), lambda i: (i, 0)),
        out_shape=jax.ShapeDtypeStruct((s, inner), F32),
        scratch_shapes=[
            pltpu.VMEM((8, 3 * inner), F32),
            pltpu.VMEM((tb, 3 * inner), F32),
            pltpu.VMEM((GDN_HEADS, GDN_HEAD_DIM, GDN_HEAD_DIM), F32),
        ],
        compiler_params=_params("arbitrary"),
        name="gdn",
    )(proj, narrow, gabt, proj, conv_w, pad_lane(a_log), pad_col(a_log), pad_lane(dt_bias), pad_col(dt_bias),
      norm_g.reshape(1, -1))


def _merge_kernel(ya_ref, yb_ref, yc_ref, br_ref, x_ref, wbr_ref, wout_ref, gm_ref, g_ref, sc_ref, sh_ref,
                  wrt_ref, brt_ref, xo_ref, h_ref, lg_ref):
    d = x_ref.shape[1]
    merged = None
    for i, y_ref in enumerate((ya_ref, yb_ref, yc_ref)):
        gate = jax.nn.sigmoid(br_ref[:, i * d:(i + 1) * d].astype(F32))
        term = gate * _dot(_bf(y_ref[...]), wbr_ref[i])
        merged = term if merged is None else merged + term
    x_new = x_ref[...] + gm_ref[...] * _dot(_bf(merged), wout_ref[...])
    xo_ref[...] = x_new
    h = _norm_mod(x_new, g_ref[...], sc_ref[...], sh_ref[...])
    h_ref[...] = h
    lg_ref[...] = _dot3(h, wrt_ref[...]) + brt_ref[...]


def _merge(ya, yb, yc, proj, x2, wbr_bf, wout_bf, gate_m, g, scale, shift, w_rt, b_rt):
    s, d = x2.shape
    tb = min(256, s)
    bw = ya.shape[1]
    row = lambda a: a.reshape(1, -1)
    vec = pl.BlockSpec((1, d), lambda i: (0, 0))
    blk = lambda w: pl.BlockSpec((tb, w), lambda i: (i, 0))
    return pl.pallas_call(
        _merge_kernel,
        grid=(s // tb,),
        in_specs=[blk(bw), blk(bw), blk(bw), _col_spec(tb, 3 * d, COL_BR), blk(d),
                  pl.BlockSpec((3, bw, d), lambda i: (0, 0, 0)), pl.BlockSpec((d, d), lambda i: (0, 0)),
                  vec, vec, vec, vec,
                  pl.BlockSpec((d, LANES), lambda i: (0, 0)), pl.BlockSpec((1, LANES), lambda i: (0, 0))],
        out_specs=[blk(d), blk(d), blk(LANES)],
        out_shape=[jax.ShapeDtypeStruct((s, d), F32), jax.ShapeDtypeStruct((s, d), F32),
                   jax.ShapeDtypeStruct((s, LANES), F32)],
        compiler_params=_params("arbitrary"),
        name="merge",
    )(ya, yb, yc, proj, x2, wbr_bf, wout_bf, row(gate_m), row(g), row(scale), row(shift), w_rt, row(b_rt))


ROUTE_E0 = MOE_GROUPS


def _route_kernel(lg_ref, ids_ref, wts_ref, cnt_ref, carry_scr):
    tb = lg_ref.shape[0]

    @pl.when(pl.program_id(0) == 0)
    def _():
        carry_scr[...] = jnp.zeros_like(carry_scr)

    lg = lg_ref[...]
    lane = _iota2((tb, LANES), 1)
    big = jnp.int32(LANES)
    neg = -jnp.inf
    gl = jnp.where(lane < MOE_GROUPS, lg, neg)
    gmax = jnp.max(gl, axis=-1, keepdims=True)
    g_sel = jnp.min(jnp.where(gl == gmax, lane, big), axis=-1, keepdims=True)
    p_group = 1.0 / jnp.sum(jnp.exp(gl - gmax), axis=-1, keepdims=True)
    lo = ROUTE_E0 + EXPERTS_PER_GROUP * g_sel
    el = jnp.where(jnp.logical_and(lane >= lo, lane < lo + EXPERTS_PER_GROUP), lg, neg)
    m1 = jnp.max(el, axis=-1, keepdims=True)
    i1 = jnp.min(jnp.where(el == m1, lane, big), axis=-1, keepdims=True)
    esum = jnp.sum(jnp.exp(el - m1), axis=-1, keepdims=True)
    el2 = jnp.where(lane == i1, neg, el)
    m2 = jnp.max(el2, axis=-1, keepdims=True)
    i2 = jnp.min(jnp.where(el2 == m2, lane, big), axis=-1, keepdims=True)
    p1 = 1.0 / esum
    p2 = jnp.exp(m2 - m1) / esum
    w1 = p_group * p1 / (p1 + p2)
    w2 = p_group * p2 / (p1 + p2)

    sel1 = lane == i1
    sel2 = lane == i2
    onehot = jnp.where(jnp.logical_or(sel1, sel2), 1.0, 0.0)
    before = (_iota2((tb, tb), 0) > _iota2((tb, tb), 1)).astype(BF16)
    seen = _dot(before, _bf(onehot)) + carry_scr[...]
    r1 = jnp.sum(jnp.where(sel1, seen, 0.0), axis=-1, keepdims=True)
    r2 = jnp.sum(jnp.where(sel2, seen, 0.0), axis=-1, keepdims=True)
    carry = carry_scr[...] + jnp.sum(onehot, axis=0, keepdims=True)
    carry_scr[...] = carry
    cnt_ref[...] = jnp.broadcast_to(carry, cnt_ref.shape)

    ids = jnp.where(lane == 0, i1 - ROUTE_E0, jnp.where(lane == 1, i2 - ROUTE_E0,
          jnp.where(lane == 2, r1.astype(jnp.int32), jnp.where(lane == 3, r2.astype(jnp.int32), 0))))
    ids_ref[...] = ids
    wts_ref[...] = jnp.where(lane == 0, w1, jnp.where(lane == 1, w2, 0.0))


def _route(logits):
    s = logits.shape[0]
    tb = min(256, s)
    blk = pl.BlockSpec((tb, LANES), lambda i: (i, 0))
    return pl.pallas_call(
        _route_kernel,
        grid=(s // tb,),
        in_specs=[blk],
        out_specs=[blk, blk, pl.BlockSpec((8, LANES), lambda i: (0, 0))],
        out_shape=[jax.ShapeDtypeStruct((s, LANES), jnp.int32), jax.ShapeDtypeStruct((s, LANES), F32),
                   jax.ShapeDtypeStruct((8, LANES), F32)],
        scratch_shapes=[pltpu.VMEM((1, LANES), F32)],
        compiler_params=_params("arbitrary"),
        name="route",
    )(logits)


EXPERT_BLOCK = 256
ROW_TB = 256


def _dispatch_kernel(dest_ref, h_ref, xb_in_ref, xb_ref, sem):
    del xb_in_ref
    tb = h_ref.shape[0]
    base = pl.program_id(0) * tb * MOE_TOP_K

    def row_copy(t, k, d):
        return pltpu.make_async_copy(h_ref.at[pl.ds(t, 1), :], xb_ref.at[pl.ds(d, 1), :], sem)

    def issue(t, carry):
        for k in range(MOE_TOP_K):
            row_copy(t, k, dest_ref[base + t * MOE_TOP_K + k]).start()
        return carry

    lax.fori_loop(0, tb, issue, 0, unroll=8)
    for k in range(MOE_TOP_K):
        pltpu.make_async_copy(h_ref, xb_ref.at[pl.ds(0, tb), :], sem).wait()


def _dispatch(dest, h, n_slots):
    s, d = h.shape
    tb = min(ROW_TB, s)
    xb0 = jnp.zeros((n_slots, d), F32)
    return pl.pallas_call(
        _dispatch_kernel,
        grid_spec=pltpu.PrefetchScalarGridSpec(
            num_scalar_prefetch=1,
            grid=(s // tb,),
            in_specs=[pl.BlockSpec((tb, d), lambda i, dest: (i, 0)), pl.BlockSpec(memory_space=pl.ANY)],
            out_specs=pl.BlockSpec(memory_space=pl.ANY),
            scratch_shapes=[pltpu.SemaphoreType.DMA(())],
        ),
        out_shape=jax.ShapeDtypeStruct((n_slots, d), F32),
        input_output_aliases={2: 0},
        compiler_params=_params("arbitrary"),
        name="dispatch",
    )(dest, h, xb0)


def _expert_kernel(be_ref, nused_ref, x_ref, wg_ref, wu_ref, wd_ref, o_ref):
    b = pl.program_id(0)

    @pl.when(b < nused_ref[0])
    def _():
        x = _bf(x_ref[...])
        hid = _silu(_dot(x, wg_ref[0])) * _dot(x, wu_ref[0])
        o_ref[...] = _dot(_bf(hid), wd_ref[0])

    @pl.when(b >= nused_ref[0])
    def _():
        o_ref[...] = jnp.zeros_like(o_ref)


def _experts(block_expert, n_used, xb, wg_bf, wu_bf, wd_bf):
    n_slots, d = xb.shape
    ff = wg_bf.shape[2]
    nb = n_slots // EXPERT_BLOCK
    return pl.pallas_call(
        _expert_kernel,
        grid_spec=pltpu.PrefetchScalarGridSpec(
            num_scalar_prefetch=2,
            grid=(nb,),
            in_specs=[
                pl.BlockSpec((EXPERT_BLOCK, d), lambda b, be, nu: (b, 0)),
                pl.BlockSpec((1, d, ff), lambda b, be, nu: (be[b], 0, 0)),
                pl.BlockSpec((1, d, ff), lambda b, be, nu: (be[b], 0, 0)),
                pl.BlockSpec((1, ff, d), lambda b, be, nu: (be[b], 0, 0)),
            ],
            out_specs=pl.BlockSpec((EXPERT_BLOCK, d), lambda b, be, nu: (b, 0)),
        ),
        out_shape=jax.ShapeDtypeStruct((n_slots, d), F32),
        compiler_params=_params("arbitrary"),
        name="experts",
    )(block_expert, n_used, xb, wg_bf, wu_bf, wd_bf)


def _combine_kernel(dest_ref, yb_ref, wts_ref, x_ref, gf_ref, o_ref, buf, sem):
    tb = x_ref.shape[0]
    base = pl.program_id(0) * tb * MOE_TOP_K

    def row_copy(t, k, d):
        return pltpu.make_async_copy(yb_ref.at[pl.ds(d, 1), :], buf.at[k, pl.ds(t, 1), :], sem)

    def issue(t, carry):
        for k in range(MOE_TOP_K):
            row_copy(t, k, dest_ref[base + t * MOE_TOP_K + k]).start()
        return carry

    lax.fori_loop(0, tb, issue, 0, unroll=8)
    for k in range(MOE_TOP_K):
        pltpu.make_async_copy(yb_ref.at[pl.ds(0, tb), :], buf.at[k], sem).wait()
    wts = wts_ref[...]
    y = wts[:, 0:1] * buf[0] + wts[:, 1:2] * buf[1]
    o_ref[...] = x_ref[...] + gf_ref[...] * y


def _combine(dest, yb, wts, x2, gate_f):
    s, d = x2.shape
    tb = min(ROW_TB, s)
    return pl.pallas_call(
        _combine_kernel,
        grid_spec=pltpu.PrefetchScalarGridSpec(
            num_scalar_prefetch=1,
            grid=(s // tb,),
            in_specs=[pl.BlockSpec(memory_space=pl.ANY),
                      pl.BlockSpec((tb, LANES), lambda i, dest: (i, 0)),
                      pl.BlockSpec((tb, d), lambda i, dest: (i, 0)),
                      pl.BlockSpec((1, d), lambda i, dest: (0, 0))],
            out_specs=pl.BlockSpec((tb, d), lambda i, dest: (i, 0)),
            scratch_shapes=[pltpu.VMEM((MOE_TOP_K, tb, d), F32), pltpu.SemaphoreType.DMA(())],
        ),
        out_shape=jax.ShapeDtypeStruct((s, d), F32),
        compiler_params=_params("arbitrary"),
        name="combine",
    )(dest, yb, wts, x2, gate_f.reshape(1, d))


def _moe(h, logits, x2, gate_f, wg_bf, wu_bf, wd_bf):
    s = h.shape[0]
    ids, wts, cnt = _route(logits)
    counts = cnt[0, ROUTE_E0:ROUTE_E0 + N_EXPERTS].astype(jnp.int32)
    padded = (counts + EXPERT_BLOCK - 1) // EXPERT_BLOCK * EXPERT_BLOCK
    pend = jnp.cumsum(padded)
    pstart = pend - padded
    dest = (pstart[ids[:, 0:MOE_TOP_K]] + ids[:, MOE_TOP_K:2 * MOE_TOP_K]).reshape(s * MOE_TOP_K)
    nb = (s * MOE_TOP_K) // EXPERT_BLOCK + N_EXPERTS
    block_start = jnp.arange(nb, dtype=jnp.int32) * EXPERT_BLOCK
    block_expert = jnp.sum((pend[None, :] <= block_start[:, None]).astype(jnp.int32), axis=1)
    block_expert = jnp.minimum(block_expert, N_EXPERTS - 1)
    n_used = (pend[-1:] // EXPERT_BLOCK).astype(jnp.int32)
    xb = _dispatch(dest, h, nb * EXPERT_BLOCK)
    yb = _experts(block_expert, n_used, xb, wg_bf, wu_bf, wd_bf)
    return _combine(dest, yb, wts, x2, gate_f)


def kernel(x, c, w_ada, b_ada, norm_mix, norm_ffn, w_in, ssm_conv_w, ssm_conv_b, ssm_dt_bias, ssm_a_log, ssm_d,
           ssm_norm, sb_q_norm, sb_k_norm, gdn_conv_w, gdn_a_log, gdn_dt_bias, gdn_norm, w_branch, w_out,
           w_group, b_group, w_router, b_router, w_gate, w_up, w_down):
    bsz, s, d = x.shape
    assert bsz == 1 and d == D_MODEL
    depth = w_in.shape[0]
    mod = _adaln_mod(c, w_ada, b_ada)
    x2 = x.reshape(s, d)
    for l in range(depth):
        shift_m, scale_m, gate_m, shift_f, scale_f, gate_f = jnp.split(mod[l], 6)
        proj, narrow = _inproj(x2, norm_mix[l], scale_m, shift_m, *_layout_w_in(w_in[l]))
        dtt = narrow[:, NCOL_DT:NCOL_DT + 8].T
        gabt = narrow[:, NCOL_GAB:NCOL_GAB + 8].T
        ya = _ssd(proj, narrow, dtt, ssm_conv_w[l], ssm_conv_b[l], ssm_dt_bias[l], ssm_a_log[l], ssm_d[l], ssm_norm[l])
        yb = _stick_breaking(proj, sb_q_norm[l], sb_k_norm[l])
        yc = _gdn(proj, narrow, gabt, gdn_conv_w[l], gdn_a_log[l], gdn_dt_bias[l], gdn_norm[l])
        pad = jnp.zeros((d, LANES - MOE_GROUPS - N_EXPERTS), F32)
        w_rt = jnp.concatenate([w_group[l], w_router[l], pad], axis=1)
        b_rt = jnp.concatenate([b_group[l], b_router[l], pad[0]])
        x2, h, logits = _merge(ya, yb, yc, proj, x2, _bf(w_branch[l]), _bf(w_out[l]), gate_m,
                               norm_ffn[l], scale_f, shift_f, w_rt, b_rt)
        x2 = _moe(h, logits, x2, gate_f, _bf(w_gate[l]), _bf(w_up[l]), _bf(w_down[l]))
    return x2.reshape(bsz, s, d)
```

```python
import functools

import jax
import jax.numpy as jnp
from jax import lax
from jax.experimental import pallas as pl
from jax.experimental.pallas import tpu as pltpu

F32 = jnp.float32
BF16 = jnp.bfloat16
EPS = 1e-6

D_MODEL = 1024
SSM_HEADS = 8
SSM_HEAD_DIM = 64
SSM_INNER = 512
SSM_GROUPS = 2
SSM_STATE = 128
SSM_XBC = 1024
SSD_CHUNK = 128
SB_HEADS = 4
SB_HEAD_DIM = 128
SB_BLOCK = 128
GDN_HEADS = 4
GDN_HEAD_DIM = 128
GDN_CHUNK = 64
MOE_GROUPS = 4
EXPERTS_PER_GROUP = 8
N_EXPERTS = 32
MOE_TOP_K = 2
EXPERT_FF = 512

LANES = 128
COL_BR = 0
COL_SB = 3072
COL_GQKV = 4608
COL_XBC = 6144
COL_Z = 7168
COL_GGATE = 7680
WIDE_COLS = 8192
NCOL_DT = 0
NCOL_GAB = 128
NARROW_COLS = 256


def _col_spec(tb, width, col):
    assert col % width == 0
    return pl.BlockSpec((tb, width), lambda i: (i, col // width))

VMEM_LIMIT = 48 * 1024 * 1024
SB_SKIP_LOG = -110.0


def _bf(x):
    return x.astype(BF16)


def _dot(a, b):
    return jnp.dot(a, b, preferred_element_type=F32)


def _dot_nt(a, b):
    return lax.dot_general(a, b, (((1,), (1,)), ((), ())), preferred_element_type=F32)


def _dot_tn(a, b):
    return lax.dot_general(a, b, (((0,), (0,)), ((), ())), preferred_element_type=F32)


def _dot_hp(a, b):
    return jnp.dot(a, b, preferred_element_type=F32, precision=lax.Precision.HIGHEST)


def _split2(x):
    hi = _bf(x)
    lo = _bf(x - hi.astype(F32))
    return hi, lo


def _dot3(a, b):
    ah, al = _split2(a)
    bh, bl = _split2(b)
    return _dot(ah, bh) + (_dot(ah, bl) + _dot(al, bh))


def _silu(x):
    return x * jax.nn.sigmoid(x)


def _softplus(x):
    return jnp.maximum(x, 0.0) + jnp.log1p(jnp.exp(-jnp.abs(x)))


def _iota2(shape, dim):
    return lax.broadcasted_iota(jnp.int32, shape, dim)


def _params(*sem):
    return pltpu.CompilerParams(dimension_semantics=sem, vmem_limit_bytes=VMEM_LIMIT)


def _mod_kernel(c_ref, w_ref, b_ref, o_ref):
    c = _silu(c_ref[...])
    o_ref[0] = _dot_hp(c, w_ref[0]) + b_ref[0]


def _adaln_mod(c, w_ada, b_ada):
    depth, d, cols = w_ada.shape
    tn = 1024
    c8 = jnp.broadcast_to(c, (8, d))
    out = pl.pallas_call(
        _mod_kernel,
        grid=(depth, cols // tn),
        in_specs=[
            pl.BlockSpec((8, d), lambda l, j: (0, 0)),
            pl.BlockSpec((1, d, tn), lambda l, j: (l, 0, j)),
            pl.BlockSpec((1, 1, tn), lambda l, j: (l, 0, j)),
        ],
        out_specs=pl.BlockSpec((1, 8, tn), lambda l, j: (l, 0, j)),
        out_shape=jax.ShapeDtypeStruct((depth, 8, cols), F32),
        compiler_params=_params("arbitrary", "arbitrary"),
        name="adaln_mod",
    )(c8, w_ada, b_ada.reshape(depth, 1, cols))
    return out[:, 0, :]


def _norm_mod(x, g, scale, shift):
    y = x * lax.rsqrt(jnp.mean(x * x, axis=-1, keepdims=True) + EPS)
    return (y * g) * (1.0 + scale) + shift


INPROJ_TN = 1024
assert COL_SB % INPROJ_TN == 0 and 2 * SB_HEADS * SB_HEAD_DIM == INPROJ_TN


def _inproj_kernel(x_ref, g_ref, sc_ref, sh_ref, w_ref, wn_ref, qkg_ref, qks_ref, o_ref, on_ref, h_scr):
    j = pl.program_id(1)

    @pl.when(j == 0)
    def _():
        h = _bf(_norm_mod(x_ref[...], g_ref[...], sc_ref[...], sh_ref[...]))
        h_scr[...] = h
        on_ref[...] = _dot(h, wn_ref[...])

    @pl.when(j != COL_SB // INPROJ_TN)
    def _():
        o_ref[...] = _bf(_dot(h_scr[...], w_ref[...]))

    @pl.when(j == COL_SB // INPROJ_TN)
    def _():
        acc = _dot(h_scr[...], w_ref[...])
        dh = SB_HEAD_DIM
        for n in range(INPROJ_TN // dh):
            cols = slice(n * dh, (n + 1) * dh)
            o_ref[:, cols] = _bf(_head_rms(acc[:, cols], qkg_ref[:, cols]) * qks_ref[:, cols])


def _inproj(x2, g, scale, shift, w_wide, w_narrow, q_g, k_g):
    s, d = x2.shape
    tm = min(1024, s)
    tn = INPROJ_TN
    row = lambda a: a.reshape(1, d)
    vec = pl.BlockSpec((1, d), lambda i, j: (0, 0))
    qk_gain = jnp.concatenate([jnp.tile(q_g, SB_HEADS), jnp.tile(k_g, SB_HEADS)])
    qk_scale = jnp.concatenate([jnp.full((tn // 2,), SB_HEAD_DIM ** -0.5, F32), jnp.ones((tn // 2,), F32)])
    return pl.pallas_call(
        _inproj_kernel,
        grid=(s // tm, WIDE_COLS // tn),
        in_specs=[pl.BlockSpec((tm, d), lambda i, j: (i, 0)), vec, vec, vec,
                  pl.BlockSpec((d, tn), lambda i, j: (0, j)),
                  pl.BlockSpec((d, NARROW_COLS), lambda i, j: (0, 0)),
                  pl.BlockSpec((1, tn), lambda i, j: (0, 0)), pl.BlockSpec((1, tn), lambda i, j: (0, 0))],
        out_specs=[pl.BlockSpec((tm, tn), lambda i, j: (i, j)),
                   pl.BlockSpec((tm, NARROW_COLS), lambda i, j: (i, 0))],
        out_shape=[jax.ShapeDtypeStruct((s, WIDE_COLS), BF16), jax.ShapeDtypeStruct((s, NARROW_COLS), F32)],
        scratch_shapes=[pltpu.VMEM((tm, d), BF16)],
        compiler_params=_params("arbitrary", "arbitrary"),
        name="inproj",
    )(x2, row(g), row(scale), row(shift), w_wide, w_narrow, qk_gain.reshape(1, tn), qk_scale.reshape(1, tn))


def _causal_conv4(x, tail, w_ref):
    tb = x.shape[0]
    ext = jnp.concatenate([tail, x], axis=0)
    y = x * w_ref[3:4, :]
    for k in (1, 2, 3):
        y = y + pltpu.roll(ext, k, axis=0)[8:8 + tb] * w_ref[3 - k:4 - k, :]
    return y


def _ssd_kernel(z_ref, xbc_ref, dt_ref, dtt_ref, cw_ref, cb_ref, dtb_ref, dtbt_ref, al_ref, alt_ref,
                dsk_ref, ng_ref, o_ref, tail_scr, act_scr, y_scr, st_scr):
    tb = xbc_ref.shape[0]
    L = SSD_CHUNK
    P = SSM_HEAD_DIM

    @pl.when(pl.program_id(0) == 0)
    def _():
        tail_scr[...] = jnp.zeros_like(tail_scr)
        st_scr[...] = jnp.zeros_like(st_scr)

    xbc = xbc_ref[...].astype(F32)
    act_scr[...] = _silu(_causal_conv4(xbc, tail_scr[...], cw_ref) + cb_ref[...])
    tail_scr[...] = xbc[tb - 8:tb]

    ri = _iota2((L, L), 0)
    ci = _iota2((L, L), 1)
    tril = (ri >= ci).astype(F32)
    causal = ri >= ci
    a_col = -jnp.exp(al_ref[...])
    a_row = -jnp.exp(alt_ref[...])
    expand = (_iota2((LANES, SSM_INNER), 1) // P == _iota2((LANES, SSM_INNER), 0)).astype(F32)
    hpg = SSM_HEADS // SSM_GROUPS
    gw = hpg * P

    for c in range(tb // L):
        rows = slice(c * L, (c + 1) * L)
        dt = _softplus(dt_ref[rows, :] + dtb_ref[...])
        dtt = _softplus(dtt_ref[:, rows] + dtbt_ref[...])
        acum = _dot_hp(tril, dt * a_col)
        acum_t = _dot_hp(dtt * a_row, tril.T)
        per_head = jnp.concatenate([dt, jnp.exp(acum[L - 1:L, :] - acum), jnp.exp(acum)], axis=0)
        per_chan = _dot_hp(per_head, expand)
        dt_e, to_end_e, from_start_e = per_chan[:L], per_chan[L:2 * L], per_chan[2 * L:]
        act = act_scr[rows, :]
        xs = act[:, :SSM_INNER]
        xdt = xs * dt_e
        xdt_b = _bf(xdt)
        xdt_end_b = _bf(xdt * to_end_e)
        off_c = SSM_INNER + SSM_GROUPS * SSM_STATE
        y_in, y_st = [], []
        for g in range(SSM_GROUPS):
            bm = _bf(act[:, SSM_INNER + g * SSM_STATE:SSM_INNER + (g + 1) * SSM_STATE])
            cm = _bf(act[:, off_c + g * SSM_STATE:off_c + (g + 1) * SSM_STATE])
            gmat = _dot_nt(cm, bm)
            st = st_scr[g]
            y_st.append(_dot_nt(cm, _bf(st)))
            new = _dot_tn(xdt_end_b[:, g * gw:(g + 1) * gw], bm)
            for hh in range(hpg):
                h = g * hpg + hh
                seg = jnp.exp(jnp.where(causal, acum[:, h:h + 1] - acum_t[h:h + 1, :], -jnp.inf))
                y_in.append(_dot(_bf(gmat * seg), xdt_b[:, h * P:(h + 1) * P]))
                sl = slice(hh * P, (hh + 1) * P)
                st_scr[g, sl, :] = st[sl] * jnp.exp(acum_t[h:h + 1, L - 1:L]) + new[sl]
        y_scr[rows, :] = (jnp.concatenate(y_in, axis=-1) + jnp.concatenate(y_st, axis=-1) * from_start_e
                          + xs * dsk_ref[...])

    y = y_scr[...] * _silu(z_ref[...].astype(F32))
    gsz = SSM_INNER // SSM_GROUPS
    outs = []
    for g in range(SSM_GROUPS):
        yg = y[:, g * gsz:(g + 1) * gsz]
        yn = yg * lax.rsqrt(jnp.mean(yg * yg, axis=-1, keepdims=True) + EPS)
        outs.append(yn * ng_ref[:, g * gsz:(g + 1) * gsz])
    o_ref[...] = jnp.concatenate(outs, axis=-1)


def _ssd(proj, narrow, dtt, conv_w, conv_b, dt_bias, a_log, d_skip, norm_g):
    s = proj.shape[0]
    tb = min(256, s)
    pad8 = lambda v: jnp.pad(v, (0, LANES - v.shape[0])).reshape(1, LANES)
    colv = lambda v: v.reshape(SSM_HEADS, 1)
    full = lambda shape: pl.BlockSpec(shape, lambda i: (0,) * len(shape))
    return pl.pallas_call(
        _ssd_kernel,
        grid=(s // tb,),
        in_specs=[
            _col_spec(tb, SSM_INNER, COL_Z),
            _col_spec(tb, SSM_XBC, COL_XBC),
            _col_spec(tb, LANES, NCOL_DT),
            pl.BlockSpec((SSM_HEADS, tb), lambda i: (0, i)),
            full((4, SSM_XBC)), full((1, SSM_XBC)), full((1, LANES)), full((SSM_HEADS, 1)),
            full((1, LANES)), full((SSM_HEADS, 1)), full((1, SSM_INNER)), full((1, SSM_INNER)),
        ],
        out_specs=pl.BlockSpec((tb, SSM_INNER), lambda i: (i, 0)),
        out_shape=jax.ShapeDtypeStruct((s, SSM_INNER), F32),
        scratch_shapes=[
            pltpu.VMEM((8, SSM_XBC), F32),
            pltpu.VMEM((tb, SSM_XBC), F32),
            pltpu.VMEM((tb, SSM_INNER), F32),
            pltpu.VMEM((SSM_GROUPS, SSM_HEADS // SSM_GROUPS * SSM_HEAD_DIM, SSM_STATE), F32),
        ],
        compiler_params=_params("arbitrary"),
        name="ssd",
    )(proj, proj, narrow, dtt, conv_w, conv_b.reshape(1, -1), pad8(dt_bias), colv(dt_bias),
      pad8(a_log), colv(a_log), jnp.repeat(d_skip, SSM_HEAD_DIM).reshape(1, -1), norm_g.reshape(1, -1))


def _layout_w_in(w):
    d = w.shape[0]
    z, xbc, dt, sb, gqkv, gab, ggate, br = jnp.split(w, [512, 1536, 1544, 3080, 4616, 4624, 5136], axis=1)
    pad = jnp.zeros((d, LANES - 8), w.dtype)
    wide = jnp.concatenate([br, sb, gqkv, xbc, z, ggate], axis=1).astype(BF16)
    narrow = jnp.concatenate([dt, pad, gab, pad], axis=1).astype(BF16)
    return wide, narrow


def _head_rms(x, g):
    return (x * lax.rsqrt(jnp.mean(x * x, axis=-1, keepdims=True) + EPS)) * g


def _sb_blocks(qs, kns, vs, accs, suffix, masked):
    blk = qs[0].shape[0]
    strict = _iota2((blk, blk), 1) < _iota2((blk, blk), 0)
    zs = [_dot_nt(q, kn) for q, kn in zip(qs, kns)]
    sps = [jnp.maximum(z, 0.0) + jnp.log1p(jnp.exp(-jnp.abs(z))) for z in zs]
    log_keeps = [jnp.where(strict, -sp, 0.0) if masked else -sp for sp in sps]
    splits = [_split2(lk) for lk in log_keeps]
    afters = [(_dot(hi, suffix) + _dot(lo, suffix)) + acc for (hi, lo), acc in zip(splits, accs)]
    atts = [jnp.exp((z - sp) + after) for z, sp, after in zip(zs, sps, afters)]
    if masked:
        atts = [jnp.where(strict, att, 0.0) for att in atts]
    outs = [_dot(_bf(att), v) for att, v in zip(atts, vs)]
    return outs, [jnp.sum(lk, axis=-1, keepdims=True) for lk in log_keeps]


def _sb_kernel(q_ref, k_ref, v_ref, o_ref, acc_scr):
    blk = SB_BLOCK
    dh = SB_HEAD_DIM
    heads = range(SB_HEADS)
    i = pl.program_id(0)
    suffix = (_iota2((blk, blk), 0) > _iota2((blk, blk), 1)).astype(BF16)
    qs = [q_ref[:, h * dh:(h + 1) * dh] for h in heads]

    def load_kv(j):
        rows = pl.ds(pl.multiple_of(j * blk, blk), blk)
        return ([k_ref[rows, h * dh:(h + 1) * dh] for h in heads],
                [v_ref[rows, h * dh:(h + 1) * dh] for h in heads])

    def live(accs):
        top = functools.reduce(jnp.maximum, accs)
        return (jnp.max(top) > SB_SKIP_LOG).astype(jnp.int32)

    kns, vs = load_kv(i)
    outs, sums = _sb_blocks(qs, kns, vs, [jnp.zeros((blk, 1), F32)] * SB_HEADS, suffix, True)
    for h in heads:
        o_ref[:, h * dh:(h + 1) * dh] = outs[h]
        acc_scr[h] = sums[h]

    def cond(carry):
        j, alive = carry
        return jnp.logical_and(j >= 0, alive > 0)

    def body(carry):
        j, _ = carry
        kns, vs = load_kv(j)
        accs = [acc_scr[h] for h in heads]
        outs, sums = _sb_blocks(qs, kns, vs, accs, suffix, False)
        accs = [acc + rs for acc, rs in zip(accs, sums)]
        for h in heads:
            o_ref[:, h * dh:(h + 1) * dh] += outs[h]
            acc_scr[h] = accs[h]
        return j - 1, live(accs)

    lax.while_loop(cond, body, (i - 1, live(sums)))


def _stick_breaking(proj):
    s = proj.shape[0]
    blk = SB_BLOCK
    dh = SB_HEAD_DIM
    width = SB_HEADS * dh
    resident = lambda col: pl.BlockSpec((s, width), lambda i: (0, col // width), pipeline_mode=pl.Buffered(1))
    return pl.pallas_call(
        _sb_kernel,
        grid=(s // blk,),
        in_specs=[_col_spec(blk, width, COL_SB), resident(COL_SB + width), resident(COL_SB + 2 * width)],
        out_specs=pl.BlockSpec((blk, width), lambda i: (i, 0)),
        out_shape=jax.ShapeDtypeStruct((s, width), F32),
        scratch_shapes=[pltpu.VMEM((SB_HEADS, blk, 1), F32)],
        compiler_params=_params("arbitrary"),
        name="stick_breaking",
    )(proj, proj, proj)


def _dot3_nt(a, b):
    ah, al = _split2(a)
    bh, bl = _split2(b)
    return _dot_nt(ah, bh) + (_dot_nt(ah, bl) + _dot_nt(al, bh))


def _chunk_lower_inverses(ms, chunk):
    n = ms[0].shape[0]
    eye = (_iota2((n, n), 0) == _iota2((n, n), 1)).astype(F32)
    ps = [-m for m in ms]
    invs = [eye + p for p in ps]
    for j in range((chunk - 1).bit_length() - 1):
        ps = [_dot3(p, p) if j == 0 else _dot(_bf(p), _bf(p)) for p in ps]
        invs = [inv + _dot3(inv, p) for inv, p in zip(invs, ps)]
    return invs


def _gdn_kernel(qkv_ref, gab_ref, gabt_ref, gate_ref, cw_ref, al_ref, alt_ref, dtb_ref, dtbt_ref, ng_ref,
                o_ref, tail_scr, act_scr, st_scr):
    tb = qkv_ref.shape[0]
    C = GDN_CHUNK
    dh = GDN_HEAD_DIM
    inner = GDN_HEADS * dh
    heads = range(GDN_HEADS)

    @pl.when(pl.program_id(0) == 0)
    def _():
        tail_scr[...] = jnp.zeros_like(tail_scr)
        st_scr[...] = jnp.zeros_like(st_scr)

    raw = qkv_ref[...].astype(F32)
    act_scr[...] = _silu(_causal_conv4(raw, tail_scr[...], cw_ref))
    tail_scr[...] = raw[tb - 8:tb]

    ri = _iota2((tb, tb), 0)
    ci = _iota2((tb, tb), 1)
    same = (ri // C) == (ci // C)
    incl = jnp.logical_and(same, ri >= ci)
    strict = jnp.logical_and(same, ri > ci)
    tril = incl.astype(F32)
    gab = gab_ref[...]
    g_col = -jnp.exp(al_ref[...]) * _softplus(gab + dtb_ref[...])
    beta_col = jax.nn.sigmoid(gab)
    g_row = -jnp.exp(alt_ref[...]) * _softplus(gabt_ref[...] + dtbt_ref[...])
    gc_col = _dot_hp(tril, g_col)
    gc_row = _dot_hp(g_row, tril.T)

    qs, ks, kbs, gcs, decays, rhss = [], [], [], [], [], []
    for h in heads:
        q = act_scr[:, h * dh:(h + 1) * dh]
        k = act_scr[:, inner + h * dh:inner + (h + 1) * dh]
        v = act_scr[:, 2 * inner + h * dh:2 * inner + (h + 1) * dh]
        q = q * lax.rsqrt(jnp.sum(q * q, axis=-1, keepdims=True) + EPS) * (dh ** -0.5)
        k = k * lax.rsqrt(jnp.sum(k * k, axis=-1, keepdims=True) + EPS)
        beta = beta_col[:, GDN_HEADS + h:GDN_HEADS + h + 1]
        gc = gc_col[:, h:h + 1]
        kb = k * beta
        qs.append(q)
        ks.append(k)
        kbs.append(kb)
        gcs.append(gc)
        decays.append(jnp.exp(jnp.where(incl, gc - gc_row[h:h + 1, :], -jnp.inf)))
        rhss.append(jnp.concatenate([v * beta, kb * jnp.exp(gc)], axis=-1))

    ms = [jnp.where(strict, _dot3_nt(kbs[h], ks[h]) * decays[h], 0.0) for h in heads]
    invs = _chunk_lower_inverses(ms, C)
    sols = [_dot3(invs[h], rhss[h]) for h in heads]
    attns = [_bf(jnp.where(incl, _dot_nt(_bf(qs[h]), _bf(ks[h])) * decays[h], 0.0)) for h in heads]
    q_decs = [_bf(qs[h] * jnp.exp(gcs[h])) for h in heads]

    sts = [st_scr[h] for h in heads]
    v_news = [[] for _ in heads]
    o_inters = [[] for _ in heads]
    for c in range(tb // C):
        rows = slice(c * C, (c + 1) * C)
        for h in heads:
            st_b = _bf(sts[h])
            g_last = gc_row[h:h + 1, (c + 1) * C - 1:(c + 1) * C]
            v_new = sols[h][rows, :dh] - _dot(_bf(sols[h][rows, dh:]), st_b)
            v_new_b = _bf(v_new)
            o_inters[h].append(_dot(q_decs[h][rows, :], st_b))
            k_dec = ks[h][rows, :] * jnp.exp(g_last - gcs[h][rows, :])
            sts[h] = sts[h] * jnp.exp(g_last) + _dot_tn(_bf(k_dec), v_new_b)
            v_news[h].append(v_new_b)
    for h in heads:
        st_scr[h] = sts[h]
        o = jnp.concatenate(o_inters[h], axis=0) + _dot(attns[h], jnp.concatenate(v_news[h], axis=0))
        o = _head_rms(o, ng_ref[...]) * _silu(gate_ref[:, h * dh:(h + 1) * dh].astype(F32))
        o_ref[:, h * dh:(h + 1) * dh] = o


def _gdn(proj, narrow, gabt, conv_w, a_log, dt_bias, norm_g):
    s = proj.shape[0]
    tb = min(256, s)
    inner = GDN_HEADS * GDN_HEAD_DIM
    pad_lane = lambda v: jnp.pad(v, (0, LANES - v.shape[0])).reshape(1, LANES)
    pad_col = lambda v: jnp.pad(v, (0, 8 - v.shape[0])).reshape(8, 1)
    full = lambda shape: pl.BlockSpec(shape, lambda i: (0,) * len(shape))
    return pl.pallas_call(
        _gdn_kernel,
        grid=(s // tb,),
        in_specs=[
            _col_spec(tb, 3 * inner, COL_GQKV),
            _col_spec(tb, LANES, NCOL_GAB),
            pl.BlockSpec((8, tb), lambda i: (0, i)),
            _col_spec(tb, inner, COL_GGATE),
            full((4, 3 * inner)), full((1, LANES)), full((8, 1)), full((1, LANES)), full((8, 1)),
            full((1, GDN_HEAD_DIM)),
        ],
        out_specs=pl.BlockSpec((tb, inner), lambda i: (i, 0)),
        out_shape=jax.ShapeDtypeStruct((s, inner), F32),
        scratch_shapes=[
            pltpu.VMEM((8, 3 * inner), F32),
            pltpu.VMEM((tb, 3 * inner), F32),
            pltpu.VMEM((GDN_HEADS, GDN_HEAD_DIM, GDN_HEAD_DIM), F32),
        ],
        compiler_params=_params("arbitrary"),
        name="gdn",
    )(proj, narrow, gabt, proj, conv_w, pad_lane(a_log), pad_col(a_log), pad_lane(dt_bias), pad_col(dt_bias),
      norm_g.reshape(1, -1))


def _merge_kernel(ya_ref, yb_ref, yc_ref, br_ref, x_ref, wbr_ref, wout_ref, gm_ref, g_ref, sc_ref, sh_ref,
                  wrt_ref, brt_ref, xo_ref, h_ref, lg_ref):
    d = x_ref.shape[1]
    merged = None
    for i, y_ref in enumerate((ya_ref, yb_ref, yc_ref)):
        gate = jax.nn.sigmoid(br_ref[:, i * d:(i + 1) * d].astype(F32))
        term = gate * _dot(_bf(y_ref[...]), wbr_ref[i])
        merged = term if merged is None else merged + term
    x_new = x_ref[...] + gm_ref[...] * _dot(_bf(merged), wout_ref[...])
    xo_ref[...] = x_new
    h = _norm_mod(x_new, g_ref[...], sc_ref[...], sh_ref[...])
    h_ref[...] = h
    lg_ref[...] = _dot3(h, wrt_ref[...]) + brt_ref[...]


def _merge(ya, yb, yc, proj, x2, wbr_bf, wout_bf, gate_m, g, scale, shift, w_rt, b_rt):
    s, d = x2.shape
    tb = min(256, s)
    bw = ya.shape[1]
    row = lambda a: a.reshape(1, -1)
    vec = pl.BlockSpec((1, d), lambda i: (0, 0))
    blk = lambda w: pl.BlockSpec((tb, w), lambda i: (i, 0))
    return pl.pallas_call(
        _merge_kernel,
        grid=(s // tb,),
        in_specs=[blk(bw), blk(bw), blk(bw), _col_spec(tb, 3 * d, COL_BR), blk(d),
                  pl.BlockSpec((3, bw, d), lambda i: (0, 0, 0)), pl.BlockSpec((d, d), lambda i: (0, 0)),
                  vec, vec, vec, vec,
                  pl.BlockSpec((d, LANES), lambda i: (0, 0)), pl.BlockSpec((1, LANES), lambda i: (0, 0))],
        out_specs=[blk(d), blk(d), blk(LANES)],
        out_shape=[jax.ShapeDtypeStruct((s, d), F32), jax.ShapeDtypeStruct((s, d), F32),
                   jax.ShapeDtypeStruct((s, LANES), F32)],
        compiler_params=_params("arbitrary"),
        name="merge",
    )(ya, yb, yc, proj, x2, wbr_bf, wout_bf, row(gate_m), row(g), row(scale), row(shift), w_rt, row(b_rt))


ROUTE_E0 = MOE_GROUPS


def _route_kernel(lg_ref, ids_ref, wts_ref, cnt_ref, carry_scr):
    tb = lg_ref.shape[0]

    @pl.when(pl.program_id(0) == 0)
    def _():
        carry_scr[...] = jnp.zeros_like(carry_scr)

    lg = lg_ref[...]
    lane = _iota2((tb, LANES), 1)
    big = jnp.int32(LANES)
    neg = -jnp.inf
    gl = jnp.where(lane < MOE_GROUPS, lg, neg)
    gmax = jnp.max(gl, axis=-1, keepdims=True)
    g_sel = jnp.min(jnp.where(gl == gmax, lane, big), axis=-1, keepdims=True)
    p_group = 1.0 / jnp.sum(jnp.exp(gl - gmax), axis=-1, keepdims=True)
    lo = ROUTE_E0 + EXPERTS_PER_GROUP * g_sel
    el = jnp.where(jnp.logical_and(lane >= lo, lane < lo + EXPERTS_PER_GROUP), lg, neg)
    m1 = jnp.max(el, axis=-1, keepdims=True)
    i1 = jnp.min(jnp.where(el == m1, lane, big), axis=-1, keepdims=True)
    esum = jnp.sum(jnp.exp(el - m1), axis=-1, keepdims=True)
    el2 = jnp.where(lane == i1, neg, el)
    m2 = jnp.max(el2, axis=-1, keepdims=True)
    i2 = jnp.min(jnp.where(el2 == m2, lane, big), axis=-1, keepdims=True)
    p1 = 1.0 / esum
    p2 = jnp.exp(m2 - m1) / esum
    w1 = p_group * p1 / (p1 + p2)
    w2 = p_group * p2 / (p1 + p2)

    sel1 = lane == i1
    sel2 = lane == i2
    onehot = jnp.where(jnp.logical_or(sel1, sel2), 1.0, 0.0)
    before = (_iota2((tb, tb), 0) > _iota2((tb, tb), 1)).astype(BF16)
    seen = _dot(before, _bf(onehot)) + carry_scr[...]
    r1 = jnp.sum(jnp.where(sel1, seen, 0.0), axis=-1, keepdims=True)
    r2 = jnp.sum(jnp.where(sel2, seen, 0.0), axis=-1, keepdims=True)
    carry = carry_scr[...] + jnp.sum(onehot, axis=0, keepdims=True)
    carry_scr[...] = carry
    cnt_ref[...] = jnp.broadcast_to(carry, cnt_ref.shape)

    ids = jnp.where(lane == 0, i1 - ROUTE_E0, jnp.where(lane == 1, i2 - ROUTE_E0,
          jnp.where(lane == 2, r1.astype(jnp.int32), jnp.where(lane == 3, r2.astype(jnp.int32), 0))))
    ids_ref[...] = ids
    wts_ref[...] = jnp.where(lane == 0, w1, jnp.where(lane == 1, w2, 0.0))


def _route(logits):
    s = logits.shape[0]
    tb = min(256, s)
    blk = pl.BlockSpec((tb, LANES), lambda i: (i, 0))
    return pl.pallas_call(
        _route_kernel,
        grid=(s // tb,),
        in_specs=[blk],
        out_specs=[blk, blk, pl.BlockSpec((8, LANES), lambda i: (0, 0))],
        out_shape=[jax.ShapeDtypeStruct((s, LANES), jnp.int32), jax.ShapeDtypeStruct((s, LANES), F32),
                   jax.ShapeDtypeStruct((8, LANES), F32)],
        scratch_shapes=[pltpu.VMEM((1, LANES), F32)],
        compiler_params=_params("arbitrary"),
        name="route",
    )(logits)


EXPERT_BLOCK = 256
ROW_TB = 256


def _dispatch_kernel(dest_ref, h_ref, xb_in_ref, xb_ref, sem):
    del xb_in_ref
    tb = h_ref.shape[0]
    base = pl.program_id(0) * tb * MOE_TOP_K

    def row_copy(t, k, d):
        return pltpu.make_async_copy(h_ref.at[pl.ds(t, 1), :], xb_ref.at[pl.ds(d, 1), :], sem)

    def issue(t, carry):
        for k in range(MOE_TOP_K):
            row_copy(t, k, dest_ref[base + t * MOE_TOP_K + k]).start()
        return carry

    lax.fori_loop(0, tb, issue, 0, unroll=8)
    for k in range(MOE_TOP_K):
        pltpu.make_async_copy(h_ref, xb_ref.at[pl.ds(0, tb), :], sem).wait()


def _dispatch(dest, h, n_slots):
    s, d = h.shape
    tb = min(ROW_TB, s)
    xb0 = jnp.zeros((n_slots, d), F32)
    return pl.pallas_call(
        _dispatch_kernel,
        grid_spec=pltpu.PrefetchScalarGridSpec(
            num_scalar_prefetch=1,
            grid=(s // tb,),
            in_specs=[pl.BlockSpec((tb, d), lambda i, dest: (i, 0)), pl.BlockSpec(memory_space=pl.ANY)],
            out_specs=pl.BlockSpec(memory_space=pl.ANY),
            scratch_shapes=[pltpu.SemaphoreType.DMA(())],
        ),
        out_shape=jax.ShapeDtypeStruct((n_slots, d), F32),
        input_output_aliases={2: 0},
        compiler_params=_params("arbitrary"),
        name="dispatch",
    )(dest, h, xb0)


def _expert_kernel(be_ref, nused_ref, x_ref, wg_ref, wu_ref, wd_ref, o_ref, wg_b, wu_b, wd_b):
    b = pl.program_id(0)

    @pl.when(jnp.logical_or(b == 0, be_ref[b] != be_ref[jnp.maximum(b - 1, 0)]))
    def _():
        wg_b[...] = _bf(wg_ref[0, 0])
        wu_b[...] = _bf(wu_ref[0, 0])
        wd_b[...] = _bf(wd_ref[0, 0])

    @pl.when(b < nused_ref[0])
    def _():
        x = _bf(x_ref[...])
        hid = _silu(_dot(x, wg_b[...])) * _dot(x, wu_b[...])
        o_ref[...] = _dot(_bf(hid), wd_b[...])

    @pl.when(b >= nused_ref[0])
    def _():
        o_ref[...] = jnp.zeros_like(o_ref)


def _experts(block_expert, n_used, xb, layer, w_gate, w_up, w_down):
    n_slots, d = xb.shape
    ff = w_gate.shape[3]
    nb = n_slots // EXPERT_BLOCK
    return pl.pallas_call(
        _expert_kernel,
        grid_spec=pltpu.PrefetchScalarGridSpec(
            num_scalar_prefetch=2,
            grid=(nb,),
            in_specs=[
                pl.BlockSpec((EXPERT_BLOCK, d), lambda b, be, nu: (b, 0)),
                pl.BlockSpec((1, 1, d, ff), lambda b, be, nu: (layer, be[b], 0, 0)),
                pl.BlockSpec((1, 1, d, ff), lambda b, be, nu: (layer, be[b], 0, 0)),
                pl.BlockSpec((1, 1, ff, d), lambda b, be, nu: (layer, be[b], 0, 0)),
            ],
            out_specs=pl.BlockSpec((EXPERT_BLOCK, d), lambda b, be, nu: (b, 0)),
            scratch_shapes=[pltpu.VMEM((d, ff), BF16), pltpu.VMEM((d, ff), BF16), pltpu.VMEM((ff, d), BF16)],
        ),
        out_shape=jax.ShapeDtypeStruct((n_slots, d), F32),
        compiler_params=_params("arbitrary"),
        name="experts",
    )(block_expert, n_used, xb, w_gate, w_up, w_down)


def _combine_kernel(dest_ref, yb_ref, wts_ref, x_ref, gf_ref, o_ref, buf, sem):
    tb = x_ref.shape[0]
    base = pl.program_id(0) * tb * MOE_TOP_K

    def row_copy(t, k, d):
        return pltpu.make_async_copy(yb_ref.at[pl.ds(d, 1), :], buf.at[k, pl.ds(t, 1), :], sem)

    def issue(t, carry):
        for k in range(MOE_TOP_K):
            row_copy(t, k, dest_ref[base + t * MOE_TOP_K + k]).start()
        return carry

    lax.fori_loop(0, tb, issue, 0, unroll=8)
    for k in range(MOE_TOP_K):
        pltpu.make_async_copy(yb_ref.at[pl.ds(0, tb), :], buf.at[k], sem).wait()
    wts = wts_ref[...]
    y = wts[:, 0:1] * buf[0] + wts[:, 1:2] * buf[1]
    o_ref[...] = x_ref[...] + gf_ref[...] * y


def _combine(dest, yb, wts, x2, gate_f):
    s, d = x2.shape
    tb = min(ROW_TB, s)
    return pl.pallas_call(
        _combine_kernel,
        grid_spec=pltpu.PrefetchScalarGridSpec(
            num_scalar_prefetch=1,
            grid=(s // tb,),
            in_specs=[pl.BlockSpec(memory_space=pl.ANY),
                      pl.BlockSpec((tb, LANES), lambda i, dest: (i, 0)),
                      pl.BlockSpec((tb, d), lambda i, dest: (i, 0)),
                      pl.BlockSpec((1, d), lambda i, dest: (0, 0))],
            out_specs=pl.BlockSpec((tb, d), lambda i, dest: (i, 0)),
            scratch_shapes=[pltpu.VMEM((MOE_TOP_K, tb, d), F32), pltpu.SemaphoreType.DMA(())],
        ),
        out_shape=jax.ShapeDtypeStruct((s, d), F32),
        compiler_params=_params("arbitrary"),
        name="combine",
    )(dest, yb, wts, x2, gate_f.reshape(1, d))


def _moe(h, logits, x2, gate_f, layer, w_gate, w_up, w_down):
    s = h.shape[0]
    ids, wts, cnt = _route(logits)
    counts = cnt[0, ROUTE_E0:ROUTE_E0 + N_EXPERTS].astype(jnp.int32)
    padded = (counts + EXPERT_BLOCK - 1) // EXPERT_BLOCK * EXPERT_BLOCK
    pend = jnp.cumsum(padded)
    pstart = pend - padded
    dest = (pstart[ids[:, 0:MOE_TOP_K]] + ids[:, MOE_TOP_K:2 * MOE_TOP_K]).reshape(s * MOE_TOP_K)
    nb = (s * MOE_TOP_K) // EXPERT_BLOCK + N_EXPERTS
    block_start = jnp.arange(nb, dtype=jnp.int32) * EXPERT_BLOCK
    block_expert = jnp.sum((pend[None, :] <= block_start[:, None]).astype(jnp.int32), axis=1)
    block_expert = jnp.minimum(block_expert, N_EXPERTS - 1)
    n_used = (pend[-1:] // EXPERT_BLOCK).astype(jnp.int32)
    xb = _dispatch(dest, h, nb * EXPERT_BLOCK)
    yb = _experts(block_expert, n_used, xb, layer, w_gate, w_up, w_down)
    return _combine(dest, yb, wts, x2, gate_f)


def kernel(x, c, w_ada, b_ada, norm_mix, norm_ffn, w_in, ssm_conv_w, ssm_conv_b, ssm_dt_bias, ssm_a_log, ssm_d,
           ssm_norm, sb_q_norm, sb_k_norm, gdn_conv_w, gdn_a_log, gdn_dt_bias, gdn_norm, w_branch, w_out,
           w_group, b_group, w_router, b_router, w_gate, w_up, w_down):
    bsz, s, d = x.shape
    assert bsz == 1 and d == D_MODEL
    depth = w_in.shape[0]
    mod = _adaln_mod(c, w_ada, b_ada)
    x2 = x.reshape(s, d)
    for l in range(depth):
        shift_m, scale_m, gate_m, shift_f, scale_f, gate_f = jnp.split(mod[l], 6)
        proj, narrow = _inproj(x2, norm_mix[l], scale_m, shift_m, *_layout_w_in(w_in[l]), sb_q_norm[l], sb_k_norm[l])
        dtt = narrow[:, NCOL_DT:NCOL_DT + 8].T
        gabt = narrow[:, NCOL_GAB:NCOL_GAB + 8].T
        ya = _ssd(proj, narrow, dtt, ssm_conv_w[l], ssm_conv_b[l], ssm_dt_bias[l], ssm_a_log[l], ssm_d[l], ssm_norm[l])
        yb = _stick_breaking(proj)
        yc = _gdn(proj, narrow, gabt, gdn_conv_w[l], gdn_a_log[l], gdn_dt_bias[l], gdn_norm[l])
        pad = jnp.zeros((d, LANES - MOE_GROUPS - N_EXPERTS), F32)
        w_rt = jnp.concatenate([w_group[l], w_router[l], pad], axis=1)
        b_rt = jnp.concatenate([b_group[l], b_router[l], pad[0]])
        x2, h, logits = _merge(ya, yb, yc, proj, x2, _bf(w_branch[l]), _bf(w_out[l]), gate_m,
                               norm_ffn[l], scale_f, shift_f, w_rt, b_rt)
        x2 = _moe(h, logits, x2, gate_f, l, w_gate, w_up, w_down)
    return x2.reshape(bsz, s, d)
```

```python
import functools

import jax
import jax.numpy as jnp
from jax import lax
from jax.experimental import pallas as pl
from jax.experimental.pallas import tpu as pltpu

F32 = jnp.float32
BF16 = jnp.bfloat16
EPS = 1e-6

D_MODEL = 1024
SSM_HEADS = 8
SSM_HEAD_DIM = 64
SSM_INNER = 512
SSM_GROUPS = 2
SSM_STATE = 128
SSM_XBC = 1024
SSD_CHUNK = 128
SB_HEADS = 4
SB_HEAD_DIM = 128
SB_BLOCK = 128
GDN_HEADS = 4
GDN_HEAD_DIM = 128
GDN_CHUNK = 64
MOE_GROUPS = 4
EXPERTS_PER_GROUP = 8
N_EXPERTS = 32
MOE_TOP_K = 2
EXPERT_FF = 512

LANES = 128
COL_BR = 0
COL_SB = 3072
COL_GQKV = 4608
COL_XBC = 6144
COL_Z = 7168
COL_GGATE = 7680
WIDE_COLS = 8192
NCOL_DT = 0
NCOL_GAB = 128
NARROW_COLS = 256


def _col_spec(tb, width, col):
    assert col % width == 0
    return pl.BlockSpec((tb, width), lambda i: (i, col // width))

VMEM_LIMIT = 48 * 1024 * 1024
SB_SKIP_LOG = -110.0


def _bf(x):
    return x.astype(BF16)


def _dot(a, b):
    return jnp.dot(a, b, preferred_element_type=F32)


def _dot_nt(a, b):
    return lax.dot_general(a, b, (((1,), (1,)), ((), ())), preferred_element_type=F32)


def _dot_tn(a, b):
    return lax.dot_general(a, b, (((0,), (0,)), ((), ())), preferred_element_type=F32)


def _dot_hp(a, b):
    return jnp.dot(a, b, preferred_element_type=F32, precision=lax.Precision.HIGHEST)


def _split2(x):
    hi = _bf(x)
    lo = _bf(x - hi.astype(F32))
    return hi, lo


def _dot3(a, b):
    ah, al = _split2(a)
    bh, bl = _split2(b)
    return _dot(ah, bh) + (_dot(ah, bl) + _dot(al, bh))


def _silu(x):
    return x * jax.nn.sigmoid(x)


def _softplus(x):
    return jnp.maximum(x, 0.0) + jnp.log1p(jnp.exp(-jnp.abs(x)))


def _iota2(shape, dim):
    return lax.broadcasted_iota(jnp.int32, shape, dim)


def _params(*sem):
    return pltpu.CompilerParams(dimension_semantics=sem, vmem_limit_bytes=VMEM_LIMIT)


def _mod_kernel(c_ref, w_ref, b_ref, o_ref):
    c = _silu(c_ref[...])
    o_ref[0] = _dot_hp(c, w_ref[0]) + b_ref[0]


def _adaln_mod(c, w_ada, b_ada):
    depth, d, cols = w_ada.shape
    tn = 1024
    c8 = jnp.broadcast_to(c, (8, d))
    out = pl.pallas_call(
        _mod_kernel,
        grid=(depth, cols // tn),
        in_specs=[
            pl.BlockSpec((8, d), lambda l, j: (0, 0)),
            pl.BlockSpec((1, d, tn), lambda l, j: (l, 0, j)),
            pl.BlockSpec((1, 1, tn), lambda l, j: (l, 0, j)),
        ],
        out_specs=pl.BlockSpec((1, 8, tn), lambda l, j: (l, 0, j)),
        out_shape=jax.ShapeDtypeStruct((depth, 8, cols), F32),
        compiler_params=_params("arbitrary", "arbitrary"),
        name="adaln_mod",
    )(c8, w_ada, b_ada.reshape(depth, 1, cols))
    return out[:, 0, :]


def _norm_mod(x, g, scale, shift):
    y = x * lax.rsqrt(jnp.mean(x * x, axis=-1, keepdims=True) + EPS)
    return (y * g) * (1.0 + scale) + shift


INPROJ_TN = 1024
assert COL_SB % INPROJ_TN == 0 and 2 * SB_HEADS * SB_HEAD_DIM == INPROJ_TN


def _inproj_kernel(x_ref, g_ref, sc_ref, sh_ref, w_ref, wn_ref, qkg_ref, qks_ref, o_ref, on_ref, h_scr):
    j = pl.program_id(1)

    @pl.when(j == 0)
    def _():
        h = _bf(_norm_mod(x_ref[...], g_ref[...], sc_ref[...], sh_ref[...]))
        h_scr[...] = h
        on_ref[...] = _dot(h, wn_ref[...])

    @pl.when(j != COL_SB // INPROJ_TN)
    def _():
        o_ref[...] = _bf(_dot(h_scr[...], w_ref[...]))

    @pl.when(j == COL_SB // INPROJ_TN)
    def _():
        acc = _dot(h_scr[...], w_ref[...])
        dh = SB_HEAD_DIM
        for n in range(INPROJ_TN // dh):
            cols = slice(n * dh, (n + 1) * dh)
            o_ref[:, cols] = _bf(_head_rms(acc[:, cols], qkg_ref[:, cols]) * qks_ref[:, cols])


def _inproj(x2, g, scale, shift, w_wide, w_narrow, q_g, k_g):
    s, d = x2.shape
    tm = min(1024, s)
    tn = INPROJ_TN
    row = lambda a: a.reshape(1, d)
    vec = pl.BlockSpec((1, d), lambda i, j: (0, 0))
    qk_gain = jnp.concatenate([jnp.tile(q_g, SB_HEADS), jnp.tile(k_g, SB_HEADS)])
    qk_scale = jnp.concatenate([jnp.full((tn // 2,), SB_HEAD_DIM ** -0.5, F32), jnp.ones((tn // 2,), F32)])
    return pl.pallas_call(
        _inproj_kernel,
        grid=(s // tm, WIDE_COLS // tn),
        in_specs=[pl.BlockSpec((tm, d), lambda i, j: (i, 0)), vec, vec, vec,
                  pl.BlockSpec((d, tn), lambda i, j: (0, j)),
                  pl.BlockSpec((d, NARROW_COLS), lambda i, j: (0, 0)),
                  pl.BlockSpec((1, tn), lambda i, j: (0, 0)), pl.BlockSpec((1, tn), lambda i, j: (0, 0))],
        out_specs=[pl.BlockSpec((tm, tn), lambda i, j: (i, j)),
                   pl.BlockSpec((tm, NARROW_COLS), lambda i, j: (i, 0))],
        out_shape=[jax.ShapeDtypeStruct((s, WIDE_COLS), BF16), jax.ShapeDtypeStruct((s, NARROW_COLS), F32)],
        scratch_shapes=[pltpu.VMEM((tm, d), BF16)],
        compiler_params=_params("arbitrary", "arbitrary"),
        name="inproj",
    )(x2, row(g), row(scale), row(shift), w_wide, w_narrow, qk_gain.reshape(1, tn), qk_scale.reshape(1, tn))


def _causal_conv4(x, tail, w_ref):
    tb = x.shape[0]
    ext = jnp.concatenate([tail, x], axis=0)
    y = x * w_ref[3:4, :]
    for k in (1, 2, 3):
        y = y + pltpu.roll(ext, k, axis=0)[8:8 + tb] * w_ref[3 - k:4 - k, :]
    return y


def _ssd_kernel(z_ref, xbc_ref, dt_ref, dtt_ref, cw_ref, cb_ref, dtb_ref, dtbt_ref, al_ref, alt_ref,
                dsk_ref, ng_ref, o_ref, tail_scr, act_scr, y_scr, st_scr):
    tb = xbc_ref.shape[0]
    L = SSD_CHUNK
    P = SSM_HEAD_DIM

    @pl.when(pl.program_id(0) == 0)
    def _():
        tail_scr[...] = jnp.zeros_like(tail_scr)
        st_scr[...] = jnp.zeros_like(st_scr)

    xbc = xbc_ref[...].astype(F32)
    act_scr[...] = _silu(_causal_conv4(xbc, tail_scr[...], cw_ref) + cb_ref[...])
    tail_scr[...] = xbc[tb - 8:tb]

    ri = _iota2((L, L), 0)
    ci = _iota2((L, L), 1)
    tril = (ri >= ci).astype(F32)
    causal = ri >= ci
    a_col = -jnp.exp(al_ref[...])
    a_row = -jnp.exp(alt_ref[...])
    expand = (_iota2((LANES, SSM_INNER), 1) // P == _iota2((LANES, SSM_INNER), 0)).astype(F32)
    hpg = SSM_HEADS // SSM_GROUPS
    gw = hpg * P

    for c in range(tb // L):
        rows = slice(c * L, (c + 1) * L)
        dt = _softplus(dt_ref[rows, :] + dtb_ref[...])
        dtt = _softplus(dtt_ref[:, rows] + dtbt_ref[...])
        acum = _dot_hp(tril, dt * a_col)
        acum_t = _dot_hp(dtt * a_row, tril.T)
        per_head = jnp.concatenate([dt, jnp.exp(acum[L - 1:L, :] - acum), jnp.exp(acum)], axis=0)
        per_chan = _dot_hp(per_head, expand)
        dt_e, to_end_e, from_start_e = per_chan[:L], per_chan[L:2 * L], per_chan[2 * L:]
        act = act_scr[rows, :]
        xs = act[:, :SSM_INNER]
        xdt = xs * dt_e
        xdt_b = _bf(xdt)
        xdt_end_b = _bf(xdt * to_end_e)
        off_c = SSM_INNER + SSM_GROUPS * SSM_STATE
        y_in, y_st = [], []
        for g in range(SSM_GROUPS):
            bm = _bf(act[:, SSM_INNER + g * SSM_STATE:SSM_INNER + (g + 1) * SSM_STATE])
            cm = _bf(act[:, off_c + g * SSM_STATE:off_c + (g + 1) * SSM_STATE])
            gmat = _dot_nt(cm, bm)
            st = st_scr[g]
            y_st.append(_dot_nt(cm, _bf(st)))
            new = _dot_tn(xdt_end_b[:, g * gw:(g + 1) * gw], bm)
            for hh in range(hpg):
                h = g * hpg + hh
                seg = jnp.exp(jnp.where(causal, acum[:, h:h + 1] - acum_t[h:h + 1, :], -jnp.inf))
                y_in.append(_dot(_bf(gmat * seg), xdt_b[:, h * P:(h + 1) * P]))
                sl = slice(hh * P, (hh + 1) * P)
                st_scr[g, sl, :] = st[sl] * jnp.exp(acum_t[h:h + 1, L - 1:L]) + new[sl]
        y_scr[rows, :] = (jnp.concatenate(y_in, axis=-1) + jnp.concatenate(y_st, axis=-1) * from_start_e
                          + xs * dsk_ref[...])

    y = y_scr[...] * _silu(z_ref[...].astype(F32))
    gsz = SSM_INNER // SSM_GROUPS
    outs = []
    for g in range(SSM_GROUPS):
        yg = y[:, g * gsz:(g + 1) * gsz]
        yn = yg * lax.rsqrt(jnp.mean(yg * yg, axis=-1, keepdims=True) + EPS)
        outs.append(yn * ng_ref[:, g * gsz:(g + 1) * gsz])
    o_ref[...] = jnp.concatenate(outs, axis=-1)


def _ssd(proj, narrow, dtt, conv_w, conv_b, dt_bias, a_log, d_skip, norm_g):
    s = proj.shape[0]
    tb = min(256, s)
    pad8 = lambda v: jnp.pad(v, (0, LANES - v.shape[0])).reshape(1, LANES)
    colv = lambda v: v.reshape(SSM_HEADS, 1)
    full = lambda shape: pl.BlockSpec(shape, lambda i: (0,) * len(shape))
    return pl.pallas_call(
        _ssd_kernel,
        grid=(s // tb,),
        in_specs=[
            _col_spec(tb, SSM_INNER, COL_Z),
            _col_spec(tb, SSM_XBC, COL_XBC),
            _col_spec(tb, LANES, NCOL_DT),
            pl.BlockSpec((SSM_HEADS, tb), lambda i: (0, i)),
            full((4, SSM_XBC)), full((1, SSM_XBC)), full((1, LANES)), full((SSM_HEADS, 1)),
            full((1, LANES)), full((SSM_HEADS, 1)), full((1, SSM_INNER)), full((1, SSM_INNER)),
        ],
        out_specs=pl.BlockSpec((tb, SSM_INNER), lambda i: (i, 0)),
        out_shape=jax.ShapeDtypeStruct((s, SSM_INNER), F32),
        scratch_shapes=[
            pltpu.VMEM((8, SSM_XBC), F32),
            pltpu.VMEM((tb, SSM_XBC), F32),
            pltpu.VMEM((tb, SSM_INNER), F32),
            pltpu.VMEM((SSM_GROUPS, SSM_HEADS // SSM_GROUPS * SSM_HEAD_DIM, SSM_STATE), F32),
        ],
        compiler_params=_params("arbitrary"),
        name="ssd",
    )(proj, proj, narrow, dtt, conv_w, conv_b.reshape(1, -1), pad8(dt_bias), colv(dt_bias),
      pad8(a_log), colv(a_log), jnp.repeat(d_skip, SSM_HEAD_DIM).reshape(1, -1), norm_g.reshape(1, -1))


def _layout_w_in(w):
    d = w.shape[0]
    z, xbc, dt, sb, gqkv, gab, ggate, br = jnp.split(w, [512, 1536, 1544, 3080, 4616, 4624, 5136], axis=1)
    pad = jnp.zeros((d, LANES - 8), w.dtype)
    wide = jnp.concatenate([br, sb, gqkv, xbc, z, ggate], axis=1).astype(BF16)
    narrow = jnp.concatenate([dt, pad, gab, pad], axis=1).astype(BF16)
    return wide, narrow


def _head_rms(x, g):
    return (x * lax.rsqrt(jnp.mean(x * x, axis=-1, keepdims=True) + EPS)) * g


def _sb_blocks(qs, kns, vs, accs, suffix, masked):
    blk = qs[0].shape[0]
    strict = _iota2((blk, blk), 1) < _iota2((blk, blk), 0)
    zs = [_dot_nt(q, kn) for q, kn in zip(qs, kns)]
    sps = [jnp.maximum(z, 0.0) + jnp.log1p(jnp.exp(-jnp.abs(z))) for z in zs]
    log_keeps = [jnp.where(strict, -sp, 0.0) if masked else -sp for sp in sps]
    splits = [_split2(lk) for lk in log_keeps]
    afters = [(_dot(hi, suffix) + _dot(lo, suffix)) + acc for (hi, lo), acc in zip(splits, accs)]
    atts = [jnp.exp((z - sp) + after) for z, sp, after in zip(zs, sps, afters)]
    if masked:
        atts = [jnp.where(strict, att, 0.0) for att in atts]
    outs = [_dot(_bf(att), v) for att, v in zip(atts, vs)]
    return outs, [jnp.sum(lk, axis=-1, keepdims=True) for lk in log_keeps]


def _sb_kernel(q_ref, k_ref, v_ref, o_ref, acc_scr):
    blk = SB_BLOCK
    dh = SB_HEAD_DIM
    heads = range(SB_HEADS)
    i = pl.program_id(0)
    suffix = (_iota2((blk, blk), 0) > _iota2((blk, blk), 1)).astype(BF16)
    qs = [q_ref[:, h * dh:(h + 1) * dh] for h in heads]

    def load_kv(j):
        rows = pl.ds(pl.multiple_of(j * blk, blk), blk)
        return ([k_ref[rows, h * dh:(h + 1) * dh] for h in heads],
                [v_ref[rows, h * dh:(h + 1) * dh] for h in heads])

    def live(accs):
        top = functools.reduce(jnp.maximum, accs)
        return (jnp.max(top) > SB_SKIP_LOG).astype(jnp.int32)

    kns, vs = load_kv(i)
    outs, sums = _sb_blocks(qs, kns, vs, [jnp.zeros((blk, 1), F32)] * SB_HEADS, suffix, True)
    for h in heads:
        o_ref[:, h * dh:(h + 1) * dh] = outs[h]
        acc_scr[h] = sums[h]

    def cond(carry):
        j, alive = carry
        return jnp.logical_and(j >= 0, alive > 0)

    def body(carry):
        j, _ = carry
        kns, vs = load_kv(j)
        accs = [acc_scr[h] for h in heads]
        outs, sums = _sb_blocks(qs, kns, vs, accs, suffix, False)
        accs = [acc + rs for acc, rs in zip(accs, sums)]
        for h in heads:
            o_ref[:, h * dh:(h + 1) * dh] += outs[h]
            acc_scr[h] = accs[h]
        return j - 1, live(accs)

    lax.while_loop(cond, body, (i - 1, live(sums)))


def _stick_breaking(proj):
    s = proj.shape[0]
    blk = SB_BLOCK
    dh = SB_HEAD_DIM
    width = SB_HEADS * dh
    resident = lambda col: pl.BlockSpec((s, width), lambda i: (0, col // width), pipeline_mode=pl.Buffered(1))
    return pl.pallas_call(
        _sb_kernel,
        grid=(s // blk,),
        in_specs=[_col_spec(blk, width, COL_SB), resident(COL_SB + width), resident(COL_SB + 2 * width)],
        out_specs=pl.BlockSpec((blk, width), lambda i: (i, 0)),
        out_shape=jax.ShapeDtypeStruct((s, width), F32),
        scratch_shapes=[pltpu.VMEM((SB_HEADS, blk, 1), F32)],
        compiler_params=_params("arbitrary"),
        name="stick_breaking",
    )(proj, proj, proj)


def _dot3_nt(a, b):
    ah, al = _split2(a)
    bh, bl = _split2(b)
    return _dot_nt(ah, bh) + (_dot_nt(ah, bl) + _dot_nt(al, bh))


def _chunk_lower_inverses(ms, chunk):
    n = ms[0].shape[0]
    eye = (_iota2((n, n), 0) == _iota2((n, n), 1)).astype(F32)
    ps = [-m for m in ms]
    invs = [eye + p for p in ps]
    for j in range((chunk - 1).bit_length() - 1):
        ps = [_dot3(p, p) if j == 0 else _dot(_bf(p), _bf(p)) for p in ps]
        invs = [inv + _dot3(inv, p) for inv, p in zip(invs, ps)]
    return invs


def _gdn_kernel(qkv_ref, gab_ref, gabt_ref, gate_ref, cw_ref, al_ref, alt_ref, dtb_ref, dtbt_ref, ng_ref,
                o_ref, tail_scr, act_scr, st_scr):
    tb = qkv_ref.shape[0]
    C = GDN_CHUNK
    dh = GDN_HEAD_DIM
    inner = GDN_HEADS * dh
    heads = range(GDN_HEADS)

    @pl.when(pl.program_id(0) == 0)
    def _():
        tail_scr[...] = jnp.zeros_like(tail_scr)
        st_scr[...] = jnp.zeros_like(st_scr)

    raw = qkv_ref[...].astype(F32)
    act_scr[...] = _silu(_causal_conv4(raw, tail_scr[...], cw_ref))
    tail_scr[...] = raw[tb - 8:tb]

    ri = _iota2((tb, tb), 0)
    ci = _iota2((tb, tb), 1)
    same = (ri // C) == (ci // C)
    incl = jnp.logical_and(same, ri >= ci)
    strict = jnp.logical_and(same, ri > ci)
    tril = incl.astype(F32)
    gab = gab_ref[...]
    g_col = -jnp.exp(al_ref[...]) * _softplus(gab + dtb_ref[...])
    beta_col = jax.nn.sigmoid(gab)
    g_row = -jnp.exp(alt_ref[...]) * _softplus(gabt_ref[...] + dtbt_ref[...])
    gc_col = _dot_hp(tril, g_col)
    gc_row = _dot_hp(g_row, tril.T)

    qs, ks, kbs, gcs, decays, rhss = [], [], [], [], [], []
    for h in heads:
        q = act_scr[:, h * dh:(h + 1) * dh]
        k = act_scr[:, inner + h * dh:inner + (h + 1) * dh]
        v = act_scr[:, 2 * inner + h * dh:2 * inner + (h + 1) * dh]
        q = q * lax.rsqrt(jnp.sum(q * q, axis=-1, keepdims=True) + EPS) * (dh ** -0.5)
        k = k * lax.rsqrt(jnp.sum(k * k, axis=-1, keepdims=True) + EPS)
        beta = beta_col[:, GDN_HEADS + h:GDN_HEADS + h + 1]
        gc = gc_col[:, h:h + 1]
        kb = k * beta
        qs.append(q)
        ks.append(k)
        kbs.append(kb)
        gcs.append(gc)
        decays.append(jnp.exp(jnp.where(incl, gc - gc_row[h:h + 1, :], -jnp.inf)))
        rhss.append(jnp.concatenate([v * beta, kb * jnp.exp(gc)], axis=-1))

    ms = [jnp.where(strict, _dot3_nt(kbs[h], ks[h]) * decays[h], 0.0) for h in heads]
    invs = _chunk_lower_inverses(ms, C)
    sols = [_dot3(invs[h], rhss[h]) for h in heads]
    attns = [_bf(jnp.where(incl, _dot_nt(_bf(qs[h]), _bf(ks[h])) * decays[h], 0.0)) for h in heads]
    q_decs = [_bf(qs[h] * jnp.exp(gcs[h])) for h in heads]

    sts = [st_scr[h] for h in heads]
    v_news = [[] for _ in heads]
    o_inters = [[] for _ in heads]
    for c in range(tb // C):
        rows = slice(c * C, (c + 1) * C)
        for h in heads:
            st_b = _bf(sts[h])
            g_last = gc_row[h:h + 1, (c + 1) * C - 1:(c + 1) * C]
            v_new = sols[h][rows, :dh] - _dot(_bf(sols[h][rows, dh:]), st_b)
            v_new_b = _bf(v_new)
            o_inters[h].append(_dot(q_decs[h][rows, :], st_b))
            k_dec = ks[h][rows, :] * jnp.exp(g_last - gcs[h][rows, :])
            sts[h] = sts[h] * jnp.exp(g_last) + _dot_tn(_bf(k_dec), v_new_b)
            v_news[h].append(v_new_b)
    for h in heads:
        st_scr[h] = sts[h]
        o = jnp.concatenate(o_inters[h], axis=0) + _dot(attns[h], jnp.concatenate(v_news[h], axis=0))
        o = _head_rms(o, ng_ref[...]) * _silu(gate_ref[:, h * dh:(h + 1) * dh].astype(F32))
        o_ref[:, h * dh:(h + 1) * dh] = o


def _gdn(proj, narrow, gabt, conv_w, a_log, dt_bias, norm_g):
    s = proj.shape[0]
    tb = min(256, s)
    inner = GDN_HEADS * GDN_HEAD_DIM
    pad_lane = lambda v: jnp.pad(v, (0, LANES - v.shape[0])).reshape(1, LANES)
    pad_col = lambda v: jnp.pad(v, (0, 8 - v.shape[0])).reshape(8, 1)
    full = lambda shape: pl.BlockSpec(shape, lambda i: (0,) * len(shape))
    return pl.pallas_call(
        _gdn_kernel,
        grid=(s // tb,),
        in_specs=[
            _col_spec(tb, 3 * inner, COL_GQKV),
            _col_spec(tb, LANES, NCOL_GAB),
            pl.BlockSpec((8, tb), lambda i: (0, i)),
            _col_spec(tb, inner, COL_GGATE),
            full((4, 3 * inner)), full((1, LANES)), full((8, 1)), full((1, LANES)), full((8, 1)),
            full((1, GDN_HEAD_DIM)),
        ],
        out_specs=pl.BlockSpec((tb, inner), lambda i: (i, 0)),
        out_shape=jax.ShapeDtypeStruct((s, inner), F32),
        scratch_shapes=[
            pltpu.VMEM((8, 3 * inner), F32),
            pltpu.VMEM((tb, 3 * inner), F32),
            pltpu.VMEM((GDN_HEADS, GDN_HEAD_DIM, GDN_HEAD_DIM), F32),
        ],
        compiler_params=_params("arbitrary"),
        name="gdn",
    )(proj, narrow, gabt, proj, conv_w, pad_lane(a_log), pad_col(a_log), pad_lane(dt_bias), pad_col(dt_bias),
      norm_g.reshape(1, -1))


def _merge_kernel(ya_ref, yb_ref, yc_ref, br_ref, x_ref, wbr_ref, wout_ref, gm_ref, g_ref, sc_ref, sh_ref,
                  wrt_ref, brt_ref, xo_ref, h_ref, lg_ref):
    d = x_ref.shape[1]
    merged = None
    for i, y_ref in enumerate((ya_ref, yb_ref, yc_ref)):
        gate = jax.nn.sigmoid(br_ref[:, i * d:(i + 1) * d].astype(F32))
        term = gate * _dot(_bf(y_ref[...]), wbr_ref[i])
        merged = term if merged is None else merged + term
    x_new = x_ref[...] + gm_ref[...] * _dot(_bf(merged), wout_ref[...])
    xo_ref[...] = x_new
    h = _norm_mod(x_new, g_ref[...], sc_ref[...], sh_ref[...])
    h_ref[...] = h
    lg_ref[...] = _dot3(h, wrt_ref[...]) + brt_ref[...]


def _merge(ya, yb, yc, proj, x2, wbr_bf, wout_bf, gate_m, g, scale, shift, w_rt, b_rt):
    s, d = x2.shape
    tb = min(512, s)
    bw = ya.shape[1]
    row = lambda a: a.reshape(1, -1)
    vec = pl.BlockSpec((1, d), lambda i: (0, 0))
    blk = lambda w: pl.BlockSpec((tb, w), lambda i: (i, 0))
    return pl.pallas_call(
        _merge_kernel,
        grid=(s // tb,),
        in_specs=[blk(bw), blk(bw), blk(bw), _col_spec(tb, 3 * d, COL_BR), blk(d),
                  pl.BlockSpec((3, bw, d), lambda i: (0, 0, 0)), pl.BlockSpec((d, d), lambda i: (0, 0)),
                  vec, vec, vec, vec,
                  pl.BlockSpec((d, LANES), lambda i: (0, 0)), pl.BlockSpec((1, LANES), lambda i: (0, 0))],
        out_specs=[blk(d), blk(d), blk(LANES)],
        out_shape=[jax.ShapeDtypeStruct((s, d), F32), jax.ShapeDtypeStruct((s, d), F32),
                   jax.ShapeDtypeStruct((s, LANES), F32)],
        compiler_params=_params("arbitrary"),
        name="merge",
    )(ya, yb, yc, proj, x2, wbr_bf, wout_bf, row(gate_m), row(g), row(scale), row(shift), w_rt, row(b_rt))


ROUTE_E0 = MOE_GROUPS


def _route_kernel(lg_ref, ids_ref, wts_ref, cnt_ref, carry_scr):
    tb = lg_ref.shape[0]

    @pl.when(pl.program_id(0) == 0)
    def _():
        carry_scr[...] = jnp.zeros_like(carry_scr)

    lg = lg_ref[...]
    lane = _iota2((tb, LANES), 1)
    big = jnp.int32(LANES)
    neg = -jnp.inf
    gl = jnp.where(lane < MOE_GROUPS, lg, neg)
    gmax = jnp.max(gl, axis=-1, keepdims=True)
    g_sel = jnp.min(jnp.where(gl == gmax, lane, big), axis=-1, keepdims=True)
    p_group = 1.0 / jnp.sum(jnp.exp(gl - gmax), axis=-1, keepdims=True)
    lo = ROUTE_E0 + EXPERTS_PER_GROUP * g_sel
    el = jnp.where(jnp.logical_and(lane >= lo, lane < lo + EXPERTS_PER_GROUP), lg, neg)
    m1 = jnp.max(el, axis=-1, keepdims=True)
    i1 = jnp.min(jnp.where(el == m1, lane, big), axis=-1, keepdims=True)
    esum = jnp.sum(jnp.exp(el - m1), axis=-1, keepdims=True)
    el2 = jnp.where(lane == i1, neg, el)
    m2 = jnp.max(el2, axis=-1, keepdims=True)
    i2 = jnp.min(jnp.where(el2 == m2, lane, big), axis=-1, keepdims=True)
    p1 = 1.0 / esum
    p2 = jnp.exp(m2 - m1) / esum
    w1 = p_group * p1 / (p1 + p2)
    w2 = p_group * p2 / (p1 + p2)

    sel1 = lane == i1
    sel2 = lane == i2
    onehot = jnp.where(jnp.logical_or(sel1, sel2), 1.0, 0.0)
    before = (_iota2((tb, tb), 0) > _iota2((tb, tb), 1)).astype(BF16)
    seen = _dot(before, _bf(onehot)) + carry_scr[...]
    r1 = jnp.sum(jnp.where(sel1, seen, 0.0), axis=-1, keepdims=True)
    r2 = jnp.sum(jnp.where(sel2, seen, 0.0), axis=-1, keepdims=True)
    carry = carry_scr[...] + jnp.sum(onehot, axis=0, keepdims=True)
    carry_scr[...] = carry
    cnt_ref[...] = jnp.broadcast_to(carry, cnt_ref.shape)

    ids = jnp.where(lane == 0, i1 - ROUTE_E0, jnp.where(lane == 1, i2 - ROUTE_E0,
          jnp.where(lane == 2, r1.astype(jnp.int32), jnp.where(lane == 3, r2.astype(jnp.int32), 0))))
    ids_ref[...] = ids
    wts_ref[...] = jnp.where(lane == 0, w1, jnp.where(lane == 1, w2, 0.0))


def _route(logits):
    s = logits.shape[0]
    tb = min(256, s)
    blk = pl.BlockSpec((tb, LANES), lambda i: (i, 0))
    return pl.pallas_call(
        _route_kernel,
        grid=(s // tb,),
        in_specs=[blk],
        out_specs=[blk, blk, pl.BlockSpec((8, LANES), lambda i: (0, 0))],
        out_shape=[jax.ShapeDtypeStruct((s, LANES), jnp.int32), jax.ShapeDtypeStruct((s, LANES), F32),
                   jax.ShapeDtypeStruct((8, LANES), F32)],
        scratch_shapes=[pltpu.VMEM((1, LANES), F32)],
        compiler_params=_params("arbitrary"),
        name="route",
    )(logits)


EXPERT_BLOCK = 512
ROW_TB = 256


def _dispatch_kernel(dest_ref, h_ref, xb_in_ref, xb_ref, sem):
    del xb_in_ref
    tb = h_ref.shape[0]
    base = pl.program_id(0) * tb * MOE_TOP_K

    def row_copy(t, k, d):
        return pltpu.make_async_copy(h_ref.at[pl.ds(t, 1), :], xb_ref.at[pl.ds(d, 1), :], sem)

    def issue(t, carry):
        for k in range(MOE_TOP_K):
            row_copy(t, k, dest_ref[base + t * MOE_TOP_K + k]).start()
        return carry

    for t in range(tb):
        issue(t, 0)
    for k in range(MOE_TOP_K):
        pltpu.make_async_copy(h_ref, xb_ref.at[pl.ds(0, tb), :], sem).wait()


def _dispatch(dest, h, n_slots):
    s, d = h.shape
    tb = min(ROW_TB, s)
    xb0 = jnp.zeros((n_slots, d), F32)
    return pl.pallas_call(
        _dispatch_kernel,
        grid_spec=pltpu.PrefetchScalarGridSpec(
            num_scalar_prefetch=1,
            grid=(s // tb,),
            in_specs=[pl.BlockSpec((tb, d), lambda i, dest: (i, 0)), pl.BlockSpec(memory_space=pl.ANY)],
            out_specs=pl.BlockSpec(memory_space=pl.ANY),
            scratch_shapes=[pltpu.SemaphoreType.DMA(())],
        ),
        out_shape=jax.ShapeDtypeStruct((n_slots, d), F32),
        input_output_aliases={2: 0},
        compiler_params=_params("arbitrary"),
        name="dispatch",
    )(dest, h, xb0)


def _expert_kernel(be_ref, nused_ref, x_ref, wg_ref, wu_ref, wd_ref, o_ref, wg_b, wu_b, wd_b):
    b = pl.program_id(0)

    @pl.when(jnp.logical_or(b == 0, be_ref[b] != be_ref[jnp.maximum(b - 1, 0)]))
    def _():
        wg_b[...] = _bf(wg_ref[0, 0])
        wu_b[...] = _bf(wu_ref[0, 0])
        wd_b[...] = _bf(wd_ref[0, 0])

    @pl.when(b < nused_ref[0])
    def _():
        x = _bf(x_ref[...])
        hid = _silu(_dot(x, wg_b[...])) * _dot(x, wu_b[...])
        o_ref[...] = _dot(_bf(hid), wd_b[...])

    @pl.when(b >= nused_ref[0])
    def _():
        o_ref[...] = jnp.zeros_like(o_ref)


def _experts(block_expert, n_used, xb, layer, w_gate, w_up, w_down):
    n_slots, d = xb.shape
    ff = w_gate.shape[3]
    nb = n_slots // EXPERT_BLOCK
    return pl.pallas_call(
        _expert_kernel,
        grid_spec=pltpu.PrefetchScalarGridSpec(
            num_scalar_prefetch=2,
            grid=(nb,),
            in_specs=[
                pl.BlockSpec((EXPERT_BLOCK, d), lambda b, be, nu: (b, 0)),
                pl.BlockSpec((1, 1, d, ff), lambda b, be, nu: (layer, be[b], 0, 0)),
                pl.BlockSpec((1, 1, d, ff), lambda b, be, nu: (layer, be[b], 0, 0)),
                pl.BlockSpec((1, 1, ff, d), lambda b, be, nu: (layer, be[b], 0, 0)),
            ],
            out_specs=pl.BlockSpec((EXPERT_BLOCK, d), lambda b, be, nu: (b, 0)),
            scratch_shapes=[pltpu.VMEM((d, ff), BF16), pltpu.VMEM((d, ff), BF16), pltpu.VMEM((ff, d), BF16)],
        ),
        out_shape=jax.ShapeDtypeStruct((n_slots, d), F32),
        compiler_params=_params("arbitrary"),
        name="experts",
    )(block_expert, n_used, xb, w_gate, w_up, w_down)


def _combine_kernel(dest_ref, yb_ref, wts_ref, x_ref, gf_ref, o_ref, buf, sem):
    tb = x_ref.shape[0]
    base = pl.program_id(0) * tb * MOE_TOP_K

    def row_copy(t, k, d):
        return pltpu.make_async_copy(yb_ref.at[pl.ds(d, 1), :], buf.at[k, pl.ds(t, 1), :], sem)

    def issue(t, carry):
        for k in range(MOE_TOP_K):
            row_copy(t, k, dest_ref[base + t * MOE_TOP_K + k]).start()
        return carry

    for t in range(tb):
        issue(t, 0)
    for k in range(MOE_TOP_K):
        pltpu.make_async_copy(yb_ref.at[pl.ds(0, tb), :], buf.at[k], sem).wait()
    wts = wts_ref[...]
    y = wts[:, 0:1] * buf[0] + wts[:, 1:2] * buf[1]
    o_ref[...] = x_ref[...] + gf_ref[...] * y


def _combine(dest, yb, wts, x2, gate_f):
    s, d = x2.shape
    tb = min(ROW_TB, s)
    return pl.pallas_call(
        _combine_kernel,
        grid_spec=pltpu.PrefetchScalarGridSpec(
            num_scalar_prefetch=1,
            grid=(s // tb,),
            in_specs=[pl.BlockSpec(memory_space=pl.ANY),
                      pl.BlockSpec((tb, LANES), lambda i, dest: (i, 0)),
                      pl.BlockSpec((tb, d), lambda i, dest: (i, 0)),
                      pl.BlockSpec((1, d), lambda i, dest: (0, 0))],
            out_specs=pl.BlockSpec((tb, d), lambda i, dest: (i, 0)),
            scratch_shapes=[pltpu.VMEM((MOE_TOP_K, tb, d), F32), pltpu.SemaphoreType.DMA(())],
        ),
        out_shape=jax.ShapeDtypeStruct((s, d), F32),
        compiler_params=_params("arbitrary"),
        name="combine",
    )(dest, yb, wts, x2, gate_f.reshape(1, d))


def _moe(h, logits, x2, gate_f, layer, w_gate, w_up, w_down):
    s = h.shape[0]
    ids, wts, cnt = _route(logits)
    counts = cnt[0, ROUTE_E0:ROUTE_E0 + N_EXPERTS].astype(jnp.int32)
    padded = (counts + EXPERT_BLOCK - 1) // EXPERT_BLOCK * EXPERT_BLOCK
    pend = jnp.cumsum(padded)
    pstart = pend - padded
    is_expert = ids[:, 0:MOE_TOP_K, None] == jnp.arange(N_EXPERTS, dtype=jnp.int32)
    slot0 = jnp.sum(jnp.where(is_expert, pstart, 0), axis=-1)
    dest = (slot0 + ids[:, MOE_TOP_K:2 * MOE_TOP_K]).reshape(s * MOE_TOP_K)
    nb = (s * MOE_TOP_K) // EXPERT_BLOCK + N_EXPERTS
    block_start = jnp.arange(nb, dtype=jnp.int32) * EXPERT_BLOCK
    block_expert = jnp.sum((pend[None, :] <= block_start[:, None]).astype(jnp.int32), axis=1)
    block_expert = jnp.minimum(block_expert, N_EXPERTS - 1)
    n_used = (pend[-1:] // EXPERT_BLOCK).astype(jnp.int32)
    xb = _dispatch(dest, h, nb * EXPERT_BLOCK)
    yb = _experts(block_expert, n_used, xb, layer, w_gate, w_up, w_down)
    return _combine(dest, yb, wts, x2, gate_f)


def kernel(x, c, w_ada, b_ada, norm_mix, norm_ffn, w_in, ssm_conv_w, ssm_conv_b, ssm_dt_bias, ssm_a_log, ssm_d,
           ssm_norm, sb_q_norm, sb_k_norm, gdn_conv_w, gdn_a_log, gdn_dt_bias, gdn_norm, w_branch, w_out,
           w_group, b_group, w_router, b_router, w_gate, w_up, w_down):
    bsz, s, d = x.shape
    assert bsz == 1 and d == D_MODEL
    depth = w_in.shape[0]
    mod = _adaln_mod(c, w_ada, b_ada)
    x2 = x.reshape(s, d)
    for l in range(depth):
        shift_m, scale_m, gate_m, shift_f, scale_f, gate_f = jnp.split(mod[l], 6)
        proj, narrow = _inproj(x2, norm_mix[l], scale_m, shift_m, *_layout_w_in(w_in[l]), sb_q_norm[l], sb_k_norm[l])
        dtt = narrow[:, NCOL_DT:NCOL_DT + 8].T
        gabt = narrow[:, NCOL_GAB:NCOL_GAB + 8].T
        ya = _ssd(proj, narrow, dtt, ssm_conv_w[l], ssm_conv_b[l], ssm_dt_bias[l], ssm_a_log[l], ssm_d[l], ssm_norm[l])
        yb = _stick_breaking(proj)
        yc = _gdn(proj, narrow, gabt, gdn_conv_w[l], gdn_a_log[l], gdn_dt_bias[l], gdn_norm[l])
        pad = jnp.zeros((d, LANES - MOE_GROUPS - N_EXPERTS), F32)
        w_rt = jnp.concatenate([w_group[l], w_router[l], pad], axis=1)
        b_rt = jnp.concatenate([b_group[l], b_router[l], pad[0]])
        x2, h, logits = _merge(ya, yb, yc, proj, x2, _bf(w_branch[l]), _bf(w_out[l]), gate_m,
                               norm_ffn[l], scale_f, shift_f, w_rt, b_rt)
        x2 = _moe(h, logits, x2, gate_f, l, w_gate, w_up, w_down)
    return x2.reshape(bsz, s, d)
```

```python
import functools

import jax
import jax.numpy as jnp
from jax import lax
from jax.experimental import pallas as pl
from jax.experimental.pallas import tpu as pltpu

F32 = jnp.float32
BF16 = jnp.bfloat16
EPS = 1e-6

D_MODEL = 1024
SSM_HEADS = 8
SSM_HEAD_DIM = 64
SSM_INNER = 512
SSM_GROUPS = 2
SSM_STATE = 128
SSM_XBC = 1024
SSD_CHUNK = 128
SB_HEADS = 4
SB_HEAD_DIM = 128
SB_BLOCK = 128
GDN_HEADS = 4
GDN_HEAD_DIM = 128
GDN_CHUNK = 64
MOE_GROUPS = 4
EXPERTS_PER_GROUP = 8
N_EXPERTS = 32
MOE_TOP_K = 2
EXPERT_FF = 512

LANES = 128
COL_BR = 0
COL_SB = 3072
COL_GQKV = 4608
COL_XBC = 6144
COL_Z = 7168
COL_GGATE = 7680
WIDE_COLS = 8192
NCOL_DT = 0
NCOL_GAB = 128
NARROW_COLS = 256


def _col_spec(tb, width, col):
    assert col % width == 0
    return pl.BlockSpec((tb, width), lambda i: (i, col // width))

VMEM_LIMIT = 48 * 1024 * 1024
SB_SKIP_LOG = -110.0


def _bf(x):
    return x.astype(BF16)


def _dot(a, b):
    return jnp.dot(a, b, preferred_element_type=F32)


def _dot_nt(a, b):
    return lax.dot_general(a, b, (((1,), (1,)), ((), ())), preferred_element_type=F32)


def _dot_tn(a, b):
    return lax.dot_general(a, b, (((0,), (0,)), ((), ())), preferred_element_type=F32)


def _dot_hp(a, b):
    return jnp.dot(a, b, preferred_element_type=F32, precision=lax.Precision.HIGHEST)


def _split2(x):
    hi = _bf(x)
    lo = _bf(x - hi.astype(F32))
    return hi, lo


def _dot3(a, b):
    ah, al = _split2(a)
    bh, bl = _split2(b)
    return _dot(ah, bh) + (_dot(ah, bl) + _dot(al, bh))


def _silu(x):
    return x * jax.nn.sigmoid(x)


def _softplus(x):
    return jnp.maximum(x, 0.0) + jnp.log1p(jnp.exp(-jnp.abs(x)))


def _iota2(shape, dim):
    return lax.broadcasted_iota(jnp.int32, shape, dim)


def _params(*sem):
    return pltpu.CompilerParams(dimension_semantics=sem, vmem_limit_bytes=VMEM_LIMIT)


def _mod_kernel(c_ref, w_ref, b_ref, o_ref):
    c = _silu(c_ref[...])
    o_ref[0] = _dot_hp(c, w_ref[0]) + b_ref[0]


def _adaln_mod(c, w_ada, b_ada):
    depth, d, cols = w_ada.shape
    tn = 1024
    c8 = jnp.broadcast_to(c, (8, d))
    out = pl.pallas_call(
        _mod_kernel,
        grid=(depth, cols // tn),
        in_specs=[
            pl.BlockSpec((8, d), lambda l, j: (0, 0)),
            pl.BlockSpec((1, d, tn), lambda l, j: (l, 0, j)),
            pl.BlockSpec((1, 1, tn), lambda l, j: (l, 0, j)),
        ],
        out_specs=pl.BlockSpec((1, 8, tn), lambda l, j: (l, 0, j)),
        out_shape=jax.ShapeDtypeStruct((depth, 8, cols), F32),
        compiler_params=_params("arbitrary", "arbitrary"),
        name="adaln_mod",
    )(c8, w_ada, b_ada.reshape(depth, 1, cols))
    return out[:, 0, :]


def _norm_mod(x, g, scale, shift):
    y = x * lax.rsqrt(jnp.mean(x * x, axis=-1, keepdims=True) + EPS)
    return (y * g) * (1.0 + scale) + shift


INPROJ_TN = 1024
assert COL_SB % INPROJ_TN == 0 and 2 * SB_HEADS * SB_HEAD_DIM == INPROJ_TN


def _inproj_kernel(x_ref, g_ref, sc_ref, sh_ref, w_ref, wn_ref, qkg_ref, qks_ref, o_ref, on_ref, h_scr):
    j = pl.program_id(1)

    @pl.when(j == 0)
    def _():
        h = _bf(_norm_mod(x_ref[...], g_ref[...], sc_ref[...], sh_ref[...]))
        h_scr[...] = h
        on_ref[...] = _dot(h, wn_ref[...])

    @pl.when(j != COL_SB // INPROJ_TN)
    def _():
        o_ref[...] = _bf(_dot(h_scr[...], w_ref[...]))

    @pl.when(j == COL_SB // INPROJ_TN)
    def _():
        acc = _dot(h_scr[...], w_ref[...])
        dh = SB_HEAD_DIM
        for n in range(INPROJ_TN // dh):
            cols = slice(n * dh, (n + 1) * dh)
            o_ref[:, cols] = _bf(_head_rms(acc[:, cols], qkg_ref[:, cols]) * qks_ref[:, cols])


def _inproj(x2, g, scale, shift, w_wide, w_narrow, q_g, k_g):
    s, d = x2.shape
    tm = min(1024, s)
    tn = INPROJ_TN
    row = lambda a: a.reshape(1, d)
    vec = pl.BlockSpec((1, d), lambda i, j: (0, 0))
    qk_gain = jnp.concatenate([jnp.tile(q_g, SB_HEADS), jnp.tile(k_g, SB_HEADS)])
    qk_scale = jnp.concatenate([jnp.full((tn // 2,), SB_HEAD_DIM ** -0.5, F32), jnp.ones((tn // 2,), F32)])
    return pl.pallas_call(
        _inproj_kernel,
        grid=(s // tm, WIDE_COLS // tn),
        in_specs=[pl.BlockSpec((tm, d), lambda i, j: (i, 0)), vec, vec, vec,
                  pl.BlockSpec((d, tn), lambda i, j: (0, j)),
                  pl.BlockSpec((d, NARROW_COLS), lambda i, j: (0, 0)),
                  pl.BlockSpec((1, tn), lambda i, j: (0, 0)), pl.BlockSpec((1, tn), lambda i, j: (0, 0))],
        out_specs=[pl.BlockSpec((tm, tn), lambda i, j: (i, j)),
                   pl.BlockSpec((tm, NARROW_COLS), lambda i, j: (i, 0))],
        out_shape=[jax.ShapeDtypeStruct((s, WIDE_COLS), BF16), jax.ShapeDtypeStruct((s, NARROW_COLS), F32)],
        scratch_shapes=[pltpu.VMEM((tm, d), BF16)],
        compiler_params=_params("arbitrary", "arbitrary"),
        name="inproj",
    )(x2, row(g), row(scale), row(shift), w_wide, w_narrow, qk_gain.reshape(1, tn), qk_scale.reshape(1, tn))


def _causal_conv4(x, tail, w_ref):
    tb = x.shape[0]
    ext = jnp.concatenate([tail, x], axis=0)
    y = x * w_ref[3:4, :]
    for k in (1, 2, 3):
        y = y + pltpu.roll(ext, k, axis=0)[8:8 + tb] * w_ref[3 - k:4 - k, :]
    return y


def _ssd_kernel(z_ref, xbc_ref, dt_ref, dtt_ref, cw_ref, cb_ref, dtb_ref, dtbt_ref, al_ref, alt_ref,
                dsk_ref, ng_ref, o_ref, tail_scr, act_scr, y_scr, st_scr):
    tb = xbc_ref.shape[0]
    L = SSD_CHUNK
    P = SSM_HEAD_DIM

    @pl.when(pl.program_id(0) == 0)
    def _():
        tail_scr[...] = jnp.zeros_like(tail_scr)
        st_scr[...] = jnp.zeros_like(st_scr)

    xbc = xbc_ref[...].astype(F32)
    act_scr[...] = _silu(_causal_conv4(xbc, tail_scr[...], cw_ref) + cb_ref[...])
    tail_scr[...] = xbc[tb - 8:tb]

    ri = _iota2((L, L), 0)
    ci = _iota2((L, L), 1)
    tril = (ri >= ci).astype(F32)
    causal = ri >= ci
    a_col = -jnp.exp(al_ref[...])
    a_row = -jnp.exp(alt_ref[...])
    expand = (_iota2((LANES, SSM_INNER), 1) // P == _iota2((LANES, SSM_INNER), 0)).astype(F32)
    hpg = SSM_HEADS // SSM_GROUPS
    gw = hpg * P

    for c in range(tb // L):
        rows = slice(c * L, (c + 1) * L)
        dt = _softplus(dt_ref[rows, :] + dtb_ref[...])
        dtt = _softplus(dtt_ref[:, rows] + dtbt_ref[...])
        acum = _dot_hp(tril, dt * a_col)
        acum_t = _dot_hp(dtt * a_row, tril.T)
        per_head = jnp.concatenate([dt, jnp.exp(acum[L - 1:L, :] - acum), jnp.exp(acum)], axis=0)
        per_chan = _dot_hp(per_head, expand)
        dt_e, to_end_e, from_start_e = per_chan[:L], per_chan[L:2 * L], per_chan[2 * L:]
        act = act_scr[rows, :]
        xs = act[:, :SSM_INNER]
        xdt = xs * dt_e
        xdt_b = _bf(xdt)
        xdt_end_b = _bf(xdt * to_end_e)
        off_c = SSM_INNER + SSM_GROUPS * SSM_STATE
        y_in, y_st = [], []
        for g in range(SSM_GROUPS):
            bm = _bf(act[:, SSM_INNER + g * SSM_STATE:SSM_INNER + (g + 1) * SSM_STATE])
            cm = _bf(act[:, off_c + g * SSM_STATE:off_c + (g + 1) * SSM_STATE])
            gmat = _dot_nt(cm, bm)
            st = st_scr[g]
            y_st.append(_dot_nt(cm, _bf(st)))
            new = _dot_tn(xdt_end_b[:, g * gw:(g + 1) * gw], bm)
            for hh in range(hpg):
                h = g * hpg + hh
                seg = jnp.exp(jnp.where(causal, acum[:, h:h + 1] - acum_t[h:h + 1, :], -jnp.inf))
                y_in.append(_dot(_bf(gmat * seg), xdt_b[:, h * P:(h + 1) * P]))
                sl = slice(hh * P, (hh + 1) * P)
                st_scr[g, sl, :] = st[sl] * jnp.exp(acum_t[h:h + 1, L - 1:L]) + new[sl]
        y_scr[rows, :] = (jnp.concatenate(y_in, axis=-1) + jnp.concatenate(y_st, axis=-1) * from_start_e
                          + xs * dsk_ref[...])

    y = y_scr[...] * _silu(z_ref[...].astype(F32))
    gsz = SSM_INNER // SSM_GROUPS
    outs = []
    for g in range(SSM_GROUPS):
        yg = y[:, g * gsz:(g + 1) * gsz]
        yn = yg * lax.rsqrt(jnp.mean(yg * yg, axis=-1, keepdims=True) + EPS)
        outs.append(yn * ng_ref[:, g * gsz:(g + 1) * gsz])
    o_ref[...] = jnp.concatenate(outs, axis=-1)


def _ssd(proj, narrow, dtt, conv_w, conv_b, dt_bias, a_log, d_skip, norm_g):
    s = proj.shape[0]
    tb = min(256, s)
    pad8 = lambda v: jnp.pad(v, (0, LANES - v.shape[0])).reshape(1, LANES)
    colv = lambda v: v.reshape(SSM_HEADS, 1)
    full = lambda shape: pl.BlockSpec(shape, lambda i: (0,) * len(shape))
    return pl.pallas_call(
        _ssd_kernel,
        grid=(s // tb,),
        in_specs=[
            _col_spec(tb, SSM_INNER, COL_Z),
            _col_spec(tb, SSM_XBC, COL_XBC),
            _col_spec(tb, LANES, NCOL_DT),
            pl.BlockSpec((SSM_HEADS, tb), lambda i: (0, i)),
            full((4, SSM_XBC)), full((1, SSM_XBC)), full((1, LANES)), full((SSM_HEADS, 1)),
            full((1, LANES)), full((SSM_HEADS, 1)), full((1, SSM_INNER)), full((1, SSM_INNER)),
        ],
        out_specs=pl.BlockSpec((tb, SSM_INNER), lambda i: (i, 0)),
        out_shape=jax.ShapeDtypeStruct((s, SSM_INNER), F32),
        scratch_shapes=[
            pltpu.VMEM((8, SSM_XBC), F32),
            pltpu.VMEM((tb, SSM_XBC), F32),
            pltpu.VMEM((tb, SSM_INNER), F32),
            pltpu.VMEM((SSM_GROUPS, SSM_HEADS // SSM_GROUPS * SSM_HEAD_DIM, SSM_STATE), F32),
        ],
        compiler_params=_params("arbitrary"),
        name="ssd",
    )(proj, proj, narrow, dtt, conv_w, conv_b.reshape(1, -1), pad8(dt_bias), colv(dt_bias),
      pad8(a_log), colv(a_log), jnp.repeat(d_skip, SSM_HEAD_DIM).reshape(1, -1), norm_g.reshape(1, -1))


def _layout_w_in(w):
    d = w.shape[0]
    z, xbc, dt, sb, gqkv, gab, ggate, br = jnp.split(w, [512, 1536, 1544, 3080, 4616, 4624, 5136], axis=1)
    pad = jnp.zeros((d, LANES - 8), w.dtype)
    wide = jnp.concatenate([br, sb, gqkv, xbc, z, ggate], axis=1).astype(BF16)
    narrow = jnp.concatenate([dt, pad, gab, pad], axis=1).astype(BF16)
    return wide, narrow


def _head_rms(x, g):
    return (x * lax.rsqrt(jnp.mean(x * x, axis=-1, keepdims=True) + EPS)) * g


def _sb_blocks(qs, kns, vs, accs, suffix, masked):
    blk = qs[0].shape[0]
    strict = _iota2((blk, blk), 1) < _iota2((blk, blk), 0)
    zs = [_dot_nt(q, kn) for q, kn in zip(qs, kns)]
    sps = [jnp.maximum(z, 0.0) + jnp.log1p(jnp.exp(-jnp.abs(z))) for z in zs]
    log_keeps = [jnp.where(strict, -sp, 0.0) if masked else -sp for sp in sps]
    splits = [_split2(lk) for lk in log_keeps]
    afters = [(_dot(hi, suffix) + _dot(lo, suffix)) + acc for (hi, lo), acc in zip(splits, accs)]
    atts = [jnp.exp((z - sp) + after) for z, sp, after in zip(zs, sps, afters)]
    if masked:
        atts = [jnp.where(strict, att, 0.0) for att in atts]
    outs = [_dot(_bf(att), v) for att, v in zip(atts, vs)]
    return outs, [jnp.sum(lk, axis=-1, keepdims=True) for lk in log_keeps]


SB_QBLOCKS = 2


def _sb_kernel(q_ref, k_ref, v_ref, o_ref, acc_scr):
    blk = SB_BLOCK
    dh = SB_HEAD_DIM
    first = pl.program_id(0) * SB_QBLOCKS
    pairs = [(b, h) for b in range(SB_QBLOCKS) for h in range(SB_HEADS)]
    suffix = (_iota2((blk, blk), 0) > _iota2((blk, blk), 1)).astype(BF16)
    qs = [q_ref[b * blk:(b + 1) * blk, h * dh:(h + 1) * dh] for b, h in pairs]

    def load_kv(offset):
        rows = [pl.ds(pl.multiple_of(jnp.maximum(first + b - offset, 0) * blk, blk), blk) for b in range(SB_QBLOCKS)]
        return ([k_ref[rows[b], h * dh:(h + 1) * dh] for b, h in pairs],
                [v_ref[rows[b], h * dh:(h + 1) * dh] for b, h in pairs])

    def live(accs):
        top = functools.reduce(jnp.maximum, accs)
        return (jnp.max(top) > SB_SKIP_LOG).astype(jnp.int32)

    def out_slice(n):
        b, h = pairs[n]
        return (slice(b * blk, (b + 1) * blk), slice(h * dh, (h + 1) * dh))

    kns, vs = load_kv(0)
    outs, sums = _sb_blocks(qs, kns, vs, [jnp.zeros((blk, 1), F32)] * len(pairs), suffix, True)
    for n in range(len(pairs)):
        o_ref[out_slice(n)] = outs[n]
        acc_scr[n] = sums[n]

    def cond(carry):
        offset, alive = carry
        return jnp.logical_and(offset <= first + SB_QBLOCKS - 1, alive > 0)

    def body(carry):
        offset, _ = carry
        kns, vs = load_kv(offset)
        accs = [acc_scr[n] for n in range(len(pairs))]
        outs, sums = _sb_blocks(qs, kns, vs, accs, suffix, False)
        valid = [jnp.where(first + b - offset >= 0, 1.0, 0.0) for b in range(SB_QBLOCKS)]
        accs = [acc + rs * valid[b] for acc, rs, (b, _) in zip(accs, sums, pairs)]
        for n in range(len(pairs)):
            o_ref[out_slice(n)] += outs[n] * valid[pairs[n][0]]
            acc_scr[n] = accs[n]
        return offset + 1, live(accs)

    lax.while_loop(cond, body, (jnp.int32(1), live(sums)))


def _stick_breaking(proj):
    s = proj.shape[0]
    tq = SB_QBLOCKS * SB_BLOCK
    dh = SB_HEAD_DIM
    width = SB_HEADS * dh
    resident = lambda col: pl.BlockSpec((s, width), lambda i: (0, col // width), pipeline_mode=pl.Buffered(1))
    return pl.pallas_call(
        _sb_kernel,
        grid=(s // tq,),
        in_specs=[_col_spec(tq, width, COL_SB), resident(COL_SB + width), resident(COL_SB + 2 * width)],
        out_specs=pl.BlockSpec((tq, width), lambda i: (i, 0)),
        out_shape=jax.ShapeDtypeStruct((s, width), F32),
        scratch_shapes=[pltpu.VMEM((SB_QBLOCKS * SB_HEADS, SB_BLOCK, 1), F32)],
        compiler_params=_params("arbitrary"),
        name="stick_breaking",
    )(proj, proj, proj)


def _dot3_nt(a, b):
    ah, al = _split2(a)
    bh, bl = _split2(b)
    return _dot_nt(ah, bh) + (_dot_nt(ah, bl) + _dot_nt(al, bh))


def _chunk_lower_inverses(ms, chunk):
    n = ms[0].shape[0]
    eye = (_iota2((n, n), 0) == _iota2((n, n), 1)).astype(F32)
    ps = [-m for m in ms]
    invs = [eye + p for p in ps]
    for j in range((chunk - 1).bit_length() - 1):
        ps = [_dot3(p, p) if j == 0 else _dot(_bf(p), _bf(p)) for p in ps]
        invs = [inv + _dot3(inv, p) for inv, p in zip(invs, ps)]
    return invs


def _gdn_kernel(qkv_ref, gab_ref, gabt_ref, gate_ref, cw_ref, al_ref, alt_ref, dtb_ref, dtbt_ref, ng_ref,
                o_ref, tail_scr, act_scr, st_scr):
    tb = qkv_ref.shape[0]
    C = GDN_CHUNK
    dh = GDN_HEAD_DIM
    inner = GDN_HEADS * dh
    heads = range(GDN_HEADS)

    @pl.when(pl.program_id(0) == 0)
    def _():
        tail_scr[...] = jnp.zeros_like(tail_scr)
        st_scr[...] = jnp.zeros_like(st_scr)

    raw = qkv_ref[...].astype(F32)
    act_scr[...] = _silu(_causal_conv4(raw, tail_scr[...], cw_ref))
    tail_scr[...] = raw[tb - 8:tb]

    ri = _iota2((tb, tb), 0)
    ci = _iota2((tb, tb), 1)
    same = (ri // C) == (ci // C)
    incl = jnp.logical_and(same, ri >= ci)
    strict = jnp.logical_and(same, ri > ci)
    tril = incl.astype(F32)
    gab = gab_ref[...]
    g_col = -jnp.exp(al_ref[...]) * _softplus(gab + dtb_ref[...])
    beta_col = jax.nn.sigmoid(gab)
    g_row = -jnp.exp(alt_ref[...]) * _softplus(gabt_ref[...] + dtbt_ref[...])
    gc_col = _dot_hp(tril, g_col)
    gc_row = _dot_hp(g_row, tril.T)

    qs, ks, kbs, gcs, decays, rhss = [], [], [], [], [], []
    for h in heads:
        q = act_scr[:, h * dh:(h + 1) * dh]
        k = act_scr[:, inner + h * dh:inner + (h + 1) * dh]
        v = act_scr[:, 2 * inner + h * dh:2 * inner + (h + 1) * dh]
        q = q * lax.rsqrt(jnp.sum(q * q, axis=-1, keepdims=True) + EPS) * (dh ** -0.5)
        k = k * lax.rsqrt(jnp.sum(k * k, axis=-1, keepdims=True) + EPS)
        beta = beta_col[:, GDN_HEADS + h:GDN_HEADS + h + 1]
        gc = gc_col[:, h:h + 1]
        kb = k * beta
        qs.append(q)
        ks.append(k)
        kbs.append(kb)
        gcs.append(gc)
        decays.append(jnp.exp(jnp.where(incl, gc - gc_row[h:h + 1, :], -jnp.inf)))
        rhss.append(jnp.concatenate([v * beta, kb * jnp.exp(gc)], axis=-1))

    ms = [jnp.where(strict, _dot3_nt(kbs[h], ks[h]) * decays[h], 0.0) for h in heads]
    invs = _chunk_lower_inverses(ms, C)
    sols = [_dot3(invs[h], rhss[h]) for h in heads]
    attns = [_bf(jnp.where(incl, _dot_nt(_bf(qs[h]), _bf(ks[h])) * decays[h], 0.0)) for h in heads]
    q_decs = [_bf(qs[h] * jnp.exp(gcs[h])) for h in heads]

    sts = [st_scr[h] for h in heads]
    v_news = [[] for _ in heads]
    o_inters = [[] for _ in heads]
    for c in range(tb // C):
        rows = slice(c * C, (c + 1) * C)
        for h in heads:
            st_b = _bf(sts[h])
            g_last = gc_row[h:h + 1, (c + 1) * C - 1:(c + 1) * C]
            v_new = sols[h][rows, :dh] - _dot(_bf(sols[h][rows, dh:]), st_b)
            v_new_b = _bf(v_new)
            o_inters[h].append(_dot(q_decs[h][rows, :], st_b))
            k_dec = ks[h][rows, :] * jnp.exp(g_last - gcs[h][rows, :])
            sts[h] = sts[h] * jnp.exp(g_last) + _dot_tn(_bf(k_dec), v_new_b)
            v_news[h].append(v_new_b)
    for h in heads:
        st_scr[h] = sts[h]
        o = jnp.concatenate(o_inters[h], axis=0) + _dot(attns[h], jnp.concatenate(v_news[h], axis=0))
        o = _head_rms(o, ng_ref[...]) * _silu(gate_ref[:, h * dh:(h + 1) * dh].astype(F32))
        o_ref[:, h * dh:(h + 1) * dh] = o


def _gdn(proj, narrow, gabt, conv_w, a_log, dt_bias, norm_g):
    s = proj.shape[0]
    tb = min(256, s)
    inner = GDN_HEADS * GDN_HEAD_DIM
    pad_lane = lambda v: jnp.pad(v, (0, LANES - v.shape[0])).reshape(1, LANES)
    pad_col = lambda v: jnp.pad(v, (0, 8 - v.shape[0])).reshape(8, 1)
    full = lambda shape: pl.BlockSpec(shape, lambda i: (0,) * len(shape))
    return pl.pallas_call(
        _gdn_kernel,
        grid=(s // tb,),
        in_specs=[
            _col_spec(tb, 3 * inner, COL_GQKV),
            _col_spec(tb, LANES, NCOL_GAB),
            pl.BlockSpec((8, tb), lambda i: (0, i)),
            _col_spec(tb, inner, COL_GGATE),
            full((4, 3 * inner)), full((1, LANES)), full((8, 1)), full((1, LANES)), full((8, 1)),
            full((1, GDN_HEAD_DIM)),
        ],
        out_specs=pl.BlockSpec((tb, inner), lambda i: (i, 0)),
        out_shape=jax.ShapeDtypeStruct((s, inner), F32),
        scratch_shapes=[
            pltpu.VMEM((8, 3 * inner), F32),
            pltpu.VMEM((tb, 3 * inner), F32),
            pltpu.VMEM((GDN_HEADS, GDN_HEAD_DIM, GDN_HEAD_DIM), F32),
        ],
        compiler_params=_params("arbitrary"),
        name="gdn",
    )(proj, narrow, gabt, proj, conv_w, pad_lane(a_log), pad_col(a_log), pad_lane(dt_bias), pad_col(dt_bias),
      norm_g.reshape(1, -1))


def _merge_kernel(ya_ref, yb_ref, yc_ref, br_ref, x_ref, wbr_ref, wout_ref, gm_ref, g_ref, sc_ref, sh_ref,
                  wrt_ref, brt_ref, xo_ref, h_ref, lg_ref):
    d = x_ref.shape[1]
    merged = None
    for i, y_ref in enumerate((ya_ref, yb_ref, yc_ref)):
        gate = jax.nn.sigmoid(br_ref[:, i * d:(i + 1) * d].astype(F32))
        term = gate * _dot(_bf(y_ref[...]), wbr_ref[i])
        merged = term if merged is None else merged + term
    x_new = x_ref[...] + gm_ref[...] * _dot(_bf(merged), wout_ref[...])
    xo_ref[...] = x_new
    h = _norm_mod(x_new, g_ref[...], sc_ref[...], sh_ref[...])
    h_ref[...] = h
    lg_ref[...] = _dot3(h, wrt_ref[...]) + brt_ref[...]


def _merge(ya, yb, yc, proj, x2, wbr_bf, wout_bf, gate_m, g, scale, shift, w_rt, b_rt):
    s, d = x2.shape
    tb = min(512, s)
    bw = ya.shape[1]
    row = lambda a: a.reshape(1, -1)
    vec = pl.BlockSpec((1, d), lambda i: (0, 0))
    blk = lambda w: pl.BlockSpec((tb, w), lambda i: (i, 0))
    return pl.pallas_call(
        _merge_kernel,
        grid=(s // tb,),
        in_specs=[blk(bw), blk(bw), blk(bw), _col_spec(tb, 3 * d, COL_BR), blk(d),
                  pl.BlockSpec((3, bw, d), lambda i: (0, 0, 0)), pl.BlockSpec((d, d), lambda i: (0, 0)),
                  vec, vec, vec, vec,
                  pl.BlockSpec((d, LANES), lambda i: (0, 0)), pl.BlockSpec((1, LANES), lambda i: (0, 0))],
        out_specs=[blk(d), blk(d), blk(LANES)],
        out_shape=[jax.ShapeDtypeStruct((s, d), F32), jax.ShapeDtypeStruct((s, d), F32),
                   jax.ShapeDtypeStruct((s, LANES), F32)],
        compiler_params=_params("arbitrary"),
        name="merge",
    )(ya, yb, yc, proj, x2, wbr_bf, wout_bf, row(gate_m), row(g), row(scale), row(shift), w_rt, row(b_rt))


ROUTE_E0 = MOE_GROUPS


def _route_kernel(lg_ref, ids_ref, wts_ref, cnt_ref, carry_scr):
    tb = lg_ref.shape[0]

    @pl.when(pl.program_id(0) == 0)
    def _():
        carry_scr[...] = jnp.zeros_like(carry_scr)

    lg = lg_ref[...]
    lane = _iota2((tb, LANES), 1)
    big = jnp.int32(LANES)
    neg = -jnp.inf
    gl = jnp.where(lane < MOE_GROUPS, lg, neg)
    gmax = jnp.max(gl, axis=-1, keepdims=True)
    g_sel = jnp.min(jnp.where(gl == gmax, lane, big), axis=-1, keepdims=True)
    p_group = 1.0 / jnp.sum(jnp.exp(gl - gmax), axis=-1, keepdims=True)
    lo = ROUTE_E0 + EXPERTS_PER_GROUP * g_sel
    el = jnp.where(jnp.logical_and(lane >= lo, lane < lo + EXPERTS_PER_GROUP), lg, neg)
    m1 = jnp.max(el, axis=-1, keepdims=True)
    i1 = jnp.min(jnp.where(el == m1, lane, big), axis=-1, keepdims=True)
    esum = jnp.sum(jnp.exp(el - m1), axis=-1, keepdims=True)
    el2 = jnp.where(lane == i1, neg, el)
    m2 = jnp.max(el2, axis=-1, keepdims=True)
    i2 = jnp.min(jnp.where(el2 == m2, lane, big), axis=-1, keepdims=True)
    p1 = 1.0 / esum
    p2 = jnp.exp(m2 - m1) / esum
    w1 = p_group * p1 / (p1 + p2)
    w2 = p_group * p2 / (p1 + p2)

    sel1 = lane == i1
    sel2 = lane == i2
    onehot = jnp.where(jnp.logical_or(sel1, sel2), 1.0, 0.0)
    before = (_iota2((tb, tb), 0) > _iota2((tb, tb), 1)).astype(BF16)
    seen = _dot(before, _bf(onehot)) + carry_scr[...]
    r1 = jnp.sum(jnp.where(sel1, seen, 0.0), axis=-1, keepdims=True)
    r2 = jnp.sum(jnp.where(sel2, seen, 0.0), axis=-1, keepdims=True)
    carry = carry_scr[...] + jnp.sum(onehot, axis=0, keepdims=True)
    carry_scr[...] = carry
    cnt_ref[...] = jnp.broadcast_to(carry, cnt_ref.shape)

    ids = jnp.where(lane == 0, i1 - ROUTE_E0, jnp.where(lane == 1, i2 - ROUTE_E0,
          jnp.where(lane == 2, r1.astype(jnp.int32), jnp.where(lane == 3, r2.astype(jnp.int32), 0))))
    ids_ref[...] = ids
    wts_ref[...] = jnp.where(lane == 0, w1, jnp.where(lane == 1, w2, 0.0))


def _route(logits):
    s = logits.shape[0]
    tb = min(256, s)
    blk = pl.BlockSpec((tb, LANES), lambda i: (i, 0))
    return pl.pallas_call(
        _route_kernel,
        grid=(s // tb,),
        in_specs=[blk],
        out_specs=[blk, blk, pl.BlockSpec((8, LANES), lambda i: (0, 0))],
        out_shape=[jax.ShapeDtypeStruct((s, LANES), jnp.int32), jax.ShapeDtypeStruct((s, LANES), F32),
                   jax.ShapeDtypeStruct((8, LANES), F32)],
        scratch_shapes=[pltpu.VMEM((1, LANES), F32)],
        compiler_params=_params("arbitrary"),
        name="route",
    )(logits)


EXPERT_BLOCK = 512
ROW_TB = 256


def _dispatch_kernel(dest_ref, h_ref, xb_in_ref, xb_ref, sem):
    del xb_in_ref
    tb = h_ref.shape[0]
    base = pl.program_id(0) * tb * MOE_TOP_K

    def row_copy(t, k, d):
        return pltpu.make_async_copy(h_ref.at[pl.ds(t, 1), :], xb_ref.at[pl.ds(d, 1), :], sem)

    def issue(t, carry):
        for k in range(MOE_TOP_K):
            row_copy(t, k, dest_ref[base + t * MOE_TOP_K + k]).start(priority=k % 2)
        return carry

    for t in range(tb):
        issue(t, 0)
    for k in range(MOE_TOP_K):
        pltpu.make_async_copy(h_ref, xb_ref.at[pl.ds(0, tb), :], sem).wait()


def _dispatch(dest, h, n_slots):
    s, d = h.shape
    tb = min(ROW_TB, s)
    xb0 = jnp.zeros((n_slots, d), F32)
    return pl.pallas_call(
        _dispatch_kernel,
        grid_spec=pltpu.PrefetchScalarGridSpec(
            num_scalar_prefetch=1,
            grid=(s // tb,),
            in_specs=[pl.BlockSpec((tb, d), lambda i, dest: (i, 0)), pl.BlockSpec(memory_space=pl.ANY)],
            out_specs=pl.BlockSpec(memory_space=pl.ANY),
            scratch_shapes=[pltpu.SemaphoreType.DMA(())],
        ),
        out_shape=jax.ShapeDtypeStruct((n_slots, d), F32),
        input_output_aliases={2: 0},
        compiler_params=_params("arbitrary"),
        name="dispatch",
    )(dest, h, xb0)


def _expert_kernel(be_ref, nused_ref, x_ref, wg_ref, wu_ref, wd_ref, o_ref, wg_b, wu_b, wd_b):
    b = pl.program_id(0)

    @pl.when(jnp.logical_or(b == 0, be_ref[b] != be_ref[jnp.maximum(b - 1, 0)]))
    def _():
        wg_b[...] = _bf(wg_ref[0, 0])
        wu_b[...] = _bf(wu_ref[0, 0])
        wd_b[...] = _bf(wd_ref[0, 0])

    @pl.when(b < nused_ref[0])
    def _():
        x = _bf(x_ref[...])
        hid = _silu(_dot(x, wg_b[...])) * _dot(x, wu_b[...])
        o_ref[...] = _dot(_bf(hid), wd_b[...])

    @pl.when(b >= nused_ref[0])
    def _():
        o_ref[...] = jnp.zeros_like(o_ref)


def _experts(block_expert, n_used, xb, layer, w_gate, w_up, w_down):
    n_slots, d = xb.shape
    ff = w_gate.shape[3]
    nb = n_slots // EXPERT_BLOCK
    return pl.pallas_call(
        _expert_kernel,
        grid_spec=pltpu.PrefetchScalarGridSpec(
            num_scalar_prefetch=2,
            grid=(nb,),
            in_specs=[
                pl.BlockSpec((EXPERT_BLOCK, d), lambda b, be, nu: (b, 0)),
                pl.BlockSpec((1, 1, d, ff), lambda b, be, nu: (layer, be[b], 0, 0)),
                pl.BlockSpec((1, 1, d, ff), lambda b, be, nu: (layer, be[b], 0, 0)),
                pl.BlockSpec((1, 1, ff, d), lambda b, be, nu: (layer, be[b], 0, 0)),
            ],
            out_specs=pl.BlockSpec((EXPERT_BLOCK, d), lambda b, be, nu: (b, 0)),
            scratch_shapes=[pltpu.VMEM((d, ff), BF16), pltpu.VMEM((d, ff), BF16), pltpu.VMEM((ff, d), BF16)],
        ),
        out_shape=jax.ShapeDtypeStruct((n_slots, d), F32),
        compiler_params=_params("arbitrary"),
        name="experts",
    )(block_expert, n_used, xb, w_gate, w_up, w_down)


def _combine_kernel(dest_ref, yb_ref, wts_ref, x_ref, gf_ref, o_ref, buf, sem):
    tb = x_ref.shape[0]
    base = pl.program_id(0) * tb * MOE_TOP_K

    def row_copy(t, k, d):
        return pltpu.make_async_copy(yb_ref.at[pl.ds(d, 1), :], buf.at[k, pl.ds(t, 1), :], sem)

    def issue(t, carry):
        for k in range(MOE_TOP_K):
            row_copy(t, k, dest_ref[base + t * MOE_TOP_K + k]).start(priority=k % 2)
        return carry

    for t in range(tb):
        issue(t, 0)
    for k in range(MOE_TOP_K):
        pltpu.make_async_copy(yb_ref.at[pl.ds(0, tb), :], buf.at[k], sem).wait()
    wts = wts_ref[...]
    y = wts[:, 0:1] * buf[0] + wts[:, 1:2] * buf[1]
    o_ref[...] = x_ref[...] + gf_ref[...] * y


def _combine(dest, yb, wts, x2, gate_f):
    s, d = x2.shape
    tb = min(ROW_TB, s)
    return pl.pallas_call(
        _combine_kernel,
        grid_spec=pltpu.PrefetchScalarGridSpec(
            num_scalar_prefetch=1,
            grid=(s // tb,),
            in_specs=[pl.BlockSpec(memory_space=pl.ANY),
                      pl.BlockSpec((tb, LANES), lambda i, dest: (i, 0)),
                      pl.BlockSpec((tb, d), lambda i, dest: (i, 0)),
                      pl.BlockSpec((1, d), lambda i, dest: (0, 0))],
            out_specs=pl.BlockSpec((tb, d), lambda i, dest: (i, 0)),
            scratch_shapes=[pltpu.VMEM((MOE_TOP_K, tb, d), F32), pltpu.SemaphoreType.DMA(())],
        ),
        out_shape=jax.ShapeDtypeStruct((s, d), F32),
        compiler_params=_params("arbitrary"),
        name="combine",
    )(dest, yb, wts, x2, gate_f.reshape(1, d))


def _moe(h, logits, x2, gate_f, layer, w_gate, w_up, w_down):
    s = h.shape[0]
    ids, wts, cnt = _route(logits)
    counts = cnt[0, ROUTE_E0:ROUTE_E0 + N_EXPERTS].astype(jnp.int32)
    padded = (counts + EXPERT_BLOCK - 1) // EXPERT_BLOCK * EXPERT_BLOCK
    pend = jnp.cumsum(padded)
    pstart = pend - padded
    is_expert = ids[:, 0:MOE_TOP_K, None] == jnp.arange(N_EXPERTS, dtype=jnp.int32)
    slot0 = jnp.sum(jnp.where(is_expert, pstart, 0), axis=-1)
    dest = (slot0 + ids[:, MOE_TOP_K:2 * MOE_TOP_K]).reshape(s * MOE_TOP_K)
    nb = (s * MOE_TOP_K) // EXPERT_BLOCK + N_EXPERTS
    block_start = jnp.arange(nb, dtype=jnp.int32) * EXPERT_BLOCK
    block_expert = jnp.sum((pend[None, :] <= block_start[:, None]).astype(jnp.int32), axis=1)
    block_expert = jnp.minimum(block_expert, N_EXPERTS - 1)
    n_used = (pend[-1:] // EXPERT_BLOCK).astype(jnp.int32)
    xb = _dispatch(dest, h, nb * EXPERT_BLOCK)
    yb = _experts(block_expert, n_used, xb, layer, w_gate, w_up, w_down)
    return _combine(dest, yb, wts, x2, gate_f)


def kernel(x, c, w_ada, b_ada, norm_mix, norm_ffn, w_in, ssm_conv_w, ssm_conv_b, ssm_dt_bias, ssm_a_log, ssm_d,
           ssm_norm, sb_q_norm, sb_k_norm, gdn_conv_w, gdn_a_log, gdn_dt_bias, gdn_norm, w_branch, w_out,
           w_group, b_group, w_router, b_router, w_gate, w_up, w_down):
    bsz, s, d = x.shape
    assert bsz == 1 and d == D_MODEL
    depth = w_in.shape[0]
    mod = _adaln_mod(c, w_ada, b_ada)
    x2 = x.reshape(s, d)
    for l in range(depth):
        shift_m, scale_m, gate_m, shift_f, scale_f, gate_f = jnp.split(mod[l], 6)
        proj, narrow = _inproj(x2, norm_mix[l], scale_m, shift_m, *_layout_w_in(w_in[l]), sb_q_norm[l], sb_k_norm[l])
        dtt = narrow[:, NCOL_DT:NCOL_DT + 8].T
        gabt = narrow[:, NCOL_GAB:NCOL_GAB + 8].T
        ya = _ssd(proj, narrow, dtt, ssm_conv_w[l], ssm_conv_b[l], ssm_dt_bias[l], ssm_a_log[l], ssm_d[l], ssm_norm[l])
        yb = _stick_breaking(proj)
        yc = _gdn(proj, narrow, gabt, gdn_conv_w[l], gdn_a_log[l], gdn_dt_bias[l], gdn_norm[l])
        pad = jnp.zeros((d, LANES - MOE_GROUPS - N_EXPERTS), F32)
        w_rt = jnp.concatenate([w_group[l], w_router[l], pad], axis=1)
        b_rt = jnp.concatenate([b_group[l], b_router[l], pad[0]])
        x2, h, logits = _merge(ya, yb, yc, proj, x2, _bf(w_branch[l]), _bf(w_out[l]), gate_m,
                               norm_ffn[l], scale_f, shift_f, w_rt, b_rt)
        x2 = _moe(h, logits, x2, gate_f, l, w_gate, w_up, w_down)
    return x2.reshape(bsz, s, d)
```

```python
import functools

import jax
import jax.numpy as jnp
from jax import lax
from jax.experimental import pallas as pl
from jax.experimental.pallas import tpu as pltpu

F32 = jnp.float32
BF16 = jnp.bfloat16
EPS = 1e-6

D_MODEL = 1024
SSM_HEADS = 8
SSM_HEAD_DIM = 64
SSM_INNER = 512
SSM_GROUPS = 2
SSM_STATE = 128
SSM_XBC = 1024
SSD_CHUNK = 128
SB_HEADS = 4
SB_HEAD_DIM = 128
SB_BLOCK = 128
GDN_HEADS = 4
GDN_HEAD_DIM = 128
GDN_CHUNK = 64
MOE_GROUPS = 4
EXPERTS_PER_GROUP = 8
N_EXPERTS = 32
MOE_TOP_K = 2
EXPERT_FF = 512

LANES = 128
COL_BR = 0
COL_SB = 3072
COL_GQKV = 4608
COL_XBC = 6144
COL_Z = 7168
COL_GGATE = 7680
WIDE_COLS = 8192
NCOL_DT = 0
NCOL_GAB = 128
NARROW_COLS = 256


def _col_spec(tb, width, col):
    assert col % width == 0
    return pl.BlockSpec((tb, width), lambda i: (i, col // width))

VMEM_LIMIT = 48 * 1024 * 1024
SB_SKIP_LOG = -110.0


def _bf(x):
    return x.astype(BF16)


def _dot(a, b):
    return jnp.dot(a, b, preferred_element_type=F32)


def _dot_nt(a, b):
    return lax.dot_general(a, b, (((1,), (1,)), ((), ())), preferred_element_type=F32)


def _dot_tn(a, b):
    return lax.dot_general(a, b, (((0,), (0,)), ((), ())), preferred_element_type=F32)


def _dot_hp(a, b):
    return jnp.dot(a, b, preferred_element_type=F32, precision=lax.Precision.HIGHEST)


def _split2(x):
    hi = _bf(x)
    lo = _bf(x - hi.astype(F32))
    return hi, lo


def _dot3(a, b):
    ah, al = _split2(a)
    bh, bl = _split2(b)
    return _dot(ah, bh) + (_dot(ah, bl) + _dot(al, bh))


def _silu(x):
    return x * jax.nn.sigmoid(x)


def _softplus(x):
    return jnp.maximum(x, 0.0) + jnp.log1p(jnp.exp(-jnp.abs(x)))


def _iota2(shape, dim):
    return lax.broadcasted_iota(jnp.int32, shape, dim)


def _params(*sem):
    return pltpu.CompilerParams(dimension_semantics=sem, vmem_limit_bytes=VMEM_LIMIT)


def _mod_kernel(c_ref, w_ref, b_ref, o_ref):
    c = _silu(c_ref[...])
    o_ref[0] = _dot3(c, w_ref[0]) + b_ref[0]


def _adaln_mod(c, w_ada, b_ada):
    depth, d, cols = w_ada.shape
    tn = 1024
    c8 = jnp.broadcast_to(c, (8, d))
    out = pl.pallas_call(
        _mod_kernel,
        grid=(depth, cols // tn),
        in_specs=[
            pl.BlockSpec((8, d), lambda l, j: (0, 0)),
            pl.BlockSpec((1, d, tn), lambda l, j: (l, 0, j)),
            pl.BlockSpec((1, 1, tn), lambda l, j: (l, 0, j)),
        ],
        out_specs=pl.BlockSpec((1, 8, tn), lambda l, j: (l, 0, j)),
        out_shape=jax.ShapeDtypeStruct((depth, 8, cols), F32),
        compiler_params=_params("arbitrary", "arbitrary"),
        name="adaln_mod",
    )(c8, w_ada, b_ada.reshape(depth, 1, cols))
    return out[:, 0, :]


def _norm_mod(x, g, scale, shift):
    y = x * lax.rsqrt(jnp.mean(x * x, axis=-1, keepdims=True) + EPS)
    return (y * g) * (1.0 + scale) + shift


INPROJ_TN = 1024
assert COL_SB % INPROJ_TN == 0 and 2 * SB_HEADS * SB_HEAD_DIM == INPROJ_TN


def _inproj_kernel(x_ref, g_ref, sc_ref, sh_ref, w_ref, wn_ref, qkg_ref, qks_ref, o_ref, on_ref, h_scr):
    j = pl.program_id(1)

    @pl.when(j == 0)
    def _():
        h = _bf(_norm_mod(x_ref[...], g_ref[...], sc_ref[...], sh_ref[...]))
        h_scr[...] = h
        on_ref[...] = _dot(h, wn_ref[...])

    @pl.when(j != COL_SB // INPROJ_TN)
    def _():
        o_ref[...] = _bf(_dot(h_scr[...], w_ref[...]))

    @pl.when(j == COL_SB // INPROJ_TN)
    def _():
        acc = _dot(h_scr[...], w_ref[...])
        dh = SB_HEAD_DIM
        for n in range(INPROJ_TN // dh):
            cols = slice(n * dh, (n + 1) * dh)
            o_ref[:, cols] = _bf(_head_rms(acc[:, cols], qkg_ref[:, cols]) * qks_ref[:, cols])


def _inproj(x2, g, scale, shift, w_wide, w_narrow, q_g, k_g):
    s, d = x2.shape
    tm = min(1024, s)
    tn = INPROJ_TN
    row = lambda a: a.reshape(1, d)
    vec = pl.BlockSpec((1, d), lambda i, j: (0, 0))
    qk_gain = jnp.concatenate([jnp.tile(q_g, SB_HEADS), jnp.tile(k_g, SB_HEADS)])
    qk_scale = jnp.concatenate([jnp.full((tn // 2,), SB_HEAD_DIM ** -0.5, F32), jnp.ones((tn // 2,), F32)])
    return pl.pallas_call(
        _inproj_kernel,
        grid=(s // tm, WIDE_COLS // tn),
        in_specs=[pl.BlockSpec((tm, d), lambda i, j: (i, 0)), vec, vec, vec,
                  pl.BlockSpec((d, tn), lambda i, j: (0, j)),
                  pl.BlockSpec((d, NARROW_COLS), lambda i, j: (0, 0)),
                  pl.BlockSpec((1, tn), lambda i, j: (0, 0)), pl.BlockSpec((1, tn), lambda i, j: (0, 0))],
        out_specs=[pl.BlockSpec((tm, tn), lambda i, j: (i, j)),
                   pl.BlockSpec((tm, NARROW_COLS), lambda i, j: (i, 0))],
        out_shape=[jax.ShapeDtypeStruct((s, WIDE_COLS), BF16), jax.ShapeDtypeStruct((s, NARROW_COLS), F32)],
        scratch_shapes=[pltpu.VMEM((tm, d), BF16)],
        compiler_params=_params("arbitrary", "arbitrary"),
        name="inproj",
    )(x2, row(g), row(scale), row(shift), w_wide, w_narrow, qk_gain.reshape(1, tn), qk_scale.reshape(1, tn))


def _causal_conv4(x, ext_scr, w_ref):
    tb = x.shape[0]
    ext_scr[8:8 + tb, :] = x
    y = x * w_ref[3:4, :]
    for k in (1, 2, 3):
        y = y + ext_scr[8 - k:8 - k + tb, :] * w_ref[3 - k:4 - k, :]
    ext_scr[0:8, :] = x[tb - 8:tb]
    return y


def _ssd_kernel(z_ref, xbc_ref, dt_ref, dtt_ref, cw_ref, cb_ref, dtb_ref, dtbt_ref, al_ref, alt_ref,
                dsk_ref, ng_ref, o_ref, ext_scr, act_scr, y_scr, st_scr):
    tb = xbc_ref.shape[0]
    L = SSD_CHUNK
    P = SSM_HEAD_DIM

    @pl.when(pl.program_id(0) == 0)
    def _():
        ext_scr[0:8, :] = jnp.zeros((8, ext_scr.shape[1]), F32)
        st_scr[...] = jnp.zeros_like(st_scr)

    act_scr[...] = _silu(_causal_conv4(xbc_ref[...].astype(F32), ext_scr, cw_ref) + cb_ref[...])

    ri = _iota2((L, L), 0)
    ci = _iota2((L, L), 1)
    tril = (ri >= ci).astype(F32)
    causal = ri >= ci
    a_col = -jnp.exp(al_ref[...])
    a_row = -jnp.exp(alt_ref[...])
    expand = (_iota2((LANES, SSM_INNER), 1) // P == _iota2((LANES, SSM_INNER), 0)).astype(F32)
    hpg = SSM_HEADS // SSM_GROUPS
    gw = hpg * P

    for c in range(tb // L):
        rows = slice(c * L, (c + 1) * L)
        dt = _softplus(dt_ref[rows, :] + dtb_ref[...])
        dtt = _softplus(dtt_ref[:, rows] + dtbt_ref[...])
        acum = _dot_hp(tril, dt * a_col)
        acum_t = _dot_hp(dtt * a_row, tril.T)
        per_head = jnp.concatenate([dt, jnp.exp(acum[L - 1:L, :] - acum), jnp.exp(acum)], axis=0)
        per_chan = _dot_hp(per_head, expand)
        dt_e, to_end_e, from_start_e = per_chan[:L], per_chan[L:2 * L], per_chan[2 * L:]
        act = act_scr[rows, :]
        xs = act[:, :SSM_INNER]
        xdt = xs * dt_e
        xdt_b = _bf(xdt)
        xdt_end_b = _bf(xdt * to_end_e)
        off_c = SSM_INNER + SSM_GROUPS * SSM_STATE
        y_in, y_st = [], []
        for g in range(SSM_GROUPS):
            bm = _bf(act[:, SSM_INNER + g * SSM_STATE:SSM_INNER + (g + 1) * SSM_STATE])
            cm = _bf(act[:, off_c + g * SSM_STATE:off_c + (g + 1) * SSM_STATE])
            gmat = _dot_nt(cm, bm)
            st = st_scr[g]
            y_st.append(_dot_nt(cm, _bf(st)))
            new = _dot_tn(xdt_end_b[:, g * gw:(g + 1) * gw], bm)
            for hh in range(hpg):
                h = g * hpg + hh
                seg = jnp.exp(jnp.where(causal, acum[:, h:h + 1] - acum_t[h:h + 1, :], -jnp.inf))
                y_in.append(_dot(_bf(gmat * seg), xdt_b[:, h * P:(h + 1) * P]))
                sl = slice(hh * P, (hh + 1) * P)
                st_scr[g, sl, :] = st[sl] * jnp.exp(acum_t[h:h + 1, L - 1:L]) + new[sl]
        y_scr[rows, :] = (jnp.concatenate(y_in, axis=-1) + jnp.concatenate(y_st, axis=-1) * from_start_e
                          + xs * dsk_ref[...])

    y = y_scr[...] * _silu(z_ref[...].astype(F32))
    gsz = SSM_INNER // SSM_GROUPS
    outs = []
    for g in range(SSM_GROUPS):
        yg = y[:, g * gsz:(g + 1) * gsz]
        yn = yg * lax.rsqrt(jnp.mean(yg * yg, axis=-1, keepdims=True) + EPS)
        outs.append(yn * ng_ref[:, g * gsz:(g + 1) * gsz])
    o_ref[...] = jnp.concatenate(outs, axis=-1)


def _ssd(proj, narrow, dtt, conv_w, conv_b, dt_bias, a_log, d_skip, norm_g):
    s = proj.shape[0]
    tb = min(256, s)
    pad8 = lambda v: jnp.pad(v, (0, LANES - v.shape[0])).reshape(1, LANES)
    colv = lambda v: v.reshape(SSM_HEADS, 1)
    full = lambda shape: pl.BlockSpec(shape, lambda i: (0,) * len(shape))
    return pl.pallas_call(
        _ssd_kernel,
        grid=(s // tb,),
        in_specs=[
            _col_spec(tb, SSM_INNER, COL_Z),
            _col_spec(tb, SSM_XBC, COL_XBC),
            _col_spec(tb, LANES, NCOL_DT),
            pl.BlockSpec((SSM_HEADS, tb), lambda i: (0, i)),
            full((4, SSM_XBC)), full((1, SSM_XBC)), full((1, LANES)), full((SSM_HEADS, 1)),
            full((1, LANES)), full((SSM_HEADS, 1)), full((1, SSM_INNER)), full((1, SSM_INNER)),
        ],
        out_specs=pl.BlockSpec((tb, SSM_INNER), lambda i: (i, 0)),
        out_shape=jax.ShapeDtypeStruct((s, SSM_INNER), F32),
        scratch_shapes=[
            pltpu.VMEM((tb + 8, SSM_XBC), F32),
            pltpu.VMEM((tb, SSM_XBC), F32),
            pltpu.VMEM((tb, SSM_INNER), F32),
            pltpu.VMEM((SSM_GROUPS, SSM_HEADS // SSM_GROUPS * SSM_HEAD_DIM, SSM_STATE), F32),
        ],
        compiler_params=_params("arbitrary"),
        name="ssd",
    )(proj, proj, narrow, dtt, conv_w, conv_b.reshape(1, -1), pad8(dt_bias), colv(dt_bias),
      pad8(a_log), colv(a_log), jnp.repeat(d_skip, SSM_HEAD_DIM).reshape(1, -1), norm_g.reshape(1, -1))


def _layout_w_in(w):
    d = w.shape[0]
    z, xbc, dt, sb, gqkv, gab, ggate, br = jnp.split(w, [512, 1536, 1544, 3080, 4616, 4624, 5136], axis=1)
    pad = jnp.zeros((d, LANES - 8), w.dtype)
    wide = jnp.concatenate([br, sb, gqkv, xbc, z, ggate], axis=1).astype(BF16)
    narrow = jnp.concatenate([dt, pad, gab, pad], axis=1).astype(BF16)
    return wide, narrow


def _head_rms(x, g):
    return (x * lax.rsqrt(jnp.mean(x * x, axis=-1, keepdims=True) + EPS)) * g


def _sb_blocks(qs, kns, vs, accs, suffix, masked):
    blk = qs[0].shape[0]
    strict = _iota2((blk, blk), 1) < _iota2((blk, blk), 0)
    zs = [_dot_nt(q, kn) for q, kn in zip(qs, kns)]
    sps = [jnp.maximum(z, 0.0) + jnp.log1p(jnp.exp(-jnp.abs(z))) for z in zs]
    log_keeps = [jnp.where(strict, -sp, 0.0) if masked else -sp for sp in sps]
    splits = [_split2(lk) for lk in log_keeps]
    afters = [(_dot(hi, suffix) + _dot(lo, suffix)) + acc for (hi, lo), acc in zip(splits, accs)]
    atts = [jnp.exp((z - sp) + after) for z, sp, after in zip(zs, sps, afters)]
    if masked:
        atts = [jnp.where(strict, att, 0.0) for att in atts]
    outs = [_dot(_bf(att), v) for att, v in zip(atts, vs)]
    return outs, [jnp.sum(lk, axis=-1, keepdims=True) for lk in log_keeps]


SB_QBLOCKS = 2


def _sb_kernel(q_ref, k_ref, v_ref, o_ref, acc_scr):
    blk = SB_BLOCK
    dh = SB_HEAD_DIM
    first = pl.program_id(0) * SB_QBLOCKS
    pairs = [(b, h) for b in range(SB_QBLOCKS) for h in range(SB_HEADS)]
    suffix = (_iota2((blk, blk), 0) > _iota2((blk, blk), 1)).astype(BF16)
    qs = [q_ref[b * blk:(b + 1) * blk, h * dh:(h + 1) * dh] for b, h in pairs]

    def load_kv(offset):
        rows = [pl.ds(pl.multiple_of(jnp.maximum(first + b - offset, 0) * blk, blk), blk) for b in range(SB_QBLOCKS)]
        return ([k_ref[rows[b], h * dh:(h + 1) * dh] for b, h in pairs],
                [v_ref[rows[b], h * dh:(h + 1) * dh] for b, h in pairs])

    def live(accs):
        top = functools.reduce(jnp.maximum, accs)
        return (jnp.max(top) > SB_SKIP_LOG).astype(jnp.int32)

    def out_slice(n):
        b, h = pairs[n]
        return (slice(b * blk, (b + 1) * blk), slice(h * dh, (h + 1) * dh))

    kns, vs = load_kv(0)
    outs, sums = _sb_blocks(qs, kns, vs, [jnp.zeros((blk, 1), F32)] * len(pairs), suffix, True)
    for n in range(len(pairs)):
        o_ref[out_slice(n)] = outs[n]
        acc_scr[n] = sums[n]

    def cond(carry):
        offset, alive = carry
        return jnp.logical_and(offset <= first + SB_QBLOCKS - 1, alive > 0)

    def body(carry):
        offset, _ = carry
        kns, vs = load_kv(offset)
        accs = [acc_scr[n] for n in range(len(pairs))]
        outs, sums = _sb_blocks(qs, kns, vs, accs, suffix, False)
        valid = [jnp.where(first + b - offset >= 0, 1.0, 0.0) for b in range(SB_QBLOCKS)]
        accs = [acc + rs * valid[b] for acc, rs, (b, _) in zip(accs, sums, pairs)]
        for n in range(len(pairs)):
            o_ref[out_slice(n)] += outs[n] * valid[pairs[n][0]]
            acc_scr[n] = accs[n]
        return offset + 1, live(accs)

    lax.while_loop(cond, body, (jnp.int32(1), live(sums)))


def _stick_breaking(proj):
    s = proj.shape[0]
    tq = SB_QBLOCKS * SB_BLOCK
    dh = SB_HEAD_DIM
    width = SB_HEADS * dh
    resident = lambda col: pl.BlockSpec((s, width), lambda i: (0, col // width), pipeline_mode=pl.Buffered(1))
    return pl.pallas_call(
        _sb_kernel,
        grid=(s // tq,),
        in_specs=[_col_spec(tq, width, COL_SB), resident(COL_SB + width), resident(COL_SB + 2 * width)],
        out_specs=pl.BlockSpec((tq, width), lambda i: (i, 0)),
        out_shape=jax.ShapeDtypeStruct((s, width), F32),
        scratch_shapes=[pltpu.VMEM((SB_QBLOCKS * SB_HEADS, SB_BLOCK, 1), F32)],
        compiler_params=_params("arbitrary"),
        name="stick_breaking",
    )(proj, proj, proj)


GDN_SUB = 128


def _dot3_nt(a, b):
    ah, al = _split2(a)
    bh, bl = _split2(b)
    return _dot_nt(ah, bh) + (_dot_nt(ah, bl) + _dot_nt(al, bh))


def _chunk_lower_inverses(ms, chunk):
    n = ms[0].shape[0]
    eye = (_iota2((n, n), 0) == _iota2((n, n), 1)).astype(F32)
    ps = [-m for m in ms]
    invs = [eye + p for p in ps]
    p_parts = [_split2(p) for p in ps]
    for j in range((chunk - 1).bit_length() - 1):
        if j == 0:
            ps = [_dot(ph, ph) + (_dot(ph, pl_) + _dot(pl_, ph)) for ph, pl_ in p_parts]
        else:
            ps = [_dot(ph, ph) for ph, _ in p_parts]
        p_parts = [_split2(p) for p in ps]
        inv_parts = [_split2(inv) for inv in invs]
        invs = [inv + (_dot(ih, ph) + (_dot(ih, pl_) + _dot(il, ph)))
                for inv, (ih, il), (ph, pl_) in zip(invs, inv_parts, p_parts)]
    return invs


def _gdn_kernel(qkv_ref, gab_ref, gabt_ref, gate_ref, cw_ref, al_ref, alt_ref, dtb_ref, dtbt_ref, ng_ref,
                o_ref, ext_scr, act_scr, st_scr):
    tb = qkv_ref.shape[0]
    C = GDN_CHUNK
    dh = GDN_HEAD_DIM
    inner = GDN_HEADS * dh
    heads = range(GDN_HEADS)

    @pl.when(pl.program_id(0) == 0)
    def _():
        ext_scr[0:8, :] = jnp.zeros((8, ext_scr.shape[1]), F32)
        st_scr[...] = jnp.zeros_like(st_scr)

    act_scr[...] = _silu(_causal_conv4(qkv_ref[...].astype(F32), ext_scr, cw_ref))

    sub = min(GDN_SUB, tb)
    cps = sub // C
    ri = _iota2((sub, sub), 0)
    ci = _iota2((sub, sub), 1)
    same = (ri // C) == (ci // C)
    incl = jnp.logical_and(same, ri >= ci)
    strict = jnp.logical_and(same, ri > ci)
    tril = incl.astype(F32)
    al_col = -jnp.exp(al_ref[...])
    al_row = -jnp.exp(alt_ref[...])
    units = [(b, h) for b in range(tb // sub) for h in heads]

    gc_cols, gc_rows, betas = [], [], []
    for b in range(tb // sub):
        rows = slice(b * sub, (b + 1) * sub)
        gab = gab_ref[rows, :]
        g_col = al_col * _softplus(gab + dtb_ref[...])
        g_row = al_row * _softplus(gabt_ref[:, rows] + dtbt_ref[...])
        gc_cols.append(_dot_hp(tril, g_col))
        gc_rows.append(_dot_hp(g_row, tril.T))
        betas.append(jax.nn.sigmoid(gab))

    qs, ks, kbs, gcs, decays, rhss = [], [], [], [], [], []
    for b, h in units:
        rows = slice(b * sub, (b + 1) * sub)
        q = act_scr[rows, h * dh:(h + 1) * dh]
        k = act_scr[rows, inner + h * dh:inner + (h + 1) * dh]
        v = act_scr[rows, 2 * inner + h * dh:2 * inner + (h + 1) * dh]
        q = q * lax.rsqrt(jnp.sum(q * q, axis=-1, keepdims=True) + EPS) * (dh ** -0.5)
        k = k * lax.rsqrt(jnp.sum(k * k, axis=-1, keepdims=True) + EPS)
        beta = betas[b][:, GDN_HEADS + h:GDN_HEADS + h + 1]
        gc = gc_cols[b][:, h:h + 1]
        kb = k * beta
        qs.append(q)
        ks.append(k)
        kbs.append(kb)
        gcs.append(gc)
        decays.append(jnp.exp(jnp.where(incl, gc - gc_rows[b][h:h + 1, :], -jnp.inf)))
        rhss.append(jnp.concatenate([v * beta, kb * jnp.exp(gc)], axis=-1))

    n_units = range(len(units))
    ms = [jnp.where(strict, _dot3_nt(kbs[n], ks[n]) * decays[n], 0.0) for n in n_units]
    invs = _chunk_lower_inverses(ms, C)
    sols = [_dot3(invs[n], rhss[n]) for n in n_units]
    attns = [_bf(jnp.where(incl, _dot_nt(_bf(qs[n]), _bf(ks[n])) * decays[n], 0.0)) for n in n_units]
    q_decs = [_bf(qs[n] * jnp.exp(gcs[n])) for n in n_units]

    sts = [st_scr[h] for h in heads]
    v_news = [[] for _ in n_units]
    o_inters = [[] for _ in n_units]
    for c in range(tb // C):
        b = c // cps
        rows = slice((c % cps) * C, (c % cps + 1) * C)
        for h in heads:
            n = b * GDN_HEADS + h
            st_b = _bf(sts[h])
            g_last = gc_rows[b][h:h + 1, rows.stop - 1:rows.stop]
            v_new = sols[n][rows, :dh] - _dot(_bf(sols[n][rows, dh:]), st_b)
            v_new_b = _bf(v_new)
            o_inters[n].append(_dot(q_decs[n][rows, :], st_b))
            k_dec = ks[n][rows, :] * jnp.exp(g_last - gcs[n][rows, :])
            sts[h] = sts[h] * jnp.exp(g_last) + _dot_tn(_bf(k_dec), v_new_b)
            v_news[n].append(v_new_b)
    for h in heads:
        st_scr[h] = sts[h]
    for n, (b, h) in enumerate(units):
        rows = slice(b * sub, (b + 1) * sub)
        o = jnp.concatenate(o_inters[n], axis=0) + _dot(attns[n], jnp.concatenate(v_news[n], axis=0))
        o = _head_rms(o, ng_ref[...]) * _silu(gate_ref[rows, h * dh:(h + 1) * dh].astype(F32))
        o_ref[rows, h * dh:(h + 1) * dh] = o


def _gdn(proj, narrow, gabt, conv_w, a_log, dt_bias, norm_g):
    s = proj.shape[0]
    tb = min(256, s)
    inner = GDN_HEADS * GDN_HEAD_DIM
    pad_lane = lambda v: jnp.pad(v, (0, LANES - v.shape[0])).reshape(1, LANES)
    pad_col = lambda v: jnp.pad(v, (0, 8 - v.shape[0])).reshape(8, 1)
    full = lambda shape: pl.BlockSpec(shape, lambda i: (0,) * len(shape))
    return pl.pallas_call(
        _gdn_kernel,
        grid=(s // tb,),
        in_specs=[
            _col_spec(tb, 3 * inner, COL_GQKV),
            _col_spec(tb, LANES, NCOL_GAB),
            pl.BlockSpec((8, tb), lambda i: (0, i)),
            _col_spec(tb, inner, COL_GGATE),
            full((4, 3 * inner)), full((1, LANES)), full((8, 1)), full((1, LANES)), full((8, 1)),
            full((1, GDN_HEAD_DIM)),
        ],
        out_specs=pl.BlockSpec((tb, inner), lambda i: (i, 0)),
        out_shape=jax.ShapeDtypeStruct((s, inner), F32),
        scratch_shapes=[
            pltpu.VMEM((tb + 8, 3 * inner), F32),
            pltpu.VMEM((tb, 3 * inner), F32),
            pltpu.VMEM((GDN_HEADS, GDN_HEAD_DIM, GDN_HEAD_DIM), F32),
        ],
        compiler_params=_params("arbitrary"),
        name="gdn",
    )(proj, narrow, gabt, proj, conv_w, pad_lane(a_log), pad_col(a_log), pad_lane(dt_bias), pad_col(dt_bias),
      norm_g.reshape(1, -1))


def _merge_kernel(ya_ref, yb_ref, yc_ref, br_ref, x_ref, wbr_ref, wout_ref, gm_ref, g_ref, sc_ref, sh_ref,
                  wrt_ref, brt_ref, xo_ref, h_ref, lg_ref):
    d = x_ref.shape[1]
    merged = None
    for i, y_ref in enumerate((ya_ref, yb_ref, yc_ref)):
        gate = jax.nn.sigmoid(br_ref[:, i * d:(i + 1) * d].astype(F32))
        term = gate * _dot(_bf(y_ref[...]), wbr_ref[i])
        merged = term if merged is None else merged + term
    x_new = x_ref[...] + gm_ref[...] * _dot(_bf(merged), wout_ref[...])
    xo_ref[...] = x_new
    h = _norm_mod(x_new, g_ref[...], sc_ref[...], sh_ref[...])
    h_ref[...] = h
    lg_ref[...] = _dot3(h, wrt_ref[...]) + brt_ref[...]


def _merge(ya, yb, yc, proj, x2, wbr_bf, wout_bf, gate_m, g, scale, shift, w_rt, b_rt):
    s, d = x2.shape
    tb = min(512, s)
    bw = ya.shape[1]
    row = lambda a: a.reshape(1, -1)
    vec = pl.BlockSpec((1, d), lambda i: (0, 0))
    blk = lambda w: pl.BlockSpec((tb, w), lambda i: (i, 0))
    return pl.pallas_call(
        _merge_kernel,
        grid=(s // tb,),
        in_specs=[blk(bw), blk(bw), blk(bw), _col_spec(tb, 3 * d, COL_BR), blk(d),
                  pl.BlockSpec((3, bw, d), lambda i: (0, 0, 0)), pl.BlockSpec((d, d), lambda i: (0, 0)),
                  vec, vec, vec, vec,
                  pl.BlockSpec((d, LANES), lambda i: (0, 0)), pl.BlockSpec((1, LANES), lambda i: (0, 0))],
        out_specs=[blk(d), blk(d), blk(LANES)],
        out_shape=[jax.ShapeDtypeStruct((s, d), F32), jax.ShapeDtypeStruct((s, d), F32),
                   jax.ShapeDtypeStruct((s, LANES), F32)],
        compiler_params=_params("arbitrary"),
        name="merge",
    )(ya, yb, yc, proj, x2, wbr_bf, wout_bf, row(gate_m), row(g), row(scale), row(shift), w_rt, row(b_rt))


ROUTE_E0 = MOE_GROUPS


def _route_kernel(lg_ref, ids_ref, wts_ref, cnt_ref, carry_scr):
    tb = lg_ref.shape[0]

    @pl.when(pl.program_id(0) == 0)
    def _():
        carry_scr[...] = jnp.zeros_like(carry_scr)

    lg = lg_ref[...]
    lane = _iota2((tb, LANES), 1)
    big = jnp.int32(LANES)
    neg = -jnp.inf
    gl = jnp.where(lane < MOE_GROUPS, lg, neg)
    gmax = jnp.max(gl, axis=-1, keepdims=True)
    g_sel = jnp.min(jnp.where(gl == gmax, lane, big), axis=-1, keepdims=True)
    p_group = 1.0 / jnp.sum(jnp.exp(gl - gmax), axis=-1, keepdims=True)
    lo = ROUTE_E0 + EXPERTS_PER_GROUP * g_sel
    el = jnp.where(jnp.logical_and(lane >= lo, lane < lo + EXPERTS_PER_GROUP), lg, neg)
    m1 = jnp.max(el, axis=-1, keepdims=True)
    i1 = jnp.min(jnp.where(el == m1, lane, big), axis=-1, keepdims=True)
    esum = jnp.sum(jnp.exp(el - m1), axis=-1, keepdims=True)
    el2 = jnp.where(lane == i1, neg, el)
    m2 = jnp.max(el2, axis=-1, keepdims=True)
    i2 = jnp.min(jnp.where(el2 == m2, lane, big), axis=-1, keepdims=True)
    p1 = 1.0 / esum
    p2 = jnp.exp(m2 - m1) / esum
    w1 = p_group * p1 / (p1 + p2)
    w2 = p_group * p2 / (p1 + p2)

    sel1 = lane == i1
    sel2 = lane == i2
    onehot = jnp.where(jnp.logical_or(sel1, sel2), 1.0, 0.0)
    before = (_iota2((tb, tb), 0) > _iota2((tb, tb), 1)).astype(BF16)
    seen = _dot(before, _bf(onehot)) + carry_scr[...]
    r1 = jnp.sum(jnp.where(sel1, seen, 0.0), axis=-1, keepdims=True)
    r2 = jnp.sum(jnp.where(sel2, seen, 0.0), axis=-1, keepdims=True)
    carry = carry_scr[...] + jnp.sum(onehot, axis=0, keepdims=True)
    carry_scr[...] = carry
    cnt_ref[...] = jnp.broadcast_to(carry, cnt_ref.shape)

    ids = jnp.where(lane == 0, i1 - ROUTE_E0, jnp.where(lane == 1, i2 - ROUTE_E0,
          jnp.where(lane == 2, r1.astype(jnp.int32), jnp.where(lane == 3, r2.astype(jnp.int32), 0))))
    ids_ref[...] = ids
    wts_ref[...] = jnp.where(lane == 0, w1, jnp.where(lane == 1, w2, 0.0))


def _route(logits):
    s = logits.shape[0]
    tb = min(256, s)
    blk = pl.BlockSpec((tb, LANES), lambda i: (i, 0))
    return pl.pallas_call(
        _route_kernel,
        grid=(s // tb,),
        in_specs=[blk],
        out_specs=[blk, blk, pl.BlockSpec((8, LANES), lambda i: (0, 0))],
        out_shape=[jax.ShapeDtypeStruct((s, LANES), jnp.int32), jax.ShapeDtypeStruct((s, LANES), F32),
                   jax.ShapeDtypeStruct((8, LANES), F32)],
        scratch_shapes=[pltpu.VMEM((1, LANES), F32)],
        compiler_params=_params("arbitrary"),
        name="route",
    )(logits)


EXPERT_BLOCK = 512
ROW_TB = 256


def _dispatch_kernel(dest_ref, pend_ref, padded_ref, nused_ref, h_ref, xb_ref, zero_scr, sem, zero_sem):
    tb = h_ref.shape[0]
    base = pl.program_id(0) * tb * MOE_TOP_K

    @pl.when(pl.program_id(0) == 0)
    def _():
        zero_scr[...] = jnp.zeros_like(zero_scr)

        def zero_block(start):
            return pltpu.make_async_copy(zero_scr, xb_ref.at[pl.ds(start, EXPERT_BLOCK), :], zero_sem)

        def for_each_zero_block(action):
            for e in range(N_EXPERTS):
                @pl.when(padded_ref[e] > 0)
                def _():
                    action(zero_block(pl.multiple_of(pend_ref[e] - EXPERT_BLOCK, EXPERT_BLOCK)))
            for b in range(xb_ref.shape[0] // EXPERT_BLOCK):
                @pl.when(b >= nused_ref[0])
                def _():
                    action(zero_block(b * EXPERT_BLOCK))

        for_each_zero_block(lambda copy: copy.start())
        for_each_zero_block(lambda copy: copy.wait())

    def row_copy(t, k, d):
        return pltpu.make_async_copy(h_ref.at[pl.ds(t, 1), :], xb_ref.at[pl.ds(d, 1), :], sem)

    def issue(t, carry):
        for k in range(MOE_TOP_K):
            row_copy(t, k, dest_ref[base + t * MOE_TOP_K + k]).start(priority=k % 2)
        return carry

    for t in range(tb):
        issue(t, 0)
    for k in range(MOE_TOP_K):
        pltpu.make_async_copy(h_ref, xb_ref.at[pl.ds(0, tb), :], sem).wait()


def _dispatch(dest, pend, padded, n_used, h, n_slots):
    s, d = h.shape
    tb = min(ROW_TB, s)
    return pl.pallas_call(
        _dispatch_kernel,
        grid_spec=pltpu.PrefetchScalarGridSpec(
            num_scalar_prefetch=4,
            grid=(s // tb,),
            in_specs=[pl.BlockSpec((tb, d), lambda i, *_: (i, 0))],
            out_specs=pl.BlockSpec(memory_space=pl.ANY),
            scratch_shapes=[pltpu.VMEM((EXPERT_BLOCK, d), F32), pltpu.SemaphoreType.DMA(()),
                            pltpu.SemaphoreType.DMA(())],
        ),
        out_shape=jax.ShapeDtypeStruct((n_slots, d), F32),
        compiler_params=_params("arbitrary"),
        name="dispatch",
    )(dest, pend, padded, n_used, h)


def _expert_kernel(be_ref, nused_ref, x_ref, wg_ref, wu_ref, wd_ref, o_ref, wg_b, wu_b, wd_b):
    b = pl.program_id(0)

    @pl.when(jnp.logical_or(b == 0, be_ref[b] != be_ref[jnp.maximum(b - 1, 0)]))
    def _():
        wg_b[...] = _bf(wg_ref[0, 0])
        wu_b[...] = _bf(wu_ref[0, 0])
        wd_b[...] = _bf(wd_ref[0, 0])

    @pl.when(b < nused_ref[0])
    def _():
        x = _bf(x_ref[...])
        hid = _silu(_dot(x, wg_b[...])) * _dot(x, wu_b[...])
        o_ref[...] = _dot(_bf(hid), wd_b[...])

    @pl.when(b >= nused_ref[0])
    def _():
        o_ref[...] = jnp.zeros_like(o_ref)


def _experts(block_expert, n_used, xb, layer, w_gate, w_up, w_down):
    n_slots, d = xb.shape
    ff = w_gate.shape[3]
    nb = n_slots // EXPERT_BLOCK
    return pl.pallas_call(
        _expert_kernel,
        grid_spec=pltpu.PrefetchScalarGridSpec(
            num_scalar_prefetch=2,
            grid=(nb,),
            in_specs=[
                pl.BlockSpec((EXPERT_BLOCK, d), lambda b, be, nu: (b, 0)),
                pl.BlockSpec((1, 1, d, ff), lambda b, be, nu: (layer, be[b], 0, 0)),
                pl.BlockSpec((1, 1, d, ff), lambda b, be, nu: (layer, be[b], 0, 0)),
                pl.BlockSpec((1, 1, ff, d), lambda b, be, nu: (layer, be[b], 0, 0)),
            ],
            out_specs=pl.BlockSpec((EXPERT_BLOCK, d), lambda b, be, nu: (b, 0)),
            scratch_shapes=[pltpu.VMEM((d, ff), BF16), pltpu.VMEM((d, ff), BF16), pltpu.VMEM((ff, d), BF16)],
        ),
        out_shape=jax.ShapeDtypeStruct((n_slots, d), F32),
        compiler_params=_params("arbitrary"),
        name="experts",
    )(block_expert, n_used, xb, w_gate, w_up, w_down)


def _combine_kernel(dest_ref, yb_ref, wts_ref, x_ref, gf_ref, o_ref, buf, sem):
    tb = x_ref.shape[0]
    base = pl.program_id(0) * tb * MOE_TOP_K

    def row_copy(t, k, d):
        return pltpu.make_async_copy(yb_ref.at[pl.ds(d, 1), :], buf.at[k, pl.ds(t, 1), :], sem)

    def issue(t, carry):
        for k in range(MOE_TOP_K):
            row_copy(t, k, dest_ref[base + t * MOE_TOP_K + k]).start(priority=k % 2)
        return carry

    for t in range(tb):
        issue(t, 0)
    for k in range(MOE_TOP_K):
        pltpu.make_async_copy(yb_ref.at[pl.ds(0, tb), :], buf.at[k], sem).wait()
    wts = wts_ref[...]
    y = wts[:, 0:1] * buf[0] + wts[:, 1:2] * buf[1]
    o_ref[...] = x_ref[...] + gf_ref[...] * y


def _combine(dest, yb, wts, x2, gate_f):
    s, d = x2.shape
    tb = min(ROW_TB, s)
    return pl.pallas_call(
        _combine_kernel,
        grid_spec=pltpu.PrefetchScalarGridSpec(
            num_scalar_prefetch=1,
            grid=(s // tb,),
            in_specs=[pl.BlockSpec(memory_space=pl.ANY),
                      pl.BlockSpec((tb, LANES), lambda i, dest: (i, 0)),
                      pl.BlockSpec((tb, d), lambda i, dest: (i, 0)),
                      pl.BlockSpec((1, d), lambda i, dest: (0, 0))],
            out_specs=pl.BlockSpec((tb, d), lambda i, dest: (i, 0)),
            scratch_shapes=[pltpu.VMEM((MOE_TOP_K, tb, d), F32), pltpu.SemaphoreType.DMA(())],
        ),
        out_shape=jax.ShapeDtypeStruct((s, d), F32),
        compiler_params=_params("arbitrary"),
        name="combine",
    )(dest, yb, wts, x2, gate_f.reshape(1, d))


def _moe(h, logits, x2, gate_f, layer, w_gate, w_up, w_down):
    s = h.shape[0]
    ids, wts, cnt = _route(logits)
    counts = cnt[0, ROUTE_E0:ROUTE_E0 + N_EXPERTS].astype(jnp.int32)
    padded = (counts + EXPERT_BLOCK - 1) // EXPERT_BLOCK * EXPERT_BLOCK
    pend = jnp.cumsum(padded)
    pstart = pend - padded
    is_expert = ids[:, 0:MOE_TOP_K, None] == jnp.arange(N_EXPERTS, dtype=jnp.int32)
    slot0 = jnp.sum(jnp.where(is_expert, pstart, 0), axis=-1)
    dest = (slot0 + ids[:, MOE_TOP_K:2 * MOE_TOP_K]).reshape(s * MOE_TOP_K)
    nb = (s * MOE_TOP_K) // EXPERT_BLOCK + N_EXPERTS
    block_start = jnp.arange(nb, dtype=jnp.int32) * EXPERT_BLOCK
    block_expert = jnp.sum((pend[None, :] <= block_start[:, None]).astype(jnp.int32), axis=1)
    block_expert = jnp.minimum(block_expert, N_EXPERTS - 1)
    n_used = (pend[-1:] // EXPERT_BLOCK).astype(jnp.int32)
    xb = _dispatch(dest, pend.astype(jnp.int32), padded.astype(jnp.int32), n_used, h, nb * EXPERT_BLOCK)
    yb = _experts(block_expert, n_used, xb, layer, w_gate, w_up, w_down)
    return _combine(dest, yb, wts, x2, gate_f)


def kernel(x, c, w_ada, b_ada, norm_mix, norm_ffn, w_in, ssm_conv_w, ssm_conv_b, ssm_dt_bias, ssm_a_log, ssm_d,
           ssm_norm, sb_q_norm, sb_k_norm, gdn_conv_w, gdn_a_log, gdn_dt_bias, gdn_norm, w_branch, w_out,
           w_group, b_group, w_router, b_router, w_gate, w_up, w_down):
    bsz, s, d = x.shape
    assert bsz == 1 and d == D_MODEL
    depth = w_in.shape[0]
    mod = _adaln_mod(c, w_ada, b_ada)
    x2 = x.reshape(s, d)
    for l in range(depth):
        shift_m, scale_m, gate_m, shift_f, scale_f, gate_f = jnp.split(mod[l], 6)
        proj, narrow = _inproj(x2, norm_mix[l], scale_m, shift_m, *_layout_w_in(w_in[l]), sb_q_norm[l], sb_k_norm[l])
        dtt = narrow[:, NCOL_DT:NCOL_DT + 8].T
        gabt = narrow[:, NCOL_GAB:NCOL_GAB + 8].T
        ya = _ssd(proj, narrow, dtt, ssm_conv_w[l], ssm_conv_b[l], ssm_dt_bias[l], ssm_a_log[l], ssm_d[l], ssm_norm[l])
        yb = _stick_breaking(proj)
        yc = _gdn(proj, narrow, gabt, gdn_conv_w[l], gdn_a_log[l], gdn_dt_bias[l], gdn_norm[l])
        pad = jnp.zeros((d, LANES - MOE_GROUPS - N_EXPERTS), F32)
        w_rt = jnp.concatenate([w_group[l], w_router[l], pad], axis=1)
        b_rt = jnp.concatenate([b_group[l], b_router[l], pad[0]])
        x2, h, logits = _merge(ya, yb, yc, proj, x2, _bf(w_branch[l]), _bf(w_out[l]), gate_m,
                               norm_ffn[l], scale_f, shift_f, w_rt, b_rt)
        x2 = _moe(h, logits, x2, gate_f, l, w_gate, w_up, w_down)
    return x2.reshape(bsz, s, d)
```

```python
import functools

import jax
import jax.numpy as jnp
from jax import lax
from jax.experimental import pallas as pl
from jax.experimental.pallas import tpu as pltpu

F32 = jnp.float32
BF16 = jnp.bfloat16
EPS = 1e-6

D_MODEL = 1024
SSM_HEADS = 8
SSM_HEAD_DIM = 64
SSM_INNER = 512
SSM_GROUPS = 2
SSM_STATE = 128
SSM_XBC = 1024
SSD_CHUNK = 128
SB_HEADS = 4
SB_HEAD_DIM = 128
SB_BLOCK = 128
GDN_HEADS = 4
GDN_HEAD_DIM = 128
GDN_CHUNK = 64
MOE_GROUPS = 4
EXPERTS_PER_GROUP = 8
N_EXPERTS = 32
MOE_TOP_K = 2
EXPERT_FF = 512

LANES = 128
COL_BR = 0
COL_SB = 3072
COL_GQKV = 4608
COL_XBC = 6144
COL_Z = 7168
COL_GGATE = 7680
WIDE_COLS = 8192
NCOL_DT = 0
NCOL_GAB = 128
NARROW_COLS = 256


def _col_spec(tb, width, col):
    assert col % width == 0
    return pl.BlockSpec((tb, width), lambda i: (i, col // width))

VMEM_LIMIT = 48 * 1024 * 1024
SB_SKIP_LOG = -110.0


def _bf(x):
    return x.astype(BF16)


def _dot(a, b):
    return jnp.dot(a, b, preferred_element_type=F32)


def _dot_nt(a, b):
    return lax.dot_general(a, b, (((1,), (1,)), ((), ())), preferred_element_type=F32)


def _dot_tn(a, b):
    return lax.dot_general(a, b, (((0,), (0,)), ((), ())), preferred_element_type=F32)


def _split3(x):
    hi = _bf(x)
    r = x - hi.astype(F32)
    mid = _bf(r)
    return hi, mid, _bf(r - mid.astype(F32))


def _dot_sel(sel, x):
    sel_b = _bf(sel)
    hi, mid, lo = _split3(x)
    return _dot(sel_b, hi) + (_dot(sel_b, mid) + _dot(sel_b, lo))


def _dot_sel_r(x, sel):
    sel_b = _bf(sel)
    hi, mid, lo = _split3(x)
    return _dot(hi, sel_b) + (_dot(mid, sel_b) + _dot(lo, sel_b))


def _split2(x):
    hi = _bf(x)
    lo = _bf(x - hi.astype(F32))
    return hi, lo


def _dot3(a, b):
    ah, al = _split2(a)
    bh, bl = _split2(b)
    return _dot(ah, bh) + (_dot(ah, bl) + _dot(al, bh))


def _silu(x):
    return x * jax.nn.sigmoid(x)


def _softplus(x):
    return jnp.maximum(x, 0.0) + jnp.log1p(jnp.exp(-jnp.abs(x)))


def _iota2(shape, dim):
    return lax.broadcasted_iota(jnp.int32, shape, dim)


def _params(*sem):
    return pltpu.CompilerParams(dimension_semantics=sem, vmem_limit_bytes=VMEM_LIMIT)


def _mod_kernel(c_ref, w_ref, b_ref, o_ref):
    c = _silu(c_ref[...])
    o_ref[0] = _dot3(c, w_ref[0]) + b_ref[0]


def _adaln_mod(c, w_ada, b_ada):
    depth, d, cols = w_ada.shape
    tn = 1024
    c8 = jnp.broadcast_to(c, (8, d))
    out = pl.pallas_call(
        _mod_kernel,
        grid=(depth, cols // tn),
        in_specs=[
            pl.BlockSpec((8, d), lambda l, j: (0, 0)),
            pl.BlockSpec((1, d, tn), lambda l, j: (l, 0, j)),
            pl.BlockSpec((1, 1, tn), lambda l, j: (l, 0, j)),
        ],
        out_specs=pl.BlockSpec((1, 8, tn), lambda l, j: (l, 0, j)),
        out_shape=jax.ShapeDtypeStruct((depth, 8, cols), F32),
        compiler_params=_params("arbitrary", "arbitrary"),
        name="adaln_mod",
    )(c8, w_ada, b_ada.reshape(depth, 1, cols))
    return out[:, 0, :]


def _norm_mod(x, g, scale, shift):
    y = x * lax.rsqrt(jnp.mean(x * x, axis=-1, keepdims=True) + EPS)
    return (y * g) * (1.0 + scale) + shift


INPROJ_TN = 1024
assert COL_SB % INPROJ_TN == 0 and 2 * SB_HEADS * SB_HEAD_DIM == INPROJ_TN


def _inproj_kernel(x_ref, g_ref, sc_ref, sh_ref, w_ref, wn_ref, qkg_ref, qks_ref, o_ref, on_ref, h_scr):
    j = pl.program_id(1)

    @pl.when(j == 0)
    def _():
        h = _bf(_norm_mod(x_ref[...], g_ref[...], sc_ref[...], sh_ref[...]))
        h_scr[...] = h
        on_ref[...] = _dot(h, wn_ref[...])

    @pl.when(j != COL_SB // INPROJ_TN)
    def _():
        o_ref[...] = _bf(_dot(h_scr[...], w_ref[...]))

    @pl.when(j == COL_SB // INPROJ_TN)
    def _():
        acc = _dot(h_scr[...], w_ref[...])
        dh = SB_HEAD_DIM
        for n in range(INPROJ_TN // dh):
            cols = slice(n * dh, (n + 1) * dh)
            o_ref[:, cols] = _bf(_head_rms(acc[:, cols], qkg_ref[:, cols]) * qks_ref[:, cols])


def _inproj(x2, g, scale, shift, w_wide, w_narrow, q_g, k_g):
    s, d = x2.shape
    tm = min(1024, s)
    tn = INPROJ_TN
    row = lambda a: a.reshape(1, d)
    vec = pl.BlockSpec((1, d), lambda i, j: (0, 0))
    qk_gain = jnp.concatenate([jnp.tile(q_g, SB_HEADS), jnp.tile(k_g, SB_HEADS)])
    qk_scale = jnp.concatenate([jnp.full((tn // 2,), SB_HEAD_DIM ** -0.5, F32), jnp.ones((tn // 2,), F32)])
    return pl.pallas_call(
        _inproj_kernel,
        grid=(s // tm, WIDE_COLS // tn),
        in_specs=[pl.BlockSpec((tm, d), lambda i, j: (i, 0)), vec, vec, vec,
                  pl.BlockSpec((d, tn), lambda i, j: (0, j)),
                  pl.BlockSpec((d, NARROW_COLS), lambda i, j: (0, 0)),
                  pl.BlockSpec((1, tn), lambda i, j: (0, 0)), pl.BlockSpec((1, tn), lambda i, j: (0, 0))],
        out_specs=[pl.BlockSpec((tm, tn), lambda i, j: (i, j)),
                   pl.BlockSpec((tm, NARROW_COLS), lambda i, j: (i, 0))],
        out_shape=[jax.ShapeDtypeStruct((s, WIDE_COLS), BF16), jax.ShapeDtypeStruct((s, NARROW_COLS), F32)],
        scratch_shapes=[pltpu.VMEM((tm, d), BF16)],
        compiler_params=_params("arbitrary", "arbitrary"),
        name="inproj",
    )(x2, row(g), row(scale), row(shift), w_wide, w_narrow, qk_gain.reshape(1, tn), qk_scale.reshape(1, tn))


def _causal_conv4(x, ext_scr, w_ref):
    tb = x.shape[0]
    ext_scr[8:8 + tb, :] = x
    y = x * w_ref[3:4, :]
    for k in (1, 2, 3):
        y = y + ext_scr[8 - k:8 - k + tb, :] * w_ref[3 - k:4 - k, :]
    ext_scr[0:8, :] = x[tb - 8:tb]
    return y


def _ssd_kernel(z_ref, xbc_ref, dt_ref, dtt_ref, cw_ref, cb_ref, dtb_ref, dtbt_ref, al_ref, alt_ref,
                dsk_ref, ng_ref, o_ref, ext_scr, act_scr, y_scr, st_scr):
    tb = xbc_ref.shape[0]
    L = SSD_CHUNK
    P = SSM_HEAD_DIM

    @pl.when(pl.program_id(0) == 0)
    def _():
        ext_scr[0:8, :] = jnp.zeros((8, ext_scr.shape[1]), F32)
        st_scr[...] = jnp.zeros_like(st_scr)

    act_scr[...] = _silu(_causal_conv4(xbc_ref[...].astype(F32), ext_scr, cw_ref) + cb_ref[...])

    ri = _iota2((L, L), 0)
    ci = _iota2((L, L), 1)
    tril = (ri >= ci).astype(F32)
    causal = ri >= ci
    a_col = -jnp.exp(al_ref[...])
    a_row = -jnp.exp(alt_ref[...])
    expand = (_iota2((LANES, SSM_INNER), 1) // P == _iota2((LANES, SSM_INNER), 0)).astype(F32)
    hpg = SSM_HEADS // SSM_GROUPS
    gw = hpg * P

    for c in range(tb // L):
        rows = slice(c * L, (c + 1) * L)
        dt = _softplus(dt_ref[rows, :] + dtb_ref[...])
        dtt = _softplus(dtt_ref[:, rows] + dtbt_ref[...])
        acum = _dot_sel(tril, dt * a_col)
        acum_t = _dot_sel_r(dtt * a_row, tril.T)
        per_head = jnp.concatenate([dt, jnp.exp(acum[L - 1:L, :] - acum), jnp.exp(acum)], axis=0)
        per_chan = _dot_sel_r(per_head, expand)
        dt_e, to_end_e, from_start_e = per_chan[:L], per_chan[L:2 * L], per_chan[2 * L:]
        act = act_scr[rows, :]
        xs = act[:, :SSM_INNER]
        xdt = xs * dt_e
        xdt_b = _bf(xdt)
        xdt_end_b = _bf(xdt * to_end_e)
        off_c = SSM_INNER + SSM_GROUPS * SSM_STATE
        y_in, y_st = [], []
        for g in range(SSM_GROUPS):
            bm = _bf(act[:, SSM_INNER + g * SSM_STATE:SSM_INNER + (g + 1) * SSM_STATE])
            cm = _bf(act[:, off_c + g * SSM_STATE:off_c + (g + 1) * SSM_STATE])
            gmat = _dot_nt(cm, bm)
            st = st_scr[g]
            y_st.append(_dot_nt(cm, _bf(st)))
            new = _dot_tn(xdt_end_b[:, g * gw:(g + 1) * gw], bm)
            for hh in range(hpg):
                h = g * hpg + hh
                seg = jnp.exp(jnp.where(causal, acum[:, h:h + 1] - acum_t[h:h + 1, :], -jnp.inf))
                y_in.append(_dot(_bf(gmat * seg), xdt_b[:, h * P:(h + 1) * P]))
                sl = slice(hh * P, (hh + 1) * P)
                st_scr[g, sl, :] = st[sl] * jnp.exp(acum_t[h:h + 1, L - 1:L]) + new[sl]
        y_scr[rows, :] = (jnp.concatenate(y_in, axis=-1) + jnp.concatenate(y_st, axis=-1) * from_start_e
                          + xs * dsk_ref[...])

    y = y_scr[...] * _silu(z_ref[...].astype(F32))
    gsz = SSM_INNER // SSM_GROUPS
    outs = []
    for g in range(SSM_GROUPS):
        yg = y[:, g * gsz:(g + 1) * gsz]
        yn = yg * lax.rsqrt(jnp.mean(yg * yg, axis=-1, keepdims=True) + EPS)
        outs.append(yn * ng_ref[:, g * gsz:(g + 1) * gsz])
    o_ref[...] = jnp.concatenate(outs, axis=-1)


def _ssd(proj, narrow, dtt, conv_w, conv_b, dt_bias, a_log, d_skip, norm_g):
    s = proj.shape[0]
    tb = min(256, s)
    pad8 = lambda v: jnp.pad(v, (0, LANES - v.shape[0])).reshape(1, LANES)
    colv = lambda v: v.reshape(SSM_HEADS, 1)
    full = lambda shape: pl.BlockSpec(shape, lambda i: (0,) * len(shape))
    return pl.pallas_call(
        _ssd_kernel,
        grid=(s // tb,),
        in_specs=[
            _col_spec(tb, SSM_INNER, COL_Z),
            _col_spec(tb, SSM_XBC, COL_XBC),
            _col_spec(tb, LANES, NCOL_DT),
            pl.BlockSpec((SSM_HEADS, tb), lambda i: (0, i)),
            full((4, SSM_XBC)), full((1, SSM_XBC)), full((1, LANES)), full((SSM_HEADS, 1)),
            full((1, LANES)), full((SSM_HEADS, 1)), full((1, SSM_INNER)), full((1, SSM_INNER)),
        ],
        out_specs=pl.BlockSpec((tb, SSM_INNER), lambda i: (i, 0)),
        out_shape=jax.ShapeDtypeStruct((s, SSM_INNER), F32),
        scratch_shapes=[
            pltpu.VMEM((tb + 8, SSM_XBC), F32),
            pltpu.VMEM((tb, SSM_XBC), F32),
            pltpu.VMEM((tb, SSM_INNER), F32),
            pltpu.VMEM((SSM_GROUPS, SSM_HEADS // SSM_GROUPS * SSM_HEAD_DIM, SSM_STATE), F32),
        ],
        compiler_params=_params("arbitrary"),
        name="ssd",
    )(proj, proj, narrow, dtt, conv_w, conv_b.reshape(1, -1), pad8(dt_bias), colv(dt_bias),
      pad8(a_log), colv(a_log), jnp.repeat(d_skip, SSM_HEAD_DIM).reshape(1, -1), norm_g.reshape(1, -1))


def _layout_w_in(w):
    d = w.shape[0]
    z, xbc, dt, sb, gqkv, gab, ggate, br = jnp.split(w, [512, 1536, 1544, 3080, 4616, 4624, 5136], axis=1)
    pad = jnp.zeros((d, LANES - 8), w.dtype)
    wide = jnp.concatenate([br, sb, gqkv, xbc, z, ggate], axis=1).astype(BF16)
    narrow = jnp.concatenate([dt, pad, gab, pad], axis=1).astype(BF16)
    return wide, narrow


def _head_rms(x, g):
    return (x * lax.rsqrt(jnp.mean(x * x, axis=-1, keepdims=True) + EPS)) * g


def _sb_blocks(qs, kns, vs, accs, suffix, masked):
    blk = qs[0].shape[0]
    strict = _iota2((blk, blk), 1) < _iota2((blk, blk), 0)
    zs = [_dot_nt(q, kn) for q, kn in zip(qs, kns)]
    sps = [jnp.maximum(z, 0.0) + jnp.log1p(jnp.exp(-jnp.abs(z))) for z in zs]
    log_keeps = [jnp.where(strict, -sp, 0.0) if masked else -sp for sp in sps]
    splits = [_split2(lk) for lk in log_keeps]
    afters = [(_dot(hi, suffix) + _dot(lo, suffix)) + acc for (hi, lo), acc in zip(splits, accs)]
    atts = [jnp.exp((z - sp) + after) for z, sp, after in zip(zs, sps, afters)]
    if masked:
        atts = [jnp.where(strict, att, 0.0) for att in atts]
    outs = [_dot(_bf(att), v) for att, v in zip(atts, vs)]
    return outs, [jnp.sum(lk, axis=-1, keepdims=True) for lk in log_keeps]


SB_QBLOCKS = 2


def _sb_kernel(q_ref, k_ref, v_ref, o_ref, acc_scr):
    blk = SB_BLOCK
    dh = SB_HEAD_DIM
    first = pl.program_id(0) * SB_QBLOCKS
    pairs = [(b, h) for b in range(SB_QBLOCKS) for h in range(SB_HEADS)]
    suffix = (_iota2((blk, blk), 0) > _iota2((blk, blk), 1)).astype(BF16)
    qs = [q_ref[b * blk:(b + 1) * blk, h * dh:(h + 1) * dh] for b, h in pairs]

    def load_kv(offset):
        rows = [pl.ds(pl.multiple_of(jnp.maximum(first + b - offset, 0) * blk, blk), blk) for b in range(SB_QBLOCKS)]
        return ([k_ref[rows[b], h * dh:(h + 1) * dh] for b, h in pairs],
                [v_ref[rows[b], h * dh:(h + 1) * dh] for b, h in pairs])

    def live(accs):
        top = functools.reduce(jnp.maximum, accs)
        return (jnp.max(top) > SB_SKIP_LOG).astype(jnp.int32)

    def out_slice(n):
        b, h = pairs[n]
        return (slice(b * blk, (b + 1) * blk), slice(h * dh, (h + 1) * dh))

    kns, vs = load_kv(0)
    outs, sums = _sb_blocks(qs, kns, vs, [jnp.zeros((blk, 1), F32)] * len(pairs), suffix, True)
    for n in range(len(pairs)):
        o_ref[out_slice(n)] = outs[n]
        acc_scr[n] = sums[n]

    def cond(carry):
        offset, alive = carry
        return jnp.logical_and(offset <= first + SB_QBLOCKS - 1, alive > 0)

    def body(carry):
        offset, _ = carry
        kns, vs = load_kv(offset)
        accs = [acc_scr[n] for n in range(len(pairs))]
        outs, sums = _sb_blocks(qs, kns, vs, accs, suffix, False)
        valid = [jnp.where(first + b - offset >= 0, 1.0, 0.0) for b in range(SB_QBLOCKS)]
        accs = [acc + rs * valid[b] for acc, rs, (b, _) in zip(accs, sums, pairs)]
        for n in range(len(pairs)):
            o_ref[out_slice(n)] += outs[n] * valid[pairs[n][0]]
            acc_scr[n] = accs[n]
        return offset + 1, live(accs)

    lax.while_loop(cond, body, (jnp.int32(1), live(sums)))


def _stick_breaking(proj):
    s = proj.shape[0]
    tq = SB_QBLOCKS * SB_BLOCK
    dh = SB_HEAD_DIM
    width = SB_HEADS * dh
    resident = lambda col: pl.BlockSpec((s, width), lambda i: (0, col // width), pipeline_mode=pl.Buffered(1))
    return pl.pallas_call(
        _sb_kernel,
        grid=(s // tq,),
        in_specs=[_col_spec(tq, width, COL_SB), resident(COL_SB + width), resident(COL_SB + 2 * width)],
        out_specs=pl.BlockSpec((tq, width), lambda i: (i, 0)),
        out_shape=jax.ShapeDtypeStruct((s, width), F32),
        scratch_shapes=[pltpu.VMEM((SB_QBLOCKS * SB_HEADS, SB_BLOCK, 1), F32)],
        compiler_params=_params("arbitrary"),
        name="stick_breaking",
    )(proj, proj, proj)


GDN_SUB = 128


def _dot3_nt(a, b):
    ah, al = _split2(a)
    bh, bl = _split2(b)
    return _dot_nt(ah, bh) + (_dot_nt(ah, bl) + _dot_nt(al, bh))


def _chunk_lower_inverses(ms, chunk):
    n = ms[0].shape[0]
    eye = (_iota2((n, n), 0) == _iota2((n, n), 1)).astype(F32)
    ps = [-m for m in ms]
    invs = [eye + p for p in ps]
    p_parts = [_split2(p) for p in ps]
    for j in range((chunk - 1).bit_length() - 1):
        if j == 0:
            ps = [_dot(ph, ph) + (_dot(ph, pl_) + _dot(pl_, ph)) for ph, pl_ in p_parts]
        else:
            ps = [_dot(ph, ph) for ph, _ in p_parts]
        p_parts = [_split2(p) for p in ps]
        inv_parts = [_split2(inv) for inv in invs]
        invs = [inv + (_dot(ih, ph) + (_dot(ih, pl_) + _dot(il, ph)))
                for inv, (ih, il), (ph, pl_) in zip(invs, inv_parts, p_parts)]
    return invs


def _gdn_kernel(qkv_ref, gab_ref, gabt_ref, gate_ref, cw_ref, al_ref, alt_ref, dtb_ref, dtbt_ref, ng_ref,
                o_ref, ext_scr, act_scr, st_scr):
    tb = qkv_ref.shape[0]
    C = GDN_CHUNK
    dh = GDN_HEAD_DIM
    inner = GDN_HEADS * dh
    heads = range(GDN_HEADS)

    @pl.when(pl.program_id(0) == 0)
    def _():
        ext_scr[0:8, :] = jnp.zeros((8, ext_scr.shape[1]), F32)
        st_scr[...] = jnp.zeros_like(st_scr)

    act_scr[...] = _silu(_causal_conv4(qkv_ref[...].astype(F32), ext_scr, cw_ref))

    sub = min(GDN_SUB, tb)
    cps = sub // C
    ri = _iota2((sub, sub), 0)
    ci = _iota2((sub, sub), 1)
    same = (ri // C) == (ci // C)
    incl = jnp.logical_and(same, ri >= ci)
    strict = jnp.logical_and(same, ri > ci)
    tril = incl.astype(F32)
    al_col = -jnp.exp(al_ref[...])
    al_row = -jnp.exp(alt_ref[...])
    units = [(b, h) for b in range(tb // sub) for h in heads]

    gc_cols, gc_rows, betas = [], [], []
    for b in range(tb // sub):
        rows = slice(b * sub, (b + 1) * sub)
        gab = gab_ref[rows, :]
        g_col = al_col * _softplus(gab + dtb_ref[...])
        g_row = al_row * _softplus(gabt_ref[:, rows] + dtbt_ref[...])
        gc_cols.append(_dot_sel(tril, g_col))
        gc_rows.append(_dot_sel_r(g_row, tril.T))
        betas.append(jax.nn.sigmoid(gab))

    qs, ks, kbs, gcs, decays, rhss = [], [], [], [], [], []
    for b, h in units:
        rows = slice(b * sub, (b + 1) * sub)
        q = act_scr[rows, h * dh:(h + 1) * dh]
        k = act_scr[rows, inner + h * dh:inner + (h + 1) * dh]
        v = act_scr[rows, 2 * inner + h * dh:2 * inner + (h + 1) * dh]
        q = q * lax.rsqrt(jnp.sum(q * q, axis=-1, keepdims=True) + EPS) * (dh ** -0.5)
        k = k * lax.rsqrt(jnp.sum(k * k, axis=-1, keepdims=True) + EPS)
        beta = betas[b][:, GDN_HEADS + h:GDN_HEADS + h + 1]
        gc = gc_cols[b][:, h:h + 1]
        kb = k * beta
        qs.append(q)
        ks.append(k)
        kbs.append(kb)
        gcs.append(gc)
        decays.append(jnp.exp(jnp.where(incl, gc - gc_rows[b][h:h + 1, :], -jnp.inf)))
        rhss.append(jnp.concatenate([v * beta, kb * jnp.exp(gc)], axis=-1))

    n_units = range(len(units))
    ms = [jnp.where(strict, _dot3_nt(kbs[n], ks[n]) * decays[n], 0.0) for n in n_units]
    invs = _chunk_lower_inverses(ms, C)
    sols = [_dot3(invs[n], rhss[n]) for n in n_units]
    attns = [_bf(jnp.where(incl, _dot_nt(_bf(qs[n]), _bf(ks[n])) * decays[n], 0.0)) for n in n_units]
    q_decs = [_bf(qs[n] * jnp.exp(gcs[n])) for n in n_units]

    sts = [st_scr[h] for h in heads]
    v_news = [[] for _ in n_units]
    o_inters = [[] for _ in n_units]
    for c in range(tb // C):
        b = c // cps
        rows = slice((c % cps) * C, (c % cps + 1) * C)
        for h in heads:
            n = b * GDN_HEADS + h
            st_b = _bf(sts[h])
            g_last = gc_rows[b][h:h + 1, rows.stop - 1:rows.stop]
            v_new = sols[n][rows, :dh] - _dot(_bf(sols[n][rows, dh:]), st_b)
            v_new_b = _bf(v_new)
            o_inters[n].append(_dot(q_decs[n][rows, :], st_b))
            k_dec = ks[n][rows, :] * jnp.exp(g_last - gcs[n][rows, :])
            sts[h] = sts[h] * jnp.exp(g_last) + _dot_tn(_bf(k_dec), v_new_b)
            v_news[n].append(v_new_b)
    for h in heads:
        st_scr[h] = sts[h]
    for n, (b, h) in enumerate(units):
        rows = slice(b * sub, (b + 1) * sub)
        o = jnp.concatenate(o_inters[n], axis=0) + _dot(attns[n], jnp.concatenate(v_news[n], axis=0))
        o = _head_rms(o, ng_ref[...]) * _silu(gate_ref[rows, h * dh:(h + 1) * dh].astype(F32))
        o_ref[rows, h * dh:(h + 1) * dh] = o


def _gdn(proj, narrow, gabt, conv_w, a_log, dt_bias, norm_g):
    s = proj.shape[0]
    tb = min(256, s)
    inner = GDN_HEADS * GDN_HEAD_DIM
    pad_lane = lambda v: jnp.pad(v, (0, LANES - v.shape[0])).reshape(1, LANES)
    pad_col = lambda v: jnp.pad(v, (0, 8 - v.shape[0])).reshape(8, 1)
    full = lambda shape: pl.BlockSpec(shape, lambda i: (0,) * len(shape))
    return pl.pallas_call(
        _gdn_kernel,
        grid=(s // tb,),
        in_specs=[
            _col_spec(tb, 3 * inner, COL_GQKV),
            _col_spec(tb, LANES, NCOL_GAB),
            pl.BlockSpec((8, tb), lambda i: (0, i)),
            _col_spec(tb, inner, COL_GGATE),
            full((4, 3 * inner)), full((1, LANES)), full((8, 1)), full((1, LANES)), full((8, 1)),
            full((1, GDN_HEAD_DIM)),
        ],
        out_specs=pl.BlockSpec((tb, inner), lambda i: (i, 0)),
        out_shape=jax.ShapeDtypeStruct((s, inner), F32),
        scratch_shapes=[
            pltpu.VMEM((tb + 8, 3 * inner), F32),
            pltpu.VMEM((tb, 3 * inner), F32),
            pltpu.VMEM((GDN_HEADS, GDN_HEAD_DIM, GDN_HEAD_DIM), F32),
        ],
        compiler_params=_params("arbitrary"),
        name="gdn",
    )(proj, narrow, gabt, proj, conv_w, pad_lane(a_log), pad_col(a_log), pad_lane(dt_bias), pad_col(dt_bias),
      norm_g.reshape(1, -1))


def _merge_kernel(ya_ref, yb_ref, yc_ref, br_ref, x_ref, wbr_ref, wout_ref, gm_ref, g_ref, sc_ref, sh_ref,
                  wrt_ref, brt_ref, xo_ref, h_ref, ids_ref, wts_ref, cnt_ref, seen_scr):
    d = x_ref.shape[1]

    @pl.when(pl.program_id(0) == 0)
    def _():
        seen_scr[...] = jnp.zeros_like(seen_scr)

    merged = None
    for i, y_ref in enumerate((ya_ref, yb_ref, yc_ref)):
        gate = jax.nn.sigmoid(br_ref[:, i * d:(i + 1) * d].astype(F32))
        term = gate * _dot(_bf(y_ref[...]), wbr_ref[i])
        merged = term if merged is None else merged + term
    x_new = x_ref[...] + gm_ref[...] * _dot(_bf(merged), wout_ref[...])
    xo_ref[...] = x_new
    h = _norm_mod(x_new, g_ref[...], sc_ref[...], sh_ref[...])
    h_ref[...] = h
    ids, wts, seen = _route_rows(_dot3(h, wrt_ref[...]) + brt_ref[...], seen_scr[...])
    ids_ref[...] = ids
    wts_ref[...] = wts
    seen_scr[...] = seen
    cnt_ref[...] = jnp.broadcast_to(seen, cnt_ref.shape)


def _merge(ya, yb, yc, proj, x2, wbr_bf, wout_bf, gate_m, g, scale, shift, w_rt, b_rt):
    s, d = x2.shape
    tb = min(512, s)
    bw = ya.shape[1]
    row = lambda a: a.reshape(1, -1)
    vec = pl.BlockSpec((1, d), lambda i: (0, 0))
    blk = lambda w: pl.BlockSpec((tb, w), lambda i: (i, 0))
    return pl.pallas_call(
        _merge_kernel,
        grid=(s // tb,),
        in_specs=[blk(bw), blk(bw), blk(bw), _col_spec(tb, 3 * d, COL_BR), blk(d),
                  pl.BlockSpec((3, bw, d), lambda i: (0, 0, 0)), pl.BlockSpec((d, d), lambda i: (0, 0)),
                  vec, vec, vec, vec,
                  pl.BlockSpec((d, LANES), lambda i: (0, 0)), pl.BlockSpec((1, LANES), lambda i: (0, 0))],
        out_specs=[blk(d), blk(d), blk(LANES), blk(LANES), pl.BlockSpec((8, LANES), lambda i: (0, 0))],
        out_shape=[jax.ShapeDtypeStruct((s, d), F32), jax.ShapeDtypeStruct((s, d), F32),
                   jax.ShapeDtypeStruct((s, LANES), jnp.int32), jax.ShapeDtypeStruct((s, LANES), F32),
                   jax.ShapeDtypeStruct((8, LANES), F32)],
        scratch_shapes=[pltpu.VMEM((1, LANES), F32)],
        compiler_params=_params("arbitrary"),
        name="merge",
    )(ya, yb, yc, proj, x2, wbr_bf, wout_bf, row(gate_m), row(g), row(scale), row(shift), w_rt, row(b_rt))


ROUTE_E0 = MOE_GROUPS


def _route_rows(lg, seen_before):
    tb = lg.shape[0]
    lane = _iota2((tb, LANES), 1)
    big = jnp.int32(LANES)
    neg = -jnp.inf
    gl = jnp.where(lane < MOE_GROUPS, lg, neg)
    gmax = jnp.max(gl, axis=-1, keepdims=True)
    g_sel = jnp.min(jnp.where(gl == gmax, lane, big), axis=-1, keepdims=True)
    p_group = 1.0 / jnp.sum(jnp.exp(gl - gmax), axis=-1, keepdims=True)
    lo = ROUTE_E0 + EXPERTS_PER_GROUP * g_sel
    el = jnp.where(jnp.logical_and(lane >= lo, lane < lo + EXPERTS_PER_GROUP), lg, neg)
    m1 = jnp.max(el, axis=-1, keepdims=True)
    i1 = jnp.min(jnp.where(el == m1, lane, big), axis=-1, keepdims=True)
    esum = jnp.sum(jnp.exp(el - m1), axis=-1, keepdims=True)
    el2 = jnp.where(lane == i1, neg, el)
    m2 = jnp.max(el2, axis=-1, keepdims=True)
    i2 = jnp.min(jnp.where(el2 == m2, lane, big), axis=-1, keepdims=True)
    p1 = 1.0 / esum
    p2 = jnp.exp(m2 - m1) / esum
    w1 = p_group * p1 / (p1 + p2)
    w2 = p_group * p2 / (p1 + p2)

    sel1 = lane == i1
    sel2 = lane == i2
    onehot = jnp.where(jnp.logical_or(sel1, sel2), 1.0, 0.0)
    before = (_iota2((tb, tb), 0) > _iota2((tb, tb), 1)).astype(BF16)
    seen = _dot(before, _bf(onehot)) + seen_before
    r1 = jnp.sum(jnp.where(sel1, seen, 0.0), axis=-1, keepdims=True)
    r2 = jnp.sum(jnp.where(sel2, seen, 0.0), axis=-1, keepdims=True)
    ids = jnp.where(lane == 0, i1 - ROUTE_E0, jnp.where(lane == 1, i2 - ROUTE_E0,
          jnp.where(lane == 2, r1.astype(jnp.int32), jnp.where(lane == 3, r2.astype(jnp.int32), 0))))
    wts = jnp.where(lane == 0, w1, jnp.where(lane == 1, w2, 0.0))
    return ids, wts, seen_before + jnp.sum(onehot, axis=0, keepdims=True)


EXPERT_BLOCK = 512
ROW_TB = 256


def _dispatch_kernel(dest_ref, pend_ref, padded_ref, nused_ref, h_ref, xb_ref, zero_scr, sem, zero_sem):
    tb = h_ref.shape[0]
    base = pl.program_id(0) * tb * MOE_TOP_K

    @pl.when(pl.program_id(0) == 0)
    def _():
        zero_scr[...] = jnp.zeros_like(zero_scr)

        def zero_block(start):
            return pltpu.make_async_copy(zero_scr, xb_ref.at[pl.ds(start, EXPERT_BLOCK), :], zero_sem)

        def for_each_zero_block(action):
            for e in range(N_EXPERTS):
                @pl.when(padded_ref[e] > 0)
                def _():
                    action(zero_block(pl.multiple_of(pend_ref[e] - EXPERT_BLOCK, EXPERT_BLOCK)))
            for b in range(xb_ref.shape[0] // EXPERT_BLOCK):
                @pl.when(b >= nused_ref[0])
                def _():
                    action(zero_block(b * EXPERT_BLOCK))

        for_each_zero_block(lambda copy: copy.start())
        for_each_zero_block(lambda copy: copy.wait())

    def row_copy(t, k, d):
        return pltpu.make_async_copy(h_ref.at[pl.ds(t, 1), :], xb_ref.at[pl.ds(d, 1), :], sem)

    def issue(t, carry):
        for k in range(MOE_TOP_K):
            row_copy(t, k, dest_ref[base + t * MOE_TOP_K + k]).start(priority=k % 2)
        return carry

    for t in range(tb):
        issue(t, 0)
    for k in range(MOE_TOP_K):
        pltpu.make_async_copy(h_ref, xb_ref.at[pl.ds(0, tb), :], sem).wait()


def _dispatch(dest, pend, padded, n_used, h, n_slots):
    s, d = h.shape
    tb = min(ROW_TB, s)
    return pl.pallas_call(
        _dispatch_kernel,
        grid_spec=pltpu.PrefetchScalarGridSpec(
            num_scalar_prefetch=4,
            grid=(s // tb,),
            in_specs=[pl.BlockSpec((tb, d), lambda i, *_: (i, 0))],
            out_specs=pl.BlockSpec(memory_space=pl.ANY),
            scratch_shapes=[pltpu.VMEM((EXPERT_BLOCK, d), F32), pltpu.SemaphoreType.DMA(()),
                            pltpu.SemaphoreType.DMA(())],
        ),
        out_shape=jax.ShapeDtypeStruct((n_slots, d), F32),
        compiler_params=_params("arbitrary"),
        name="dispatch",
    )(dest, pend, padded, n_used, h)


def _expert_kernel(be_ref, nused_ref, x_ref, wg_ref, wu_ref, wd_ref, o_ref, wg_b, wu_b, wd_b):
    b = pl.program_id(0)

    @pl.when(jnp.logical_or(b == 0, be_ref[b] != be_ref[jnp.maximum(b - 1, 0)]))
    def _():
        wg_b[...] = _bf(wg_ref[0, 0])
        wu_b[...] = _bf(wu_ref[0, 0])
        wd_b[...] = _bf(wd_ref[0, 0])

    @pl.when(b < nused_ref[0])
    def _():
        x = _bf(x_ref[...])
        hid = _silu(_dot(x, wg_b[...])) * _dot(x, wu_b[...])
        o_ref[...] = _dot(_bf(hid), wd_b[...])

    @pl.when(b >= nused_ref[0])
    def _():
        o_ref[...] = jnp.zeros_like(o_ref)


def _experts(block_expert, n_used, xb, layer, w_gate, w_up, w_down):
    n_slots, d = xb.shape
    ff = w_gate.shape[3]
    nb = n_slots // EXPERT_BLOCK
    return pl.pallas_call(
        _expert_kernel,
        grid_spec=pltpu.PrefetchScalarGridSpec(
            num_scalar_prefetch=2,
            grid=(nb,),
            in_specs=[
                pl.BlockSpec((EXPERT_BLOCK, d), lambda b, be, nu: (b, 0)),
                pl.BlockSpec((1, 1, d, ff), lambda b, be, nu: (layer, be[b], 0, 0)),
                pl.BlockSpec((1, 1, d, ff), lambda b, be, nu: (layer, be[b], 0, 0)),
                pl.BlockSpec((1, 1, ff, d), lambda b, be, nu: (layer, be[b], 0, 0)),
            ],
            out_specs=pl.BlockSpec((EXPERT_BLOCK, d), lambda b, be, nu: (b, 0)),
            scratch_shapes=[pltpu.VMEM((d, ff), BF16), pltpu.VMEM((d, ff), BF16), pltpu.VMEM((ff, d), BF16)],
        ),
        out_shape=jax.ShapeDtypeStruct((n_slots, d), F32),
        compiler_params=_params("arbitrary"),
        name="experts",
    )(block_expert, n_used, xb, w_gate, w_up, w_down)


def _combine_kernel(dest_ref, yb_ref, wts_ref, x_ref, gf_ref, o_ref, buf, sem):
    tb = x_ref.shape[0]
    base = pl.program_id(0) * tb * MOE_TOP_K

    def row_copy(t, k, d):
        return pltpu.make_async_copy(yb_ref.at[pl.ds(d, 1), :], buf.at[k, pl.ds(t, 1), :], sem)

    def issue(t, carry):
        for k in range(MOE_TOP_K):
            row_copy(t, k, dest_ref[base + t * MOE_TOP_K + k]).start(priority=k % 2)
        return carry

    for t in range(tb):
        issue(t, 0)
    for k in range(MOE_TOP_K):
        pltpu.make_async_copy(yb_ref.at[pl.ds(0, tb), :], buf.at[k], sem).wait()
    wts = wts_ref[...]
    y = wts[:, 0:1] * buf[0] + wts[:, 1:2] * buf[1]
    o_ref[...] = x_ref[...] + gf_ref[...] * y


def _combine(dest, yb, wts, x2, gate_f):
    s, d = x2.shape
    tb = min(ROW_TB, s)
    return pl.pallas_call(
        _combine_kernel,
        grid_spec=pltpu.PrefetchScalarGridSpec(
            num_scalar_prefetch=1,
            grid=(s // tb,),
            in_specs=[pl.BlockSpec(memory_space=pl.ANY),
                      pl.BlockSpec((tb, LANES), lambda i, dest: (i, 0)),
                      pl.BlockSpec((tb, d), lambda i, dest: (i, 0)),
                      pl.BlockSpec((1, d), lambda i, dest: (0, 0))],
            out_specs=pl.BlockSpec((tb, d), lambda i, dest: (i, 0)),
            scratch_shapes=[pltpu.VMEM((MOE_TOP_K, tb, d), F32), pltpu.SemaphoreType.DMA(())],
        ),
        out_shape=jax.ShapeDtypeStruct((s, d), F32),
        compiler_params=_params("arbitrary"),
        name="combine",
    )(dest, yb, wts, x2, gate_f.reshape(1, d))


def _moe(h, ids, wts, cnt, x2, gate_f, layer, w_gate, w_up, w_down):
    s = h.shape[0]
    counts = cnt[0, ROUTE_E0:ROUTE_E0 + N_EXPERTS].astype(jnp.int32)
    padded = (counts + EXPERT_BLOCK - 1) // EXPERT_BLOCK * EXPERT_BLOCK
    pend = jnp.cumsum(padded)
    pstart = pend - padded
    is_expert = ids[:, 0:MOE_TOP_K, None] == jnp.arange(N_EXPERTS, dtype=jnp.int32)
    slot0 = jnp.sum(jnp.where(is_expert, pstart, 0), axis=-1)
    dest = (slot0 + ids[:, MOE_TOP_K:2 * MOE_TOP_K]).reshape(s * MOE_TOP_K)
    nb = (s * MOE_TOP_K) // EXPERT_BLOCK + N_EXPERTS
    block_start = jnp.arange(nb, dtype=jnp.int32) * EXPERT_BLOCK
    block_expert = jnp.sum((pend[None, :] <= block_start[:, None]).astype(jnp.int32), axis=1)
    block_expert = jnp.minimum(block_expert, N_EXPERTS - 1)
    n_used = (pend[-1:] // EXPERT_BLOCK).astype(jnp.int32)
    xb = _dispatch(dest, pend.astype(jnp.int32), padded.astype(jnp.int32), n_used, h, nb * EXPERT_BLOCK)
    yb = _experts(block_expert, n_used, xb, layer, w_gate, w_up, w_down)
    return _combine(dest, yb, wts, x2, gate_f)


def kernel(x, c, w_ada, b_ada, norm_mix, norm_ffn, w_in, ssm_conv_w, ssm_conv_b, ssm_dt_bias, ssm_a_log, ssm_d,
           ssm_norm, sb_q_norm, sb_k_norm, gdn_conv_w, gdn_a_log, gdn_dt_bias, gdn_norm, w_branch, w_out,
           w_group, b_group, w_router, b_router, w_gate, w_up, w_down):
    bsz, s, d = x.shape
    assert bsz == 1 and d == D_MODEL
    depth = w_in.shape[0]
    mod = _adaln_mod(c, w_ada, b_ada)
    x2 = x.reshape(s, d)
    for l in range(depth):
        shift_m, scale_m, gate_m, shift_f, scale_f, gate_f = jnp.split(mod[l], 6)
        proj, narrow = _inproj(x2, norm_mix[l], scale_m, shift_m, *_layout_w_in(w_in[l]), sb_q_norm[l], sb_k_norm[l])
        dtt = narrow[:, NCOL_DT:NCOL_DT + 8].T
        gabt = narrow[:, NCOL_GAB:NCOL_GAB + 8].T
        ya = _ssd(proj, narrow, dtt, ssm_conv_w[l], ssm_conv_b[l], ssm_dt_bias[l], ssm_a_log[l], ssm_d[l], ssm_norm[l])
        yb = _stick_breaking(proj)
        yc = _gdn(proj, narrow, gabt, gdn_conv_w[l], gdn_a_log[l], gdn_dt_bias[l], gdn_norm[l])
        pad = jnp.zeros((d, LANES - MOE_GROUPS - N_EXPERTS), F32)
        w_rt = jnp.concatenate([w_group[l], w_router[l], pad], axis=1)
        b_rt = jnp.concatenate([b_group[l], b_router[l], pad[0]])
        x2, h, ids, wts, cnt = _merge(ya, yb, yc, proj, x2, _bf(w_branch[l]), _bf(w_out[l]), gate_m,
                               norm_ffn[l], scale_f, shift_f, w_rt, b_rt)
        x2 = _moe(h, ids, wts, cnt, x2, gate_f, l, w_gate, w_up, w_down)
    return x2.reshape(bsz, s, d)
```

```python
import functools

import jax
import jax.numpy as jnp
from jax import lax
from jax.experimental import pallas as pl
from jax.experimental.pallas import tpu as pltpu

F32 = jnp.float32
BF16 = jnp.bfloat16
EPS = 1e-6

D_MODEL = 1024
SSM_HEADS = 8
SSM_HEAD_DIM = 64
SSM_INNER = 512
SSM_GROUPS = 2
SSM_STATE = 128
SSM_XBC = 1024
SSD_CHUNK = 128
SB_HEADS = 4
SB_HEAD_DIM = 128
SB_BLOCK = 128
GDN_HEADS = 4
GDN_HEAD_DIM = 128
GDN_CHUNK = 64
MOE_GROUPS = 4
EXPERTS_PER_GROUP = 8
N_EXPERTS = 32
MOE_TOP_K = 2
EXPERT_FF = 512

LANES = 128
COL_BR = 0
COL_SB = 3072
COL_GQKV = 4608
COL_XBC = 6144
COL_Z = 7168
COL_GGATE = 7680
WIDE_COLS = 8192
NCOL_DT = 0
NCOL_GAB = 128
NARROW_COLS = 256


def _col_spec(tb, width, col):
    assert col % width == 0
    return pl.BlockSpec((tb, width), lambda i: (i, col // width))

VMEM_LIMIT = 48 * 1024 * 1024
SB_SKIP_LOG = -110.0


def _bf(x):
    return x.astype(BF16)


def _dot(a, b):
    return jnp.dot(a, b, preferred_element_type=F32)


def _dot_nt(a, b):
    return lax.dot_general(a, b, (((1,), (1,)), ((), ())), preferred_element_type=F32)


def _dot_tn(a, b):
    return lax.dot_general(a, b, (((0,), (0,)), ((), ())), preferred_element_type=F32)


def _split3(x):
    hi = _bf(x)
    r = x - hi.astype(F32)
    mid = _bf(r)
    return hi, mid, _bf(r - mid.astype(F32))


def _dot_sel(sel, x):
    sel_b = _bf(sel)
    hi, mid, lo = _split3(x)
    return _dot(sel_b, hi) + (_dot(sel_b, mid) + _dot(sel_b, lo))


def _dot_sel_r(x, sel):
    sel_b = _bf(sel)
    hi, mid, lo = _split3(x)
    return _dot(hi, sel_b) + (_dot(mid, sel_b) + _dot(lo, sel_b))


def _split2(x):
    hi = _bf(x)
    lo = _bf(x - hi.astype(F32))
    return hi, lo


def _dot3(a, b):
    ah, al = _split2(a)
    bh, bl = _split2(b)
    return _dot(ah, bh) + (_dot(ah, bl) + _dot(al, bh))


def _silu(x):
    return x * jax.nn.sigmoid(x)


def _softplus(x):
    return jnp.maximum(x, 0.0) + jnp.log1p(jnp.exp(-jnp.abs(x)))


def _iota2(shape, dim):
    return lax.broadcasted_iota(jnp.int32, shape, dim)


ROW_SUBLANES = D_MODEL // LANES
assert ROW_SUBLANES == 8


def _store_tile_rows(ref, x):
    n = x.shape[0]
    for sl in range(ROW_SUBLANES):
        ref[pl.ds(sl, n, stride=ROW_SUBLANES), :] = x[:, sl * LANES:(sl + 1) * LANES]


def _load_tile_rows(ref, n):
    return jnp.concatenate([ref[pl.ds(sl, n, stride=ROW_SUBLANES), :] for sl in range(ROW_SUBLANES)], axis=-1)


def _params(*sem):
    return pltpu.CompilerParams(dimension_semantics=sem, vmem_limit_bytes=VMEM_LIMIT)


def _mod_kernel(c_ref, w_ref, b_ref, o_ref):
    c = _silu(c_ref[...])
    o_ref[0] = _dot3(c, w_ref[0]) + b_ref[0]


def _adaln_mod(c, w_ada, b_ada):
    depth, d, cols = w_ada.shape
    tn = 1024
    c8 = jnp.broadcast_to(c, (8, d))
    out = pl.pallas_call(
        _mod_kernel,
        grid=(depth, cols // tn),
        in_specs=[
            pl.BlockSpec((8, d), lambda l, j: (0, 0)),
            pl.BlockSpec((1, d, tn), lambda l, j: (l, 0, j)),
            pl.BlockSpec((1, 1, tn), lambda l, j: (l, 0, j)),
        ],
        out_specs=pl.BlockSpec((1, 8, tn), lambda l, j: (l, 0, j)),
        out_shape=jax.ShapeDtypeStruct((depth, 8, cols), F32),
        compiler_params=_params("arbitrary", "arbitrary"),
        name="adaln_mod",
    )(c8, w_ada, b_ada.reshape(depth, 1, cols))
    return out[:, 0, :]


def _norm_mod(x, g, scale, shift):
    y = x * lax.rsqrt(jnp.mean(x * x, axis=-1, keepdims=True) + EPS)
    return (y * g) * (1.0 + scale) + shift


INPROJ_TN = 1024
assert COL_SB % INPROJ_TN == 0 and 2 * SB_HEADS * SB_HEAD_DIM == INPROJ_TN


def _inproj_kernel(x_ref, g_ref, sc_ref, sh_ref, w_ref, wn_ref, qkg_ref, qks_ref, o_ref, on_ref, h_scr):
    j = pl.program_id(1)

    @pl.when(j == 0)
    def _():
        h = _bf(_norm_mod(x_ref[...], g_ref[...], sc_ref[...], sh_ref[...]))
        h_scr[...] = h
        on_ref[...] = _dot(h, wn_ref[...])

    @pl.when(j != COL_SB // INPROJ_TN)
    def _():
        o_ref[...] = _bf(_dot(h_scr[...], w_ref[...]))

    @pl.when(j == COL_SB // INPROJ_TN)
    def _():
        acc = _dot(h_scr[...], w_ref[...])
        dh = SB_HEAD_DIM
        for n in range(INPROJ_TN // dh):
            cols = slice(n * dh, (n + 1) * dh)
            o_ref[:, cols] = _bf(_head_rms(acc[:, cols], qkg_ref[:, cols]) * qks_ref[:, cols])


def _inproj(x2, g, scale, shift, w_wide, w_narrow, q_g, k_g):
    s, d = x2.shape
    tm = min(1024, s)
    tn = INPROJ_TN
    row = lambda a: a.reshape(1, d)
    vec = pl.BlockSpec((1, d), lambda i, j: (0, 0))
    qk_gain = jnp.concatenate([jnp.tile(q_g, SB_HEADS), jnp.tile(k_g, SB_HEADS)])
    qk_scale = jnp.concatenate([jnp.full((tn // 2,), SB_HEAD_DIM ** -0.5, F32), jnp.ones((tn // 2,), F32)])
    return pl.pallas_call(
        _inproj_kernel,
        grid=(s // tm, WIDE_COLS // tn),
        in_specs=[pl.BlockSpec((tm, d), lambda i, j: (i, 0)), vec, vec, vec,
                  pl.BlockSpec((d, tn), lambda i, j: (0, j)),
                  pl.BlockSpec((d, NARROW_COLS), lambda i, j: (0, 0)),
                  pl.BlockSpec((1, tn), lambda i, j: (0, 0)), pl.BlockSpec((1, tn), lambda i, j: (0, 0))],
        out_specs=[pl.BlockSpec((tm, tn), lambda i, j: (i, j)),
                   pl.BlockSpec((tm, NARROW_COLS), lambda i, j: (i, 0))],
        out_shape=[jax.ShapeDtypeStruct((s, WIDE_COLS), BF16), jax.ShapeDtypeStruct((s, NARROW_COLS), F32)],
        scratch_shapes=[pltpu.VMEM((tm, d), BF16)],
        compiler_params=_params("arbitrary", "arbitrary"),
        name="inproj",
    )(x2, row(g), row(scale), row(shift), w_wide, w_narrow, qk_gain.reshape(1, tn), qk_scale.reshape(1, tn))


def _causal_conv4(x, ext_scr, w_ref):
    tb = x.shape[0]
    ext_scr[8:8 + tb, :] = x
    y = x * w_ref[3:4, :]
    for k in (1, 2, 3):
        y = y + ext_scr[8 - k:8 - k + tb, :] * w_ref[3 - k:4 - k, :]
    ext_scr[0:8, :] = x[tb - 8:tb]
    return y


def _ssd_kernel(z_ref, xbc_ref, dt_ref, dtt_ref, cw_ref, cb_ref, dtb_ref, dtbt_ref, al_ref, alt_ref,
                dsk_ref, ng_ref, o_ref, ext_scr, act_scr, y_scr, st_scr):
    tb = xbc_ref.shape[0]
    L = SSD_CHUNK
    P = SSM_HEAD_DIM

    @pl.when(pl.program_id(0) == 0)
    def _():
        ext_scr[0:8, :] = jnp.zeros((8, ext_scr.shape[1]), F32)
        st_scr[...] = jnp.zeros_like(st_scr)

    act_scr[...] = _silu(_causal_conv4(xbc_ref[...].astype(F32), ext_scr, cw_ref) + cb_ref[...])

    ri = _iota2((L, L), 0)
    ci = _iota2((L, L), 1)
    tril = (ri >= ci).astype(F32)
    causal = ri >= ci
    a_col = -jnp.exp(al_ref[...])
    a_row = -jnp.exp(alt_ref[...])
    expand = (_iota2((LANES, SSM_INNER), 1) // P == _iota2((LANES, SSM_INNER), 0)).astype(F32)
    hpg = SSM_HEADS // SSM_GROUPS
    gw = hpg * P

    for c in range(tb // L):
        rows = slice(c * L, (c + 1) * L)
        dt = _softplus(dt_ref[rows, :] + dtb_ref[...])
        dtt = _softplus(dtt_ref[:, rows] + dtbt_ref[...])
        acum = _dot_sel(tril, dt * a_col)
        acum_t = _dot_sel_r(dtt * a_row, tril.T)
        per_head = jnp.concatenate([dt, jnp.exp(acum[L - 1:L, :] - acum), jnp.exp(acum)], axis=0)
        per_chan = _dot_sel_r(per_head, expand)
        dt_e, to_end_e, from_start_e = per_chan[:L], per_chan[L:2 * L], per_chan[2 * L:]
        act = act_scr[rows, :]
        xs = act[:, :SSM_INNER]
        xdt = xs * dt_e
        xdt_b = _bf(xdt)
        xdt_end_b = _bf(xdt * to_end_e)
        off_c = SSM_INNER + SSM_GROUPS * SSM_STATE
        y_in, y_st = [], []
        for g in range(SSM_GROUPS):
            bm = _bf(act[:, SSM_INNER + g * SSM_STATE:SSM_INNER + (g + 1) * SSM_STATE])
            cm = _bf(act[:, off_c + g * SSM_STATE:off_c + (g + 1) * SSM_STATE])
            gmat = _dot_nt(cm, bm)
            st = st_scr[g]
            y_st.append(_dot_nt(cm, _bf(st)))
            new = _dot_tn(xdt_end_b[:, g * gw:(g + 1) * gw], bm)
            for hh in range(hpg):
                h = g * hpg + hh
                seg = jnp.exp(jnp.where(causal, acum[:, h:h + 1] - acum_t[h:h + 1, :], -jnp.inf))
                y_in.append(_dot(_bf(gmat * seg), xdt_b[:, h * P:(h + 1) * P]))
                sl = slice(hh * P, (hh + 1) * P)
                st_scr[g, sl, :] = st[sl] * jnp.exp(acum_t[h:h + 1, L - 1:L]) + new[sl]
        y_scr[rows, :] = (jnp.concatenate(y_in, axis=-1) + jnp.concatenate(y_st, axis=-1) * from_start_e
                          + xs * dsk_ref[...])

    y = y_scr[...] * _silu(z_ref[...].astype(F32))
    gsz = SSM_INNER // SSM_GROUPS
    outs = []
    for g in range(SSM_GROUPS):
        yg = y[:, g * gsz:(g + 1) * gsz]
        yn = yg * lax.rsqrt(jnp.mean(yg * yg, axis=-1, keepdims=True) + EPS)
        outs.append(yn * ng_ref[:, g * gsz:(g + 1) * gsz])
    o_ref[...] = jnp.concatenate(outs, axis=-1)


def _ssd(proj, narrow, dtt, conv_w, conv_b, dt_bias, a_log, d_skip, norm_g):
    s = proj.shape[0]
    tb = min(256, s)
    pad8 = lambda v: jnp.pad(v, (0, LANES - v.shape[0])).reshape(1, LANES)
    colv = lambda v: v.reshape(SSM_HEADS, 1)
    full = lambda shape: pl.BlockSpec(shape, lambda i: (0,) * len(shape))
    return pl.pallas_call(
        _ssd_kernel,
        grid=(s // tb,),
        in_specs=[
            _col_spec(tb, SSM_INNER, COL_Z),
            _col_spec(tb, SSM_XBC, COL_XBC),
            _col_spec(tb, LANES, NCOL_DT),
            pl.BlockSpec((SSM_HEADS, tb), lambda i: (0, i)),
            full((4, SSM_XBC)), full((1, SSM_XBC)), full((1, LANES)), full((SSM_HEADS, 1)),
            full((1, LANES)), full((SSM_HEADS, 1)), full((1, SSM_INNER)), full((1, SSM_INNER)),
        ],
        out_specs=pl.BlockSpec((tb, SSM_INNER), lambda i: (i, 0)),
        out_shape=jax.ShapeDtypeStruct((s, SSM_INNER), F32),
        scratch_shapes=[
            pltpu.VMEM((tb + 8, SSM_XBC), F32),
            pltpu.VMEM((tb, SSM_XBC), F32),
            pltpu.VMEM((tb, SSM_INNER), F32),
            pltpu.VMEM((SSM_GROUPS, SSM_HEADS // SSM_GROUPS * SSM_HEAD_DIM, SSM_STATE), F32),
        ],
        compiler_params=_params("arbitrary"),
        name="ssd",
    )(proj, proj, narrow, dtt, conv_w, conv_b.reshape(1, -1), pad8(dt_bias), colv(dt_bias),
      pad8(a_log), colv(a_log), jnp.repeat(d_skip, SSM_HEAD_DIM).reshape(1, -1), norm_g.reshape(1, -1))


def _layout_w_in(w):
    d = w.shape[0]
    z, xbc, dt, sb, gqkv, gab, ggate, br = jnp.split(w, [512, 1536, 1544, 3080, 4616, 4624, 5136], axis=1)
    pad = jnp.zeros((d, LANES - 8), w.dtype)
    wide = jnp.concatenate([br, sb, gqkv, xbc, z, ggate], axis=1).astype(BF16)
    narrow = jnp.concatenate([dt, pad, gab, pad], axis=1).astype(BF16)
    return wide, narrow


def _head_rms(x, g):
    return (x * lax.rsqrt(jnp.mean(x * x, axis=-1, keepdims=True) + EPS)) * g


def _sb_blocks(qs, kns, vs, accs, suffix, masked):
    blk = qs[0].shape[0]
    strict = _iota2((blk, blk), 1) < _iota2((blk, blk), 0)
    zs = [_dot_nt(q, kn) for q, kn in zip(qs, kns)]
    sps = [jnp.maximum(z, 0.0) + jnp.log1p(jnp.exp(-jnp.abs(z))) for z in zs]
    log_keeps = [jnp.where(strict, -sp, 0.0) if masked else -sp for sp in sps]
    splits = [_split2(lk) for lk in log_keeps]
    afters = [(_dot(hi, suffix) + _dot(lo, suffix)) + acc for (hi, lo), acc in zip(splits, accs)]
    atts = [jnp.exp((z - sp) + after) for z, sp, after in zip(zs, sps, afters)]
    if masked:
        atts = [jnp.where(strict, att, 0.0) for att in atts]
    outs = [_dot(_bf(att), v) for att, v in zip(atts, vs)]
    return outs, [jnp.sum(lk, axis=-1, keepdims=True) for lk in log_keeps]


SB_QBLOCKS = 2


def _sb_kernel(q_ref, k_ref, v_ref, o_ref, acc_scr):
    blk = SB_BLOCK
    dh = SB_HEAD_DIM
    first = pl.program_id(0) * SB_QBLOCKS
    pairs = [(b, h) for b in range(SB_QBLOCKS) for h in range(SB_HEADS)]
    suffix = (_iota2((blk, blk), 0) > _iota2((blk, blk), 1)).astype(BF16)
    qs = [q_ref[b * blk:(b + 1) * blk, h * dh:(h + 1) * dh] for b, h in pairs]

    def load_kv(offset):
        rows = [pl.ds(pl.multiple_of(jnp.maximum(first + b - offset, 0) * blk, blk), blk) for b in range(SB_QBLOCKS)]
        return ([k_ref[rows[b], h * dh:(h + 1) * dh] for b, h in pairs],
                [v_ref[rows[b], h * dh:(h + 1) * dh] for b, h in pairs])

    def live(accs):
        top = functools.reduce(jnp.maximum, accs)
        return (jnp.max(top) > SB_SKIP_LOG).astype(jnp.int32)

    def out_slice(n):
        b, h = pairs[n]
        return (slice(b * blk, (b + 1) * blk), slice(h * dh, (h + 1) * dh))

    kns, vs = load_kv(0)
    outs, sums = _sb_blocks(qs, kns, vs, [jnp.zeros((blk, 1), F32)] * len(pairs), suffix, True)
    for n in range(len(pairs)):
        o_ref[out_slice(n)] = outs[n]
        acc_scr[n] = sums[n]

    def cond(carry):
        offset, alive = carry
        return jnp.logical_and(offset <= first + SB_QBLOCKS - 1, alive > 0)

    def body(carry):
        offset, _ = carry
        kns, vs = load_kv(offset)
        accs = [acc_scr[n] for n in range(len(pairs))]
        outs, sums = _sb_blocks(qs, kns, vs, accs, suffix, False)
        valid = [jnp.where(first + b - offset >= 0, 1.0, 0.0) for b in range(SB_QBLOCKS)]
        accs = [acc + rs * valid[b] for acc, rs, (b, _) in zip(accs, sums, pairs)]
        for n in range(len(pairs)):
            o_ref[out_slice(n)] += outs[n] * valid[pairs[n][0]]
            acc_scr[n] = accs[n]
        return offset + 1, live(accs)

    lax.while_loop(cond, body, (jnp.int32(1), live(sums)))


def _stick_breaking(proj):
    s = proj.shape[0]
    tq = SB_QBLOCKS * SB_BLOCK
    dh = SB_HEAD_DIM
    width = SB_HEADS * dh
    resident = lambda col: pl.BlockSpec((s, width), lambda i: (0, col // width), pipeline_mode=pl.Buffered(1))
    return pl.pallas_call(
        _sb_kernel,
        grid=(s // tq,),
        in_specs=[_col_spec(tq, width, COL_SB), resident(COL_SB + width), resident(COL_SB + 2 * width)],
        out_specs=pl.BlockSpec((tq, width), lambda i: (i, 0)),
        out_shape=jax.ShapeDtypeStruct((s, width), F32),
        scratch_shapes=[pltpu.VMEM((SB_QBLOCKS * SB_HEADS, SB_BLOCK, 1), F32)],
        compiler_params=_params("arbitrary"),
        name="stick_breaking",
    )(proj, proj, proj)


GDN_SUB = 128


def _dot3_nt(a, b):
    ah, al = _split2(a)
    bh, bl = _split2(b)
    return _dot_nt(ah, bh) + (_dot_nt(ah, bl) + _dot_nt(al, bh))


def _chunk_lower_inverses(ms, chunk):
    n = ms[0].shape[0]
    eye = (_iota2((n, n), 0) == _iota2((n, n), 1)).astype(F32)
    ps = [-m for m in ms]
    invs = [eye + p for p in ps]
    p_parts = [_split2(p) for p in ps]
    for j in range((chunk - 1).bit_length() - 1):
        if j == 0:
            ps = [_dot(ph, ph) + (_dot(ph, pl_) + _dot(pl_, ph)) for ph, pl_ in p_parts]
        else:
            ps = [_dot(ph, ph) for ph, _ in p_parts]
        p_parts = [_split2(p) for p in ps]
        inv_parts = [_split2(inv) for inv in invs]
        invs = [inv + (_dot(ih, ph) + (_dot(ih, pl_) + _dot(il, ph)))
                for inv, (ih, il), (ph, pl_) in zip(invs, inv_parts, p_parts)]
    return invs


def _gdn_kernel(qkv_ref, gab_ref, gabt_ref, gate_ref, cw_ref, al_ref, alt_ref, dtb_ref, dtbt_ref, ng_ref,
                o_ref, ext_scr, act_scr, st_scr):
    tb = qkv_ref.shape[0]
    C = GDN_CHUNK
    dh = GDN_HEAD_DIM
    inner = GDN_HEADS * dh
    heads = range(GDN_HEADS)

    @pl.when(pl.program_id(0) == 0)
    def _():
        ext_scr[0:8, :] = jnp.zeros((8, ext_scr.shape[1]), F32)
        st_scr[...] = jnp.zeros_like(st_scr)

    act_scr[...] = _silu(_causal_conv4(qkv_ref[...].astype(F32), ext_scr, cw_ref))

    sub = min(GDN_SUB, tb)
    cps = sub // C
    ri = _iota2((sub, sub), 0)
    ci = _iota2((sub, sub), 1)
    same = (ri // C) == (ci // C)
    incl = jnp.logical_and(same, ri >= ci)
    strict = jnp.logical_and(same, ri > ci)
    tril = incl.astype(F32)
    al_col = -jnp.exp(al_ref[...])
    al_row = -jnp.exp(alt_ref[...])
    units = [(b, h) for b in range(tb // sub) for h in heads]

    gc_cols, gc_rows, betas = [], [], []
    for b in range(tb // sub):
        rows = slice(b * sub, (b + 1) * sub)
        gab = gab_ref[rows, :]
        g_col = al_col * _softplus(gab + dtb_ref[...])
        g_row = al_row * _softplus(gabt_ref[:, rows] + dtbt_ref[...])
        gc_cols.append(_dot_sel(tril, g_col))
        gc_rows.append(_dot_sel_r(g_row, tril.T))
        betas.append(jax.nn.sigmoid(gab))

    qs, ks, kbs, gcs, decays, rhss = [], [], [], [], [], []
    for b, h in units:
        rows = slice(b * sub, (b + 1) * sub)
        q = act_scr[rows, h * dh:(h + 1) * dh]
        k = act_scr[rows, inner + h * dh:inner + (h + 1) * dh]
        v = act_scr[rows, 2 * inner + h * dh:2 * inner + (h + 1) * dh]
        q = q * lax.rsqrt(jnp.sum(q * q, axis=-1, keepdims=True) + EPS) * (dh ** -0.5)
        k = k * lax.rsqrt(jnp.sum(k * k, axis=-1, keepdims=True) + EPS)
        beta = betas[b][:, GDN_HEADS + h:GDN_HEADS + h + 1]
        gc = gc_cols[b][:, h:h + 1]
        kb = k * beta
        qs.append(q)
        ks.append(k)
        kbs.append(kb)
        gcs.append(gc)
        decays.append(jnp.exp(jnp.where(incl, gc - gc_rows[b][h:h + 1, :], -jnp.inf)))
        rhss.append(jnp.concatenate([v * beta, kb * jnp.exp(gc)], axis=-1))

    n_units = range(len(units))
    ms = [jnp.where(strict, _dot3_nt(kbs[n], ks[n]) * decays[n], 0.0) for n in n_units]
    invs = _chunk_lower_inverses(ms, C)
    sols = [_dot3(invs[n], rhss[n]) for n in n_units]
    attns = [_bf(jnp.where(incl, _dot_nt(_bf(qs[n]), _bf(ks[n])) * decays[n], 0.0)) for n in n_units]
    q_decs = [_bf(qs[n] * jnp.exp(gcs[n])) for n in n_units]

    sts = [st_scr[h] for h in heads]
    v_news = [[] for _ in n_units]
    o_inters = [[] for _ in n_units]
    for c in range(tb // C):
        b = c // cps
        rows = slice((c % cps) * C, (c % cps + 1) * C)
        for h in heads:
            n = b * GDN_HEADS + h
            st_b = _bf(sts[h])
            g_last = gc_rows[b][h:h + 1, rows.stop - 1:rows.stop]
            v_new = sols[n][rows, :dh] - _dot(_bf(sols[n][rows, dh:]), st_b)
            v_new_b = _bf(v_new)
            o_inters[n].append(_dot(q_decs[n][rows, :], st_b))
            k_dec = ks[n][rows, :] * jnp.exp(g_last - gcs[n][rows, :])
            sts[h] = sts[h] * jnp.exp(g_last) + _dot_tn(_bf(k_dec), v_new_b)
            v_news[n].append(v_new_b)
    for h in heads:
        st_scr[h] = sts[h]
    for n, (b, h) in enumerate(units):
        rows = slice(b * sub, (b + 1) * sub)
        o = jnp.concatenate(o_inters[n], axis=0) + _dot(attns[n], jnp.concatenate(v_news[n], axis=0))
        o = _head_rms(o, ng_ref[...]) * _silu(gate_ref[rows, h * dh:(h + 1) * dh].astype(F32))
        o_ref[rows, h * dh:(h + 1) * dh] = o


def _gdn(proj, narrow, gabt, conv_w, a_log, dt_bias, norm_g):
    s = proj.shape[0]
    tb = min(256, s)
    inner = GDN_HEADS * GDN_HEAD_DIM
    pad_lane = lambda v: jnp.pad(v, (0, LANES - v.shape[0])).reshape(1, LANES)
    pad_col = lambda v: jnp.pad(v, (0, 8 - v.shape[0])).reshape(8, 1)
    full = lambda shape: pl.BlockSpec(shape, lambda i: (0,) * len(shape))
    return pl.pallas_call(
        _gdn_kernel,
        grid=(s // tb,),
        in_specs=[
            _col_spec(tb, 3 * inner, COL_GQKV),
            _col_spec(tb, LANES, NCOL_GAB),
            pl.BlockSpec((8, tb), lambda i: (0, i)),
            _col_spec(tb, inner, COL_GGATE),
            full((4, 3 * inner)), full((1, LANES)), full((8, 1)), full((1, LANES)), full((8, 1)),
            full((1, GDN_HEAD_DIM)),
        ],
        out_specs=pl.BlockSpec((tb, inner), lambda i: (i, 0)),
        out_shape=jax.ShapeDtypeStruct((s, inner), F32),
        scratch_shapes=[
            pltpu.VMEM((tb + 8, 3 * inner), F32),
            pltpu.VMEM((tb, 3 * inner), F32),
            pltpu.VMEM((GDN_HEADS, GDN_HEAD_DIM, GDN_HEAD_DIM), F32),
        ],
        compiler_params=_params("arbitrary"),
        name="gdn",
    )(proj, narrow, gabt, proj, conv_w, pad_lane(a_log), pad_col(a_log), pad_lane(dt_bias), pad_col(dt_bias),
      norm_g.reshape(1, -1))


def _merge_kernel(ya_ref, yb_ref, yc_ref, br_ref, x_ref, wbr_ref, wout_ref, gm_ref, g_ref, sc_ref, sh_ref,
                  wrt_ref, brt_ref, xo_ref, h_ref, ids_ref, wts_ref, cnt_ref, seen_scr):
    d = x_ref.shape[1]

    @pl.when(pl.program_id(0) == 0)
    def _():
        seen_scr[...] = jnp.zeros_like(seen_scr)

    merged = None
    for i, y_ref in enumerate((ya_ref, yb_ref, yc_ref)):
        gate = jax.nn.sigmoid(br_ref[:, i * d:(i + 1) * d].astype(F32))
        term = gate * _dot(_bf(y_ref[...]), wbr_ref[i])
        merged = term if merged is None else merged + term
    x_new = x_ref[...] + gm_ref[...] * _dot(_bf(merged), wout_ref[...])
    xo_ref[...] = x_new
    h = _norm_mod(x_new, g_ref[...], sc_ref[...], sh_ref[...])
    _store_tile_rows(h_ref, h)
    ids, wts, seen = _route_rows(_dot3(h, wrt_ref[...]) + brt_ref[...], seen_scr[...])
    ids_ref[...] = ids
    wts_ref[...] = wts
    seen_scr[...] = seen
    cnt_ref[...] = jnp.broadcast_to(seen, cnt_ref.shape)


def _merge(ya, yb, yc, proj, x2, wbr_bf, wout_bf, gate_m, g, scale, shift, w_rt, b_rt):
    s, d = x2.shape
    tb = min(512, s)
    bw = ya.shape[1]
    row = lambda a: a.reshape(1, -1)
    vec = pl.BlockSpec((1, d), lambda i: (0, 0))
    blk = lambda w: pl.BlockSpec((tb, w), lambda i: (i, 0))
    return pl.pallas_call(
        _merge_kernel,
        grid=(s // tb,),
        in_specs=[blk(bw), blk(bw), blk(bw), _col_spec(tb, 3 * d, COL_BR), blk(d),
                  pl.BlockSpec((3, bw, d), lambda i: (0, 0, 0)), pl.BlockSpec((d, d), lambda i: (0, 0)),
                  vec, vec, vec, vec,
                  pl.BlockSpec((d, LANES), lambda i: (0, 0)), pl.BlockSpec((1, LANES), lambda i: (0, 0))],
        out_specs=[blk(d), pl.BlockSpec((tb * ROW_SUBLANES, LANES), lambda i: (i, 0)), blk(LANES), blk(LANES),
                   pl.BlockSpec((8, LANES), lambda i: (0, 0))],
        out_shape=[jax.ShapeDtypeStruct((s, d), F32), jax.ShapeDtypeStruct((s * ROW_SUBLANES, LANES), F32),
                   jax.ShapeDtypeStruct((s, LANES), jnp.int32), jax.ShapeDtypeStruct((s, LANES), F32),
                   jax.ShapeDtypeStruct((8, LANES), F32)],
        scratch_shapes=[pltpu.VMEM((1, LANES), F32)],
        compiler_params=_params("arbitrary"),
        name="merge",
    )(ya, yb, yc, proj, x2, wbr_bf, wout_bf, row(gate_m), row(g), row(scale), row(shift), w_rt, row(b_rt))


ROUTE_E0 = MOE_GROUPS


def _route_rows(lg, seen_before):
    tb = lg.shape[0]
    lane = _iota2((tb, LANES), 1)
    big = jnp.int32(LANES)
    neg = -jnp.inf
    gl = jnp.where(lane < MOE_GROUPS, lg, neg)
    gmax = jnp.max(gl, axis=-1, keepdims=True)
    g_sel = jnp.min(jnp.where(gl == gmax, lane, big), axis=-1, keepdims=True)
    p_group = 1.0 / jnp.sum(jnp.exp(gl - gmax), axis=-1, keepdims=True)
    lo = ROUTE_E0 + EXPERTS_PER_GROUP * g_sel
    el = jnp.where(jnp.logical_and(lane >= lo, lane < lo + EXPERTS_PER_GROUP), lg, neg)
    m1 = jnp.max(el, axis=-1, keepdims=True)
    i1 = jnp.min(jnp.where(el == m1, lane, big), axis=-1, keepdims=True)
    esum = jnp.sum(jnp.exp(el - m1), axis=-1, keepdims=True)
    el2 = jnp.where(lane == i1, neg, el)
    m2 = jnp.max(el2, axis=-1, keepdims=True)
    i2 = jnp.min(jnp.where(el2 == m2, lane, big), axis=-1, keepdims=True)
    p1 = 1.0 / esum
    p2 = jnp.exp(m2 - m1) / esum
    w1 = p_group * p1 / (p1 + p2)
    w2 = p_group * p2 / (p1 + p2)

    sel1 = lane == i1
    sel2 = lane == i2
    onehot = jnp.where(jnp.logical_or(sel1, sel2), 1.0, 0.0)
    before = (_iota2((tb, tb), 0) > _iota2((tb, tb), 1)).astype(BF16)
    seen = _dot(before, _bf(onehot)) + seen_before
    r1 = jnp.sum(jnp.where(sel1, seen, 0.0), axis=-1, keepdims=True)
    r2 = jnp.sum(jnp.where(sel2, seen, 0.0), axis=-1, keepdims=True)
    ids = jnp.where(lane == 0, i1 - ROUTE_E0, jnp.where(lane == 1, i2 - ROUTE_E0,
          jnp.where(lane == 2, r1.astype(jnp.int32), jnp.where(lane == 3, r2.astype(jnp.int32), 0))))
    wts = jnp.where(lane == 0, w1, jnp.where(lane == 1, w2, 0.0))
    return ids, wts, seen_before + jnp.sum(onehot, axis=0, keepdims=True)


EXPERT_BLOCK = 512
ROW_TB = 256


def _tile_row(ref, row):
    if not isinstance(row, int):
        row = pl.multiple_of(row * ROW_SUBLANES, ROW_SUBLANES)
    else:
        row = row * ROW_SUBLANES
    return ref.at[pl.ds(row, ROW_SUBLANES), :]


def _dispatch_kernel(dest_ref, pend_ref, padded_ref, nused_ref, h_ref, xb_ref, zero_scr, sem, zero_sem):
    tb = h_ref.shape[0] // ROW_SUBLANES
    blk_rows = EXPERT_BLOCK * ROW_SUBLANES
    base = pl.program_id(0) * tb * MOE_TOP_K

    @pl.when(pl.program_id(0) == 0)
    def _():
        zero_scr[...] = jnp.zeros_like(zero_scr)

        def zero_block(block):
            start = block * blk_rows
            if not isinstance(start, int):
                start = pl.multiple_of(start, blk_rows)
            return pltpu.make_async_copy(zero_scr, xb_ref.at[pl.ds(start, blk_rows), :], zero_sem)

        def for_each_zero_block(action):
            for e in range(N_EXPERTS):
                @pl.when(padded_ref[e] > 0)
                def _():
                    action(zero_block(pend_ref[e] // EXPERT_BLOCK - 1))
            for b in range(xb_ref.shape[0] // blk_rows):
                @pl.when(b >= nused_ref[0])
                def _():
                    action(zero_block(b))

        for_each_zero_block(lambda copy: copy.start())
        for_each_zero_block(lambda copy: copy.wait())

    def row_copy(t, k, d):
        return pltpu.make_async_copy(_tile_row(h_ref, t), _tile_row(xb_ref, d), sem)

    def issue(t, carry):
        for k in range(MOE_TOP_K):
            row_copy(t, k, dest_ref[base + t * MOE_TOP_K + k]).start(priority=k % 2)
        return carry

    for t in range(tb):
        issue(t, 0)
    for k in range(MOE_TOP_K):
        pltpu.make_async_copy(h_ref, xb_ref.at[pl.ds(0, tb * ROW_SUBLANES), :], sem).wait()


def _dispatch(dest, pend, padded, n_used, h_tiles, n_slots):
    s = h_tiles.shape[0] // ROW_SUBLANES
    tb = min(ROW_TB, s)
    return pl.pallas_call(
        _dispatch_kernel,
        grid_spec=pltpu.PrefetchScalarGridSpec(
            num_scalar_prefetch=4,
            grid=(s // tb,),
            in_specs=[pl.BlockSpec((tb * ROW_SUBLANES, LANES), lambda i, *_: (i, 0))],
            out_specs=pl.BlockSpec(memory_space=pl.ANY),
            scratch_shapes=[pltpu.VMEM((EXPERT_BLOCK * ROW_SUBLANES, LANES), F32), pltpu.SemaphoreType.DMA(()),
                            pltpu.SemaphoreType.DMA(())],
        ),
        out_shape=jax.ShapeDtypeStruct((n_slots * ROW_SUBLANES, LANES), F32),
        compiler_params=_params("arbitrary"),
        name="dispatch",
    )(dest, pend, padded, n_used, h_tiles)


def _expert_kernel(be_ref, nused_ref, x_ref, wg_ref, wu_ref, wd_ref, o_ref, wg_b, wu_b, wd_b):
    b = pl.program_id(0)

    @pl.when(jnp.logical_or(b == 0, be_ref[b] != be_ref[jnp.maximum(b - 1, 0)]))
    def _():
        wg_b[...] = _bf(wg_ref[0, 0])
        wu_b[...] = _bf(wu_ref[0, 0])
        wd_b[...] = _bf(wd_ref[0, 0])

    @pl.when(b < nused_ref[0])
    def _():
        x = _bf(_load_tile_rows(x_ref, EXPERT_BLOCK))
        hid = _silu(_dot(x, wg_b[...])) * _dot(x, wu_b[...])
        _store_tile_rows(o_ref, _dot(_bf(hid), wd_b[...]))

    @pl.when(b >= nused_ref[0])
    def _():
        o_ref[...] = jnp.zeros_like(o_ref)


def _experts(block_expert, n_used, xb, layer, w_gate, w_up, w_down):
    d, ff = w_gate.shape[2:]
    blk_rows = EXPERT_BLOCK * ROW_SUBLANES
    nb = xb.shape[0] // blk_rows
    return pl.pallas_call(
        _expert_kernel,
        grid_spec=pltpu.PrefetchScalarGridSpec(
            num_scalar_prefetch=2,
            grid=(nb,),
            in_specs=[
                pl.BlockSpec((blk_rows, LANES), lambda b, be, nu: (b, 0)),
                pl.BlockSpec((1, 1, d, ff), lambda b, be, nu: (layer, be[b], 0, 0)),
                pl.BlockSpec((1, 1, d, ff), lambda b, be, nu: (layer, be[b], 0, 0)),
                pl.BlockSpec((1, 1, ff, d), lambda b, be, nu: (layer, be[b], 0, 0)),
            ],
            out_specs=pl.BlockSpec((blk_rows, LANES), lambda b, be, nu: (b, 0)),
            scratch_shapes=[pltpu.VMEM((d, ff), BF16), pltpu.VMEM((d, ff), BF16), pltpu.VMEM((ff, d), BF16)],
        ),
        out_shape=jax.ShapeDtypeStruct(xb.shape, F32),
        compiler_params=_params("arbitrary"),
        name="experts",
    )(block_expert, n_used, xb, w_gate, w_up, w_down)


def _combine_kernel(dest_ref, yb_ref, wts_ref, x_ref, gf_ref, o_ref, buf, sem):
    tb = x_ref.shape[0]
    base = pl.program_id(0) * tb * MOE_TOP_K

    def row_copy(t, k, d):
        return pltpu.make_async_copy(_tile_row(yb_ref, d), _tile_row(buf.at[k], t), sem)

    def issue(t, carry):
        for k in range(MOE_TOP_K):
            row_copy(t, k, dest_ref[base + t * MOE_TOP_K + k]).start(priority=k % 2)
        return carry

    for t in range(tb):
        issue(t, 0)
    for k in range(MOE_TOP_K):
        pltpu.make_async_copy(yb_ref.at[pl.ds(0, tb * ROW_SUBLANES), :], buf.at[k], sem).wait()
    wts = wts_ref[...]
    y = wts[:, 0:1] * _load_tile_rows(buf.at[0], tb) + wts[:, 1:2] * _load_tile_rows(buf.at[1], tb)
    o_ref[...] = x_ref[...] + gf_ref[...] * y


def _combine(dest, yb, wts, x2, gate_f):
    s, d = x2.shape
    tb = min(ROW_TB, s)
    return pl.pallas_call(
        _combine_kernel,
        grid_spec=pltpu.PrefetchScalarGridSpec(
            num_scalar_prefetch=1,
            grid=(s // tb,),
            in_specs=[pl.BlockSpec(memory_space=pl.ANY),
                      pl.BlockSpec((tb, LANES), lambda i, dest: (i, 0)),
                      pl.BlockSpec((tb, d), lambda i, dest: (i, 0)),
                      pl.BlockSpec((1, d), lambda i, dest: (0, 0))],
            out_specs=pl.BlockSpec((tb, d), lambda i, dest: (i, 0)),
            scratch_shapes=[pltpu.VMEM((MOE_TOP_K, tb * ROW_SUBLANES, LANES), F32), pltpu.SemaphoreType.DMA(())],
        ),
        out_shape=jax.ShapeDtypeStruct((s, d), F32),
        compiler_params=_params("arbitrary"),
        name="combine",
    )(dest, yb, wts, x2, gate_f.reshape(1, d))


def _moe(h, ids, wts, cnt, x2, gate_f, layer, w_gate, w_up, w_down):
    s = x2.shape[0]
    counts = cnt[0, ROUTE_E0:ROUTE_E0 + N_EXPERTS].astype(jnp.int32)
    padded = (counts + EXPERT_BLOCK - 1) // EXPERT_BLOCK * EXPERT_BLOCK
    pend = jnp.cumsum(padded)
    pstart = pend - padded
    is_expert = ids[:, 0:MOE_TOP_K, None] == jnp.arange(N_EXPERTS, dtype=jnp.int32)
    slot0 = jnp.sum(jnp.where(is_expert, pstart, 0), axis=-1)
    dest = (slot0 + ids[:, MOE_TOP_K:2 * MOE_TOP_K]).reshape(s * MOE_TOP_K)
    nb = (s * MOE_TOP_K) // EXPERT_BLOCK + N_EXPERTS
    block_start = jnp.arange(nb, dtype=jnp.int32) * EXPERT_BLOCK
    block_expert = jnp.sum((pend[None, :] <= block_start[:, None]).astype(jnp.int32), axis=1)
    block_expert = jnp.minimum(block_expert, N_EXPERTS - 1)
    n_used = (pend[-1:] // EXPERT_BLOCK).astype(jnp.int32)
    xb = _dispatch(dest, pend.astype(jnp.int32), padded.astype(jnp.int32), n_used, h, nb * EXPERT_BLOCK)
    yb = _experts(block_expert, n_used, xb, layer, w_gate, w_up, w_down)
    return _combine(dest, yb, wts, x2, gate_f)


def kernel(x, c, w_ada, b_ada, norm_mix, norm_ffn, w_in, ssm_conv_w, ssm_conv_b, ssm_dt_bias, ssm_a_log, ssm_d,
           ssm_norm, sb_q_norm, sb_k_norm, gdn_conv_w, gdn_a_log, gdn_dt_bias, gdn_norm, w_branch, w_out,
           w_group, b_group, w_router, b_router, w_gate, w_up, w_down):
    bsz, s, d = x.shape
    assert bsz == 1 and d == D_MODEL
    depth = w_in.shape[0]
    mod = _adaln_mod(c, w_ada, b_ada)
    x2 = x.reshape(s, d)
    for l in range(depth):
        shift_m, scale_m, gate_m, shift_f, scale_f, gate_f = jnp.split(mod[l], 6)
        proj, narrow = _inproj(x2, norm_mix[l], scale_m, shift_m, *_layout_w_in(w_in[l]), sb_q_norm[l], sb_k_norm[l])
        dtt = narrow[:, NCOL_DT:NCOL_DT + 8].T
        gabt = narrow[:, NCOL_GAB:NCOL_GAB + 8].T
        ya = _ssd(proj, narrow, dtt, ssm_conv_w[l], ssm_conv_b[l], ssm_dt_bias[l], ssm_a_log[l], ssm_d[l], ssm_norm[l])
        yb = _stick_breaking(proj)
        yc = _gdn(proj, narrow, gabt, gdn_conv_w[l], gdn_a_log[l], gdn_dt_bias[l], gdn_norm[l])
        pad = jnp.zeros((d, LANES - MOE_GROUPS - N_EXPERTS), F32)
        w_rt = jnp.concatenate([w_group[l], w_router[l], pad], axis=1)
        b_rt = jnp.concatenate([b_group[l], b_router[l], pad[0]])
        x2, h, ids, wts, cnt = _merge(ya, yb, yc, proj, x2, _bf(w_branch[l]), _bf(w_out[l]), gate_m,
                               norm_ffn[l], scale_f, shift_f, w_rt, b_rt)
        x2 = _moe(h, ids, wts, cnt, x2, gate_f, l, w_gate, w_up, w_down)
    return x2.reshape(bsz, s, d)
```

```python
import functools

import jax
import jax.numpy as jnp
from jax import lax
from jax.experimental import pallas as pl
from jax.experimental.pallas import tpu as pltpu

F32 = jnp.float32
BF16 = jnp.bfloat16
EPS = 1e-6

D_MODEL = 1024
SSM_HEADS = 8
SSM_HEAD_DIM = 64
SSM_INNER = 512
SSM_GROUPS = 2
SSM_STATE = 128
SSM_XBC = 1024
SSD_CHUNK = 128
SB_HEADS = 4
SB_HEAD_DIM = 128
SB_BLOCK = 128
GDN_HEADS = 4
GDN_HEAD_DIM = 128
GDN_CHUNK = 64
MOE_GROUPS = 4
EXPERTS_PER_GROUP = 8
N_EXPERTS = 32
MOE_TOP_K = 2
EXPERT_FF = 512

LANES = 128
COL_BR = 0
COL_SB = 3072
COL_GQKV = 4608
COL_XBC = 6144
COL_Z = 7168
COL_GGATE = 7680
WIDE_COLS = 8192
NCOL_DT = 0
NCOL_GAB = 128
NARROW_COLS = 256


def _col_spec(tb, width, col):
    assert col % width == 0
    return pl.BlockSpec((tb, width), lambda i: (i, col // width))

VMEM_LIMIT = 48 * 1024 * 1024
SB_SKIP_LOG = -110.0


def _bf(x):
    return x.astype(BF16)


def _dot(a, b):
    return jnp.dot(a, b, preferred_element_type=F32)


def _dot_nt(a, b):
    return lax.dot_general(a, b, (((1,), (1,)), ((), ())), preferred_element_type=F32)


def _dot_tn(a, b):
    return lax.dot_general(a, b, (((0,), (0,)), ((), ())), preferred_element_type=F32)


def _split3(x):
    hi = _bf(x)
    r = x - hi.astype(F32)
    mid = _bf(r)
    return hi, mid, _bf(r - mid.astype(F32))


def _dot_sel(sel, x):
    sel_b = _bf(sel)
    hi, mid, lo = _split3(x)
    return _dot(sel_b, hi) + (_dot(sel_b, mid) + _dot(sel_b, lo))


def _dot_sel_r(x, sel):
    sel_b = _bf(sel)
    hi, mid, lo = _split3(x)
    return _dot(hi, sel_b) + (_dot(mid, sel_b) + _dot(lo, sel_b))


def _split2(x):
    hi = _bf(x)
    lo = _bf(x - hi.astype(F32))
    return hi, lo


def _dot3(a, b):
    ah, al = _split2(a)
    bh, bl = _split2(b)
    return _dot(ah, bh) + (_dot(ah, bl) + _dot(al, bh))


def _silu(x):
    return x * jax.nn.sigmoid(x)


def _softplus(x):
    return jnp.maximum(x, 0.0) + jnp.log1p(jnp.exp(-jnp.abs(x)))


def _iota2(shape, dim):
    return lax.broadcasted_iota(jnp.int32, shape, dim)


ROW_SUBLANES = D_MODEL // LANES
assert ROW_SUBLANES == 8


def _store_tile_rows(ref, x):
    n = x.shape[0]
    for sl in range(ROW_SUBLANES):
        ref[pl.ds(sl, n, stride=ROW_SUBLANES), :] = x[:, sl * LANES:(sl + 1) * LANES]


def _load_tile_rows(ref, n):
    return jnp.concatenate([ref[pl.ds(sl, n, stride=ROW_SUBLANES), :] for sl in range(ROW_SUBLANES)], axis=-1)


def _params(*sem):
    return pltpu.CompilerParams(dimension_semantics=sem, vmem_limit_bytes=VMEM_LIMIT)


def _mod_kernel(c_ref, w_ref, b_ref, o_ref):
    c = _silu(c_ref[...])
    o_ref[0] = _dot3(c, w_ref[0]) + b_ref[0]


def _adaln_mod(c, w_ada, b_ada):
    depth, d, cols = w_ada.shape
    tn = 1024
    c8 = jnp.broadcast_to(c, (8, d))
    out = pl.pallas_call(
        _mod_kernel,
        grid=(depth, cols // tn),
        in_specs=[
            pl.BlockSpec((8, d), lambda l, j: (0, 0)),
            pl.BlockSpec((1, d, tn), lambda l, j: (l, 0, j)),
            pl.BlockSpec((1, 1, tn), lambda l, j: (l, 0, j)),
        ],
        out_specs=pl.BlockSpec((1, 8, tn), lambda l, j: (l, 0, j)),
        out_shape=jax.ShapeDtypeStruct((depth, 8, cols), F32),
        compiler_params=_params("arbitrary", "arbitrary"),
        name="adaln_mod",
    )(c8, w_ada, b_ada.reshape(depth, 1, cols))
    return out[:, 0, :]


def _norm_mod(x, g, scale, shift):
    y = x * lax.rsqrt(jnp.mean(x * x, axis=-1, keepdims=True) + EPS)
    return (y * g) * (1.0 + scale) + shift


INPROJ_TN = 1024
assert COL_SB % INPROJ_TN == 0 and 2 * SB_HEADS * SB_HEAD_DIM == INPROJ_TN


def _inproj_kernel(x_ref, g_ref, sc_ref, sh_ref, w_ref, wn_ref, qkg_ref, qks_ref, o_ref, on_ref, h_scr):
    j = pl.program_id(1)

    @pl.when(j == 0)
    def _():
        h = _bf(_norm_mod(x_ref[...], g_ref[...], sc_ref[...], sh_ref[...]))
        h_scr[...] = h
        on_ref[...] = _dot(h, wn_ref[...])

    @pl.when(j != COL_SB // INPROJ_TN)
    def _():
        o_ref[...] = _bf(_dot(h_scr[...], w_ref[...]))

    @pl.when(j == COL_SB // INPROJ_TN)
    def _():
        acc = _dot(h_scr[...], w_ref[...])
        dh = SB_HEAD_DIM
        for n in range(INPROJ_TN // dh):
            cols = slice(n * dh, (n + 1) * dh)
            o_ref[:, cols] = _bf(_head_rms(acc[:, cols], qkg_ref[:, cols]) * qks_ref[:, cols])


def _inproj(x2, g, scale, shift, w_wide, w_narrow, q_g, k_g):
    s, d = x2.shape
    tm = min(1024, s)
    tn = INPROJ_TN
    row = lambda a: a.reshape(1, d)
    vec = pl.BlockSpec((1, d), lambda i, j: (0, 0))
    qk_gain = jnp.concatenate([jnp.tile(q_g, SB_HEADS), jnp.tile(k_g, SB_HEADS)])
    qk_scale = jnp.concatenate([jnp.full((tn // 2,), SB_HEAD_DIM ** -0.5, F32), jnp.ones((tn // 2,), F32)])
    return pl.pallas_call(
        _inproj_kernel,
        grid=(s // tm, WIDE_COLS // tn),
        in_specs=[pl.BlockSpec((tm, d), lambda i, j: (i, 0)), vec, vec, vec,
                  pl.BlockSpec((d, tn), lambda i, j: (0, j)),
                  pl.BlockSpec((d, NARROW_COLS), lambda i, j: (0, 0)),
                  pl.BlockSpec((1, tn), lambda i, j: (0, 0)), pl.BlockSpec((1, tn), lambda i, j: (0, 0))],
        out_specs=[pl.BlockSpec((tm, tn), lambda i, j: (i, j)),
                   pl.BlockSpec((tm, NARROW_COLS), lambda i, j: (i, 0))],
        out_shape=[jax.ShapeDtypeStruct((s, WIDE_COLS), BF16), jax.ShapeDtypeStruct((s, NARROW_COLS), F32)],
        scratch_shapes=[pltpu.VMEM((tm, d), BF16)],
        compiler_params=_params("arbitrary", "arbitrary"),
        name="inproj",
    )(x2, row(g), row(scale), row(shift), w_wide, w_narrow, qk_gain.reshape(1, tn), qk_scale.reshape(1, tn))


def _causal_conv4(x, ext_scr, w_ref):
    tb = x.shape[0]
    ext_scr[8:8 + tb, :] = x
    y = x * w_ref[3:4, :]
    for k in (1, 2, 3):
        y = y + ext_scr[8 - k:8 - k + tb, :] * w_ref[3 - k:4 - k, :]
    ext_scr[0:8, :] = x[tb - 8:tb]
    return y


def _ssd_kernel(z_ref, xbc_ref, dt_ref, dtt_ref, cw_ref, cb_ref, dtb_ref, dtbt_ref, al_ref, alt_ref,
                dsk_ref, ng_ref, o_ref, ext_scr, act_scr, y_scr, st_scr):
    tb = xbc_ref.shape[0]
    L = SSD_CHUNK
    P = SSM_HEAD_DIM

    @pl.when(pl.program_id(0) == 0)
    def _():
        ext_scr[0:8, :] = jnp.zeros((8, ext_scr.shape[1]), F32)
        st_scr[...] = jnp.zeros_like(st_scr)

    act_scr[...] = _silu(_causal_conv4(xbc_ref[...].astype(F32), ext_scr, cw_ref) + cb_ref[...])

    ri = _iota2((L, L), 0)
    ci = _iota2((L, L), 1)
    tril = (ri >= ci).astype(F32)
    causal = ri >= ci
    a_col = -jnp.exp(al_ref[...])
    a_row = -jnp.exp(alt_ref[...])
    expand = (_iota2((LANES, SSM_INNER), 1) // P == _iota2((LANES, SSM_INNER), 0)).astype(F32)
    hpg = SSM_HEADS // SSM_GROUPS
    gw = hpg * P

    for c in range(tb // L):
        rows = slice(c * L, (c + 1) * L)
        dt = _softplus(dt_ref[rows, :] + dtb_ref[...])
        dtt = _softplus(dtt_ref[:, rows] + dtbt_ref[...])
        acum = _dot_sel(tril, dt * a_col)
        acum_t = _dot_sel_r(dtt * a_row, tril.T)
        per_head = jnp.concatenate([dt, jnp.exp(acum[L - 1:L, :] - acum), jnp.exp(acum)], axis=0)
        per_chan = _dot_sel_r(per_head, expand)
        dt_e, to_end_e, from_start_e = per_chan[:L], per_chan[L:2 * L], per_chan[2 * L:]
        act = act_scr[rows, :]
        xs = act[:, :SSM_INNER]
        xdt = xs * dt_e
        xdt_b = _bf(xdt)
        xdt_end_b = _bf(xdt * to_end_e)
        off_c = SSM_INNER + SSM_GROUPS * SSM_STATE
        y_in, y_st = [], []
        for g in range(SSM_GROUPS):
            bm = _bf(act[:, SSM_INNER + g * SSM_STATE:SSM_INNER + (g + 1) * SSM_STATE])
            cm = _bf(act[:, off_c + g * SSM_STATE:off_c + (g + 1) * SSM_STATE])
            gmat = _dot_nt(cm, bm)
            st = st_scr[g]
            y_st.append(_dot_nt(cm, _bf(st)))
            new = _dot_tn(xdt_end_b[:, g * gw:(g + 1) * gw], bm)
            for hh in range(hpg):
                h = g * hpg + hh
                seg = jnp.exp(jnp.where(causal, acum[:, h:h + 1] - acum_t[h:h + 1, :], -jnp.inf))
                y_in.append(_dot(_bf(gmat * seg), xdt_b[:, h * P:(h + 1) * P]))
                sl = slice(hh * P, (hh + 1) * P)
                st_scr[g, sl, :] = st[sl] * jnp.exp(acum_t[h:h + 1, L - 1:L]) + new[sl]
        y_scr[rows, :] = (jnp.concatenate(y_in, axis=-1) + jnp.concatenate(y_st, axis=-1) * from_start_e
                          + xs * dsk_ref[...])

    y = y_scr[...] * _silu(z_ref[...].astype(F32))
    gsz = SSM_INNER // SSM_GROUPS
    outs = []
    for g in range(SSM_GROUPS):
        yg = y[:, g * gsz:(g + 1) * gsz]
        yn = yg * lax.rsqrt(jnp.mean(yg * yg, axis=-1, keepdims=True) + EPS)
        outs.append(yn * ng_ref[:, g * gsz:(g + 1) * gsz])
    o_ref[...] = jnp.concatenate(outs, axis=-1)


def _ssd(proj, narrow, dtt, conv_w, conv_b, dt_bias, a_log, d_skip, norm_g):
    s = proj.shape[0]
    tb = min(512, s)
    pad8 = lambda v: jnp.pad(v, (0, LANES - v.shape[0])).reshape(1, LANES)
    colv = lambda v: v.reshape(SSM_HEADS, 1)
    full = lambda shape: pl.BlockSpec(shape, lambda i: (0,) * len(shape))
    return pl.pallas_call(
        _ssd_kernel,
        grid=(s // tb,),
        in_specs=[
            _col_spec(tb, SSM_INNER, COL_Z),
            _col_spec(tb, SSM_XBC, COL_XBC),
            _col_spec(tb, LANES, NCOL_DT),
            pl.BlockSpec((SSM_HEADS, tb), lambda i: (0, i)),
            full((4, SSM_XBC)), full((1, SSM_XBC)), full((1, LANES)), full((SSM_HEADS, 1)),
            full((1, LANES)), full((SSM_HEADS, 1)), full((1, SSM_INNER)), full((1, SSM_INNER)),
        ],
        out_specs=pl.BlockSpec((tb, SSM_INNER), lambda i: (i, 0)),
        out_shape=jax.ShapeDtypeStruct((s, SSM_INNER), F32),
        scratch_shapes=[
            pltpu.VMEM((tb + 8, SSM_XBC), F32),
            pltpu.VMEM((tb, SSM_XBC), F32),
            pltpu.VMEM((tb, SSM_INNER), F32),
            pltpu.VMEM((SSM_GROUPS, SSM_HEADS // SSM_GROUPS * SSM_HEAD_DIM, SSM_STATE), F32),
        ],
        compiler_params=_params("arbitrary"),
        name="ssd",
    )(proj, proj, narrow, dtt, conv_w, conv_b.reshape(1, -1), pad8(dt_bias), colv(dt_bias),
      pad8(a_log), colv(a_log), jnp.repeat(d_skip, SSM_HEAD_DIM).reshape(1, -1), norm_g.reshape(1, -1))


def _layout_w_in(w):
    d = w.shape[0]
    z, xbc, dt, sb, gqkv, gab, ggate, br = jnp.split(w, [512, 1536, 1544, 3080, 4616, 4624, 5136], axis=1)
    pad = jnp.zeros((d, LANES - 8), w.dtype)
    wide = jnp.concatenate([br, sb, gqkv, xbc, z, ggate], axis=1).astype(BF16)
    narrow = jnp.concatenate([dt, pad, gab, pad], axis=1).astype(BF16)
    return wide, narrow


def _head_rms(x, g):
    return (x * lax.rsqrt(jnp.mean(x * x, axis=-1, keepdims=True) + EPS)) * g


def _sb_blocks(qs, kns, vs, accs, suffix, masked):
    blk = qs[0].shape[0]
    strict = _iota2((blk, blk), 1) < _iota2((blk, blk), 0)
    zs = [_dot_nt(q, kn) for q, kn in zip(qs, kns)]
    sps = [jnp.maximum(z, 0.0) + jnp.log(1.0 + jnp.exp(-jnp.abs(z))) for z in zs]
    log_keeps = [jnp.where(strict, -sp, 0.0) if masked else -sp for sp in sps]
    splits = [_split2(lk) for lk in log_keeps]
    afters = [(_dot(hi, suffix) + _dot(lo, suffix)) + acc for (hi, lo), acc in zip(splits, accs)]
    atts = [jnp.exp((z - sp) + after) for z, sp, after in zip(zs, sps, afters)]
    if masked:
        atts = [jnp.where(strict, att, 0.0) for att in atts]
    outs = [_dot(_bf(att), v) for att, v in zip(atts, vs)]
    return outs, [jnp.sum(lk, axis=-1, keepdims=True) for lk in log_keeps]


SB_QBLOCKS = 2


def _sb_kernel(q_ref, k_ref, v_ref, o_ref, acc_scr):
    blk = SB_BLOCK
    dh = SB_HEAD_DIM
    first = pl.program_id(0) * SB_QBLOCKS
    pairs = [(b, h) for b in range(SB_QBLOCKS) for h in range(SB_HEADS)]
    suffix = (_iota2((blk, blk), 0) > _iota2((blk, blk), 1)).astype(BF16)
    qs = [q_ref[b * blk:(b + 1) * blk, h * dh:(h + 1) * dh] for b, h in pairs]

    def load_kv(offset):
        rows = [pl.ds(pl.multiple_of(jnp.maximum(first + b - offset, 0) * blk, blk), blk) for b in range(SB_QBLOCKS)]
        return ([k_ref[rows[b], h * dh:(h + 1) * dh] for b, h in pairs],
                [v_ref[rows[b], h * dh:(h + 1) * dh] for b, h in pairs])

    def live(accs):
        top = functools.reduce(jnp.maximum, accs)
        return (jnp.max(top) > SB_SKIP_LOG).astype(jnp.int32)

    def out_slice(n):
        b, h = pairs[n]
        return (slice(b * blk, (b + 1) * blk), slice(h * dh, (h + 1) * dh))

    kns, vs = load_kv(0)
    outs, sums = _sb_blocks(qs, kns, vs, [jnp.zeros((blk, 1), F32)] * len(pairs), suffix, True)
    for n in range(len(pairs)):
        o_ref[out_slice(n)] = outs[n]
        acc_scr[n] = sums[n]

    def cond(carry):
        offset, alive = carry
        return jnp.logical_and(offset <= first + SB_QBLOCKS - 1, alive > 0)

    def body(carry):
        offset, _ = carry
        kns, vs = load_kv(offset)
        accs = [acc_scr[n] for n in range(len(pairs))]
        outs, sums = _sb_blocks(qs, kns, vs, accs, suffix, False)
        valid = [jnp.where(first + b - offset >= 0, 1.0, 0.0) for b in range(SB_QBLOCKS)]
        accs = [acc + rs * valid[b] for acc, rs, (b, _) in zip(accs, sums, pairs)]
        for n in range(len(pairs)):
            o_ref[out_slice(n)] += outs[n] * valid[pairs[n][0]]
            acc_scr[n] = accs[n]
        return offset + 1, live(accs)

    lax.while_loop(cond, body, (jnp.int32(1), live(sums)))


def _stick_breaking(proj):
    s = proj.shape[0]
    tq = SB_QBLOCKS * SB_BLOCK
    dh = SB_HEAD_DIM
    width = SB_HEADS * dh
    resident = lambda col: pl.BlockSpec((s, width), lambda i: (0, col // width), pipeline_mode=pl.Buffered(1))
    return pl.pallas_call(
        _sb_kernel,
        grid=(s // tq,),
        in_specs=[_col_spec(tq, width, COL_SB), resident(COL_SB + width), resident(COL_SB + 2 * width)],
        out_specs=pl.BlockSpec((tq, width), lambda i: (i, 0)),
        out_shape=jax.ShapeDtypeStruct((s, width), F32),
        scratch_shapes=[pltpu.VMEM((SB_QBLOCKS * SB_HEADS, SB_BLOCK, 1), F32)],
        compiler_params=_params("arbitrary"),
        name="stick_breaking",
    )(proj, proj, proj)


GDN_SUB = 128


def _dot3_nt(a, b):
    ah, al = _split2(a)
    bh, bl = _split2(b)
    return _dot_nt(ah, bh) + (_dot_nt(ah, bl) + _dot_nt(al, bh))


def _chunk_lower_inverses(ms, chunk):
    n = ms[0].shape[0]
    eye = (_iota2((n, n), 0) == _iota2((n, n), 1)).astype(F32)
    ps = [-m for m in ms]
    invs = [eye + p for p in ps]
    p_parts = [_split2(p) for p in ps]
    for j in range((chunk - 1).bit_length() - 1):
        if j == 0:
            ps = [_dot(ph, ph) + (_dot(ph, pl_) + _dot(pl_, ph)) for ph, pl_ in p_parts]
        else:
            ps = [_dot(ph, ph) for ph, _ in p_parts]
        p_parts = [_split2(p) for p in ps]
        inv_parts = [_split2(inv) for inv in invs]
        invs = [inv + (_dot(ih, ph) + (_dot(ih, pl_) + _dot(il, ph)))
                for inv, (ih, il), (ph, pl_) in zip(invs, inv_parts, p_parts)]
    return invs


def _gdn_kernel(qkv_ref, gab_ref, gabt_ref, gate_ref, cw_ref, al_ref, alt_ref, dtb_ref, dtbt_ref, ng_ref,
                o_ref, ext_scr, act_scr, st_scr):
    tb = qkv_ref.shape[0]
    C = GDN_CHUNK
    dh = GDN_HEAD_DIM
    inner = GDN_HEADS * dh
    heads = range(GDN_HEADS)

    @pl.when(pl.program_id(0) == 0)
    def _():
        ext_scr[0:8, :] = jnp.zeros((8, ext_scr.shape[1]), F32)
        st_scr[...] = jnp.zeros_like(st_scr)

    act_scr[...] = _silu(_causal_conv4(qkv_ref[...].astype(F32), ext_scr, cw_ref))

    sub = min(GDN_SUB, tb)
    cps = sub // C
    ri = _iota2((sub, sub), 0)
    ci = _iota2((sub, sub), 1)
    same = (ri // C) == (ci // C)
    incl = jnp.logical_and(same, ri >= ci)
    strict = jnp.logical_and(same, ri > ci)
    tril = incl.astype(F32)
    al_col = -jnp.exp(al_ref[...])
    al_row = -jnp.exp(alt_ref[...])
    units = [(b, h) for b in range(tb // sub) for h in heads]

    gc_cols, gc_rows, betas = [], [], []
    for b in range(tb // sub):
        rows = slice(b * sub, (b + 1) * sub)
        gab = gab_ref[rows, :]
        g_col = al_col * _softplus(gab + dtb_ref[...])
        g_row = al_row * _softplus(gabt_ref[:, rows] + dtbt_ref[...])
        gc_cols.append(_dot_sel(tril, g_col))
        gc_rows.append(_dot_sel_r(g_row, tril.T))
        betas.append(jax.nn.sigmoid(gab))

    qs, ks, kbs, gcs, decays, rhss = [], [], [], [], [], []
    for b, h in units:
        rows = slice(b * sub, (b + 1) * sub)
        q = act_scr[rows, h * dh:(h + 1) * dh]
        k = act_scr[rows, inner + h * dh:inner + (h + 1) * dh]
        v = act_scr[rows, 2 * inner + h * dh:2 * inner + (h + 1) * dh]
        q = q * lax.rsqrt(jnp.sum(q * q, axis=-1, keepdims=True) + EPS) * (dh ** -0.5)
        k = k * lax.rsqrt(jnp.sum(k * k, axis=-1, keepdims=True) + EPS)
        beta = betas[b][:, GDN_HEADS + h:GDN_HEADS + h + 1]
        gc = gc_cols[b][:, h:h + 1]
        kb = k * beta
        qs.append(q)
        ks.append(k)
        kbs.append(kb)
        gcs.append(gc)
        decays.append(jnp.exp(jnp.where(incl, gc - gc_rows[b][h:h + 1, :], -jnp.inf)))
        rhss.append(jnp.concatenate([v * beta, kb * jnp.exp(gc)], axis=-1))

    n_units = range(len(units))
    ms = [jnp.where(strict, _dot3_nt(kbs[n], ks[n]) * decays[n], 0.0) for n in n_units]
    invs = _chunk_lower_inverses(ms, C)
    sols = [_dot3(invs[n], rhss[n]) for n in n_units]
    attns = [_bf(jnp.where(incl, _dot_nt(_bf(qs[n]), _bf(ks[n])) * decays[n], 0.0)) for n in n_units]
    q_decs = [_bf(qs[n] * jnp.exp(gcs[n])) for n in n_units]

    sts = [st_scr[h] for h in heads]
    v_news = [[] for _ in n_units]
    o_inters = [[] for _ in n_units]
    for c in range(tb // C):
        b = c // cps
        rows = slice((c % cps) * C, (c % cps + 1) * C)
        for h in heads:
            n = b * GDN_HEADS + h
            st_b = _bf(sts[h])
            g_last = gc_rows[b][h:h + 1, rows.stop - 1:rows.stop]
            v_new = sols[n][rows, :dh] - _dot(_bf(sols[n][rows, dh:]), st_b)
            v_new_b = _bf(v_new)
            o_inters[n].append(_dot(q_decs[n][rows, :], st_b))
            k_dec = ks[n][rows, :] * jnp.exp(g_last - gcs[n][rows, :])
            sts[h] = sts[h] * jnp.exp(g_last) + _dot_tn(_bf(k_dec), v_new_b)
            v_news[n].append(v_new_b)
    for h in heads:
        st_scr[h] = sts[h]
    for n, (b, h) in enumerate(units):
        rows = slice(b * sub, (b + 1) * sub)
        o = jnp.concatenate(o_inters[n], axis=0) + _dot(attns[n], jnp.concatenate(v_news[n], axis=0))
        o = _head_rms(o, ng_ref[...]) * _silu(gate_ref[rows, h * dh:(h + 1) * dh].astype(F32))
        o_ref[rows, h * dh:(h + 1) * dh] = o


def _gdn(proj, narrow, gabt, conv_w, a_log, dt_bias, norm_g):
    s = proj.shape[0]
    tb = min(512, s)
    inner = GDN_HEADS * GDN_HEAD_DIM
    pad_lane = lambda v: jnp.pad(v, (0, LANES - v.shape[0])).reshape(1, LANES)
    pad_col = lambda v: jnp.pad(v, (0, 8 - v.shape[0])).reshape(8, 1)
    full = lambda shape: pl.BlockSpec(shape, lambda i: (0,) * len(shape))
    return pl.pallas_call(
        _gdn_kernel,
        grid=(s // tb,),
        in_specs=[
            _col_spec(tb, 3 * inner, COL_GQKV),
            _col_spec(tb, LANES, NCOL_GAB),
            pl.BlockSpec((8, tb), lambda i: (0, i)),
            _col_spec(tb, inner, COL_GGATE),
            full((4, 3 * inner)), full((1, LANES)), full((8, 1)), full((1, LANES)), full((8, 1)),
            full((1, GDN_HEAD_DIM)),
        ],
        out_specs=pl.BlockSpec((tb, inner), lambda i: (i, 0)),
        out_shape=jax.ShapeDtypeStruct((s, inner), F32),
        scratch_shapes=[
            pltpu.VMEM((tb + 8, 3 * inner), F32),
            pltpu.VMEM((tb, 3 * inner), F32),
            pltpu.VMEM((GDN_HEADS, GDN_HEAD_DIM, GDN_HEAD_DIM), F32),
        ],
        compiler_params=_params("arbitrary"),
        name="gdn",
    )(proj, narrow, gabt, proj, conv_w, pad_lane(a_log), pad_col(a_log), pad_lane(dt_bias), pad_col(dt_bias),
      norm_g.reshape(1, -1))


def _merge_kernel(ya_ref, yb_ref, yc_ref, br_ref, x_ref, wbr_ref, wout_ref, gm_ref, g_ref, sc_ref, sh_ref,
                  wrt_ref, brt_ref, xo_ref, h_ref, ids_ref, wts_ref, cnt_ref, seen_scr):
    d = x_ref.shape[1]

    @pl.when(pl.program_id(0) == 0)
    def _():
        seen_scr[...] = jnp.zeros_like(seen_scr)

    merged = None
    for i, y_ref in enumerate((ya_ref, yb_ref, yc_ref)):
        gate = jax.nn.sigmoid(br_ref[:, i * d:(i + 1) * d].astype(F32))
        term = gate * _dot(_bf(y_ref[...]), wbr_ref[i])
        merged = term if merged is None else merged + term
    x_new = x_ref[...] + gm_ref[...] * _dot(_bf(merged), wout_ref[...])
    xo_ref[...] = x_new
    h = _norm_mod(x_new, g_ref[...], sc_ref[...], sh_ref[...])
    _store_tile_rows(h_ref, h)
    ids, wts, seen = _route_rows(_dot3(h, wrt_ref[...]) + brt_ref[...], seen_scr[...])
    ids_ref[...] = ids
    wts_ref[...] = wts
    seen_scr[...] = seen
    cnt_ref[...] = jnp.broadcast_to(seen, cnt_ref.shape)


def _merge(ya, yb, yc, proj, x2, wbr_bf, wout_bf, gate_m, g, scale, shift, w_rt, b_rt):
    s, d = x2.shape
    tb = min(512, s)
    bw = ya.shape[1]
    row = lambda a: a.reshape(1, -1)
    vec = pl.BlockSpec((1, d), lambda i: (0, 0))
    blk = lambda w: pl.BlockSpec((tb, w), lambda i: (i, 0))
    return pl.pallas_call(
        _merge_kernel,
        grid=(s // tb,),
        in_specs=[blk(bw), blk(bw), blk(bw), _col_spec(tb, 3 * d, COL_BR), blk(d),
                  pl.BlockSpec((3, bw, d), lambda i: (0, 0, 0)), pl.BlockSpec((d, d), lambda i: (0, 0)),
                  vec, vec, vec, vec,
                  pl.BlockSpec((d, LANES), lambda i: (0, 0)), pl.BlockSpec((1, LANES), lambda i: (0, 0))],
        out_specs=[blk(d), pl.BlockSpec((tb * ROW_SUBLANES, LANES), lambda i: (i, 0)), blk(LANES), blk(LANES),
                   pl.BlockSpec((8, LANES), lambda i: (0, 0))],
        out_shape=[jax.ShapeDtypeStruct((s, d), F32), jax.ShapeDtypeStruct((s * ROW_SUBLANES, LANES), F32),
                   jax.ShapeDtypeStruct((s, LANES), jnp.int32), jax.ShapeDtypeStruct((s, LANES), F32),
                   jax.ShapeDtypeStruct((8, LANES), F32)],
        scratch_shapes=[pltpu.VMEM((1, LANES), F32)],
        compiler_params=_params("arbitrary"),
        name="merge",
    )(ya, yb, yc, proj, x2, wbr_bf, wout_bf, row(gate_m), row(g), row(scale), row(shift), w_rt, row(b_rt))


ROUTE_E0 = MOE_GROUPS


def _route_rows(lg, seen_before):
    tb = lg.shape[0]
    lane = _iota2((tb, LANES), 1)
    big = jnp.int32(LANES)
    neg = -jnp.inf
    gl = jnp.where(lane < MOE_GROUPS, lg, neg)
    gmax = jnp.max(gl, axis=-1, keepdims=True)
    g_sel = jnp.min(jnp.where(gl == gmax, lane, big), axis=-1, keepdims=True)
    p_group = 1.0 / jnp.sum(jnp.exp(gl - gmax), axis=-1, keepdims=True)
    lo = ROUTE_E0 + EXPERTS_PER_GROUP * g_sel
    el = jnp.where(jnp.logical_and(lane >= lo, lane < lo + EXPERTS_PER_GROUP), lg, neg)
    m1 = jnp.max(el, axis=-1, keepdims=True)
    i1 = jnp.min(jnp.where(el == m1, lane, big), axis=-1, keepdims=True)
    esum = jnp.sum(jnp.exp(el - m1), axis=-1, keepdims=True)
    el2 = jnp.where(lane == i1, neg, el)
    m2 = jnp.max(el2, axis=-1, keepdims=True)
    i2 = jnp.min(jnp.where(el2 == m2, lane, big), axis=-1, keepdims=True)
    p1 = 1.0 / esum
    p2 = jnp.exp(m2 - m1) / esum
    w1 = p_group * p1 / (p1 + p2)
    w2 = p_group * p2 / (p1 + p2)

    sel1 = lane == i1
    sel2 = lane == i2
    onehot = jnp.where(jnp.logical_or(sel1, sel2), 1.0, 0.0)
    before = (_iota2((tb, tb), 0) > _iota2((tb, tb), 1)).astype(BF16)
    seen = _dot(before, _bf(onehot)) + seen_before
    r1 = jnp.sum(jnp.where(sel1, seen, 0.0), axis=-1, keepdims=True)
    r2 = jnp.sum(jnp.where(sel2, seen, 0.0), axis=-1, keepdims=True)
    ids = jnp.where(lane == 0, i1 - ROUTE_E0, jnp.where(lane == 1, i2 - ROUTE_E0,
          jnp.where(lane == 2, r1.astype(jnp.int32), jnp.where(lane == 3, r2.astype(jnp.int32), 0))))
    wts = jnp.where(lane == 0, w1, jnp.where(lane == 1, w2, 0.0))
    return ids, wts, seen_before + jnp.sum(onehot, axis=0, keepdims=True)


EXPERT_BLOCK = 512
ROW_TB = 256


def _tile_row(ref, row):
    if not isinstance(row, int):
        row = pl.multiple_of(row * ROW_SUBLANES, ROW_SUBLANES)
    else:
        row = row * ROW_SUBLANES
    return ref.at[pl.ds(row, ROW_SUBLANES), :]


def _dispatch_kernel(dest_ref, pend_ref, padded_ref, nused_ref, h_ref, xb_ref, zero_scr, sem, zero_sem):
    tb = h_ref.shape[0] // ROW_SUBLANES
    blk_rows = EXPERT_BLOCK * ROW_SUBLANES
    base = pl.program_id(0) * tb * MOE_TOP_K

    @pl.when(pl.program_id(0) == 0)
    def _():
        zero_scr[...] = jnp.zeros_like(zero_scr)

        def zero_block(block):
            start = block * blk_rows
            if not isinstance(start, int):
                start = pl.multiple_of(start, blk_rows)
            return pltpu.make_async_copy(zero_scr, xb_ref.at[pl.ds(start, blk_rows), :], zero_sem)

        def for_each_zero_block(action):
            for e in range(N_EXPERTS):
                @pl.when(padded_ref[e] > 0)
                def _():
                    action(zero_block(pend_ref[e] // EXPERT_BLOCK - 1))
            for b in range(xb_ref.shape[0] // blk_rows):
                @pl.when(b >= nused_ref[0])
                def _():
                    action(zero_block(b))

        for_each_zero_block(lambda copy: copy.start())
        for_each_zero_block(lambda copy: copy.wait())

    def row_copy(t, k, d):
        return pltpu.make_async_copy(_tile_row(h_ref, t), _tile_row(xb_ref, d), sem)

    def issue(t, carry):
        for k in range(MOE_TOP_K):
            row_copy(t, k, dest_ref[base + t * MOE_TOP_K + k]).start(priority=k % 2)
        return carry

    for t in range(tb):
        issue(t, 0)
    for k in range(MOE_TOP_K):
        pltpu.make_async_copy(h_ref, xb_ref.at[pl.ds(0, tb * ROW_SUBLANES), :], sem).wait()


def _dispatch(dest, pend, padded, n_used, h_tiles, n_slots):
    s = h_tiles.shape[0] // ROW_SUBLANES
    tb = min(ROW_TB, s)
    return pl.pallas_call(
        _dispatch_kernel,
        grid_spec=pltpu.PrefetchScalarGridSpec(
            num_scalar_prefetch=4,
            grid=(s // tb,),
            in_specs=[pl.BlockSpec((tb * ROW_SUBLANES, LANES), lambda i, *_: (i, 0))],
            out_specs=pl.BlockSpec(memory_space=pl.ANY),
            scratch_shapes=[pltpu.VMEM((EXPERT_BLOCK * ROW_SUBLANES, LANES), F32), pltpu.SemaphoreType.DMA(()),
                            pltpu.SemaphoreType.DMA(())],
        ),
        out_shape=jax.ShapeDtypeStruct((n_slots * ROW_SUBLANES, LANES), F32),
        compiler_params=_params("arbitrary"),
        name="dispatch",
    )(dest, pend, padded, n_used, h_tiles)


def _expert_kernel(be_ref, nused_ref, x_ref, wg_ref, wu_ref, wd_ref, o_ref, wg_b, wu_b, wd_b):
    b = pl.program_id(0)

    @pl.when(jnp.logical_or(b == 0, be_ref[b] != be_ref[jnp.maximum(b - 1, 0)]))
    def _():
        wg_b[...] = _bf(wg_ref[0, 0])
        wu_b[...] = _bf(wu_ref[0, 0])
        wd_b[...] = _bf(wd_ref[0, 0])

    @pl.when(b < nused_ref[0])
    def _():
        x = _bf(_load_tile_rows(x_ref, EXPERT_BLOCK))
        hid = _silu(_dot(x, wg_b[...])) * _dot(x, wu_b[...])
        _store_tile_rows(o_ref, _dot(_bf(hid), wd_b[...]))

    @pl.when(b >= nused_ref[0])
    def _():
        o_ref[...] = jnp.zeros_like(o_ref)


def _experts(block_expert, n_used, xb, layer, w_gate, w_up, w_down):
    d, ff = w_gate.shape[2:]
    blk_rows = EXPERT_BLOCK * ROW_SUBLANES
    nb = xb.shape[0] // blk_rows
    return pl.pallas_call(
        _expert_kernel,
        grid_spec=pltpu.PrefetchScalarGridSpec(
            num_scalar_prefetch=2,
            grid=(nb,),
            in_specs=[
                pl.BlockSpec((blk_rows, LANES), lambda b, be, nu: (b, 0)),
                pl.BlockSpec((1, 1, d, ff), lambda b, be, nu: (layer, be[b], 0, 0)),
                pl.BlockSpec((1, 1, d, ff), lambda b, be, nu: (layer, be[b], 0, 0)),
                pl.BlockSpec((1, 1, ff, d), lambda b, be, nu: (layer, be[b], 0, 0)),
            ],
            out_specs=pl.BlockSpec((blk_rows, LANES), lambda b, be, nu: (b, 0)),
            scratch_shapes=[pltpu.VMEM((d, ff), BF16), pltpu.VMEM((d, ff), BF16), pltpu.VMEM((ff, d), BF16)],
        ),
        out_shape=jax.ShapeDtypeStruct(xb.shape, F32),
        compiler_params=_params("arbitrary"),
        name="experts",
    )(block_expert, n_used, xb, w_gate, w_up, w_down)


def _combine_kernel(dest_ref, yb_ref, wts_ref, x_ref, gf_ref, o_ref, buf, sem):
    tb = x_ref.shape[0]
    base = pl.program_id(0) * tb * MOE_TOP_K

    def row_copy(t, k, d):
        return pltpu.make_async_copy(_tile_row(yb_ref, d), _tile_row(buf.at[k], t), sem)

    def issue(t, carry):
        for k in range(MOE_TOP_K):
            row_copy(t, k, dest_ref[base + t * MOE_TOP_K + k]).start(priority=k % 2)
        return carry

    for t in range(tb):
        issue(t, 0)
    for k in range(MOE_TOP_K):
        pltpu.make_async_copy(yb_ref.at[pl.ds(0, tb * ROW_SUBLANES), :], buf.at[k], sem).wait()
    wts = wts_ref[...]
    y = wts[:, 0:1] * _load_tile_rows(buf.at[0], tb) + wts[:, 1:2] * _load_tile_rows(buf.at[1], tb)
    o_ref[...] = x_ref[...] + gf_ref[...] * y


def _combine(dest, yb, wts, x2, gate_f):
    s, d = x2.shape
    tb = min(ROW_TB, s)
    return pl.pallas_call(
        _combine_kernel,
        grid_spec=pltpu.PrefetchScalarGridSpec(
            num_scalar_prefetch=1,
            grid=(s // tb,),
            in_specs=[pl.BlockSpec(memory_space=pl.ANY),
                      pl.BlockSpec((tb, LANES), lambda i, dest: (i, 0)),
                      pl.BlockSpec((tb, d), lambda i, dest: (i, 0)),
                      pl.BlockSpec((1, d), lambda i, dest: (0, 0))],
            out_specs=pl.BlockSpec((tb, d), lambda i, dest: (i, 0)),
            scratch_shapes=[pltpu.VMEM((MOE_TOP_K, tb * ROW_SUBLANES, LANES), F32), pltpu.SemaphoreType.DMA(())],
        ),
        out_shape=jax.ShapeDtypeStruct((s, d), F32),
        compiler_params=_params("arbitrary"),
        name="combine",
    )(dest, yb, wts, x2, gate_f.reshape(1, d))


def _moe(h, ids, wts, cnt, x2, gate_f, layer, w_gate, w_up, w_down):
    s = x2.shape[0]
    counts = cnt[0, ROUTE_E0:ROUTE_E0 + N_EXPERTS].astype(jnp.int32)
    padded = (counts + EXPERT_BLOCK - 1) // EXPERT_BLOCK * EXPERT_BLOCK
    pend = jnp.cumsum(padded)
    pstart = pend - padded
    is_expert = ids[:, 0:MOE_TOP_K, None] == jnp.arange(N_EXPERTS, dtype=jnp.int32)
    slot0 = jnp.sum(jnp.where(is_expert, pstart, 0), axis=-1)
    dest = (slot0 + ids[:, MOE_TOP_K:2 * MOE_TOP_K]).reshape(s * MOE_TOP_K)
    nb = (s * MOE_TOP_K) // EXPERT_BLOCK + N_EXPERTS
    block_start = jnp.arange(nb, dtype=jnp.int32) * EXPERT_BLOCK
    block_expert = jnp.sum((pend[None, :] <= block_start[:, None]).astype(jnp.int32), axis=1)
    block_expert = jnp.minimum(block_expert, N_EXPERTS - 1)
    n_used = (pend[-1:] // EXPERT_BLOCK).astype(jnp.int32)
    xb = _dispatch(dest, pend.astype(jnp.int32), padded.astype(jnp.int32), n_used, h, nb * EXPERT_BLOCK)
    yb = _experts(block_expert, n_used, xb, layer, w_gate, w_up, w_down)
    return _combine(dest, yb, wts, x2, gate_f)


def kernel(x, c, w_ada, b_ada, norm_mix, norm_ffn, w_in, ssm_conv_w, ssm_conv_b, ssm_dt_bias, ssm_a_log, ssm_d,
           ssm_norm, sb_q_norm, sb_k_norm, gdn_conv_w, gdn_a_log, gdn_dt_bias, gdn_norm, w_branch, w_out,
           w_group, b_group, w_router, b_router, w_gate, w_up, w_down):
    bsz, s, d = x.shape
    assert bsz == 1 and d == D_MODEL
    depth = w_in.shape[0]
    mod = _adaln_mod(c, w_ada, b_ada)
    x2 = x.reshape(s, d)
    for l in range(depth):
        shift_m, scale_m, gate_m, shift_f, scale_f, gate_f = jnp.split(mod[l], 6)
        proj, narrow = _inproj(x2, norm_mix[l], scale_m, shift_m, *_layout_w_in(w_in[l]), sb_q_norm[l], sb_k_norm[l])
        dtt = narrow[:, NCOL_DT:NCOL_DT + 8].T
        gabt = narrow[:, NCOL_GAB:NCOL_GAB + 8].T
        ya = _ssd(proj, narrow, dtt, ssm_conv_w[l], ssm_conv_b[l], ssm_dt_bias[l], ssm_a_log[l], ssm_d[l], ssm_norm[l])
        yb = _stick_breaking(proj)
        yc = _gdn(proj, narrow, gabt, gdn_conv_w[l], gdn_a_log[l], gdn_dt_bias[l], gdn_norm[l])
        pad = jnp.zeros((d, LANES - MOE_GROUPS - N_EXPERTS), F32)
        w_rt = jnp.concatenate([w_group[l], w_router[l], pad], axis=1)
        b_rt = jnp.concatenate([b_group[l], b_router[l], pad[0]])
        x2, h, ids, wts, cnt = _merge(ya, yb, yc, proj, x2, _bf(w_branch[l]), _bf(w_out[l]), gate_m,
                               norm_ffn[l], scale_f, shift_f, w_rt, b_rt)
        x2 = _moe(h, ids, wts, cnt, x2, gate_f, l, w_gate, w_up, w_down)
    return x2.reshape(bsz, s, d)
```

```python
import functools

import jax
import jax.numpy as jnp
from jax import lax
from jax.experimental import pallas as pl
from jax.experimental.pallas import tpu as pltpu

F32 = jnp.float32
BF16 = jnp.bfloat16
EPS = 1e-6

D_MODEL = 1024
SSM_HEADS = 8
SSM_HEAD_DIM = 64
SSM_INNER = 512
SSM_GROUPS = 2
SSM_STATE = 128
SSM_XBC = 1024
SSD_CHUNK = 128
SB_HEADS = 4
SB_HEAD_DIM = 128
SB_BLOCK = 128
GDN_HEADS = 4
GDN_HEAD_DIM = 128
GDN_CHUNK = 64
MOE_GROUPS = 4
EXPERTS_PER_GROUP = 8
N_EXPERTS = 32
MOE_TOP_K = 2
EXPERT_FF = 512

LANES = 128
COL_BR = 0
COL_SB = 3072
COL_GQKV = 4608
COL_XBC = 6144
COL_Z = 7168
COL_GGATE = 7680
WIDE_COLS = 8192
NCOL_DT = 0
NCOL_GAB = 128
NARROW_COLS = 256


def _col_spec(tb, width, col):
    assert col % width == 0
    return pl.BlockSpec((tb, width), lambda i: (i, col // width))

VMEM_LIMIT = 48 * 1024 * 1024
SB_SKIP_LOG = -110.0


def _bf(x):
    return x.astype(BF16)


def _dot(a, b):
    return jnp.dot(a, b, preferred_element_type=F32)


def _dot_nt(a, b):
    return lax.dot_general(a, b, (((1,), (1,)), ((), ())), preferred_element_type=F32)


def _dot_tn(a, b):
    return lax.dot_general(a, b, (((0,), (0,)), ((), ())), preferred_element_type=F32)


def _split3(x):
    hi = _bf(x)
    r = x - hi.astype(F32)
    mid = _bf(r)
    return hi, mid, _bf(r - mid.astype(F32))


def _dot_sel(sel, x):
    sel_b = _bf(sel)
    hi, mid, lo = _split3(x)
    return _dot(sel_b, hi) + (_dot(sel_b, mid) + _dot(sel_b, lo))


def _dot_sel_r(x, sel):
    sel_b = _bf(sel)
    hi, mid, lo = _split3(x)
    return _dot(hi, sel_b) + (_dot(mid, sel_b) + _dot(lo, sel_b))


def _split2(x):
    hi = _bf(x)
    lo = _bf(x - hi.astype(F32))
    return hi, lo


def _dot3(a, b):
    ah, al = _split2(a)
    bh, bl = _split2(b)
    return _dot(ah, bh) + (_dot(ah, bl) + _dot(al, bh))


def _silu(x):
    return x * jax.nn.sigmoid(x)


def _softplus(x):
    return jnp.maximum(x, 0.0) + jnp.log1p(jnp.exp(-jnp.abs(x)))


def _iota2(shape, dim):
    return lax.broadcasted_iota(jnp.int32, shape, dim)


ROW_SUBLANES = D_MODEL // LANES
assert ROW_SUBLANES == 8


def _store_tile_rows(ref, x):
    n = x.shape[0]
    for sl in range(ROW_SUBLANES):
        ref[pl.ds(sl, n, stride=ROW_SUBLANES), :] = x[:, sl * LANES:(sl + 1) * LANES]


def _load_tile_rows(ref, n):
    return jnp.concatenate([ref[pl.ds(sl, n, stride=ROW_SUBLANES), :] for sl in range(ROW_SUBLANES)], axis=-1)


def _params(*sem):
    return pltpu.CompilerParams(dimension_semantics=sem, vmem_limit_bytes=VMEM_LIMIT)


def _mod_kernel(c_ref, w_ref, b_ref, o_ref):
    c = _silu(c_ref[...])
    o_ref[0] = _dot3(c, w_ref[0]) + b_ref[0]


def _adaln_mod(c, w_ada, b_ada):
    depth, d, cols = w_ada.shape
    tn = 1024
    c8 = jnp.broadcast_to(c, (8, d))
    out = pl.pallas_call(
        _mod_kernel,
        grid=(depth, cols // tn),
        in_specs=[
            pl.BlockSpec((8, d), lambda l, j: (0, 0)),
            pl.BlockSpec((1, d, tn), lambda l, j: (l, 0, j)),
            pl.BlockSpec((1, 1, tn), lambda l, j: (l, 0, j)),
        ],
        out_specs=pl.BlockSpec((1, 8, tn), lambda l, j: (l, 0, j)),
        out_shape=jax.ShapeDtypeStruct((depth, 8, cols), F32),
        compiler_params=_params("arbitrary", "arbitrary"),
        name="adaln_mod",
    )(c8, w_ada, b_ada.reshape(depth, 1, cols))
    return out[:, 0, :]


def _norm_mod(x, g, scale, shift):
    y = x * lax.rsqrt(jnp.mean(x * x, axis=-1, keepdims=True) + EPS)
    return (y * g) * (1.0 + scale) + shift


INPROJ_TN = 1024
assert COL_SB % INPROJ_TN == 0 and 2 * SB_HEADS * SB_HEAD_DIM == INPROJ_TN


def _inproj_kernel(x_ref, g_ref, sc_ref, sh_ref, w_ref, wn_ref, qkg_ref, qks_ref, o_ref, on_ref, h_scr):
    j = pl.program_id(1)

    @pl.when(j == 0)
    def _():
        h = _bf(_norm_mod(x_ref[...], g_ref[...], sc_ref[...], sh_ref[...]))
        h_scr[...] = h
        on_ref[...] = _dot(h, wn_ref[...])

    @pl.when(j != COL_SB // INPROJ_TN)
    def _():
        o_ref[...] = _bf(_dot(h_scr[...], w_ref[...]))

    @pl.when(j == COL_SB // INPROJ_TN)
    def _():
        acc = _dot(h_scr[...], w_ref[...])
        dh = SB_HEAD_DIM
        for n in range(INPROJ_TN // dh):
            cols = slice(n * dh, (n + 1) * dh)
            o_ref[:, cols] = _bf(_head_rms(acc[:, cols], qkg_ref[:, cols]) * qks_ref[:, cols])


def _inproj(x2, g, scale, shift, w_wide, w_narrow, q_g, k_g):
    s, d = x2.shape
    tm = min(1024, s)
    tn = INPROJ_TN
    row = lambda a: a.reshape(1, d)
    vec = pl.BlockSpec((1, d), lambda i, j: (0, 0))
    qk_gain = jnp.concatenate([jnp.tile(q_g, SB_HEADS), jnp.tile(k_g, SB_HEADS)])
    qk_scale = jnp.concatenate([jnp.full((tn // 2,), SB_HEAD_DIM ** -0.5, F32), jnp.ones((tn // 2,), F32)])
    return pl.pallas_call(
        _inproj_kernel,
        grid=(s // tm, WIDE_COLS // tn),
        in_specs=[pl.BlockSpec((tm, d), lambda i, j: (i, 0)), vec, vec, vec,
                  pl.BlockSpec((d, tn), lambda i, j: (0, j)),
                  pl.BlockSpec((d, NARROW_COLS), lambda i, j: (0, 0)),
                  pl.BlockSpec((1, tn), lambda i, j: (0, 0)), pl.BlockSpec((1, tn), lambda i, j: (0, 0))],
        out_specs=[pl.BlockSpec((tm, tn), lambda i, j: (i, j)),
                   pl.BlockSpec((tm, NARROW_COLS), lambda i, j: (i, 0))],
        out_shape=[jax.ShapeDtypeStruct((s, WIDE_COLS), BF16), jax.ShapeDtypeStruct((s, NARROW_COLS), F32)],
        scratch_shapes=[pltpu.VMEM((tm, d), BF16)],
        compiler_params=_params("arbitrary", "arbitrary"),
        name="inproj",
    )(x2, row(g), row(scale), row(shift), w_wide, w_narrow, qk_gain.reshape(1, tn), qk_scale.reshape(1, tn))


def _causal_conv4(x, ext_scr, w_ref):
    tb = x.shape[0]
    ext_scr[8:8 + tb, :] = x
    y = x * w_ref[3:4, :]
    for k in (1, 2, 3):
        y = y + ext_scr[8 - k:8 - k + tb, :] * w_ref[3 - k:4 - k, :]
    ext_scr[0:8, :] = x[tb - 8:tb]
    return y


def _ssd_kernel(z_ref, xbc_ref, dt_ref, dtt_ref, cw_ref, cb_ref, dtb_ref, dtbt_ref, al_ref, alt_ref,
                dsk_ref, ng_ref, o_ref, ext_scr, act_scr, y_scr, st_scr):
    tb = xbc_ref.shape[0]
    L = SSD_CHUNK
    P = SSM_HEAD_DIM

    @pl.when(pl.program_id(0) == 0)
    def _():
        ext_scr[0:8, :] = jnp.zeros((8, ext_scr.shape[1]), F32)
        st_scr[...] = jnp.zeros_like(st_scr)

    act_scr[...] = _silu(_causal_conv4(xbc_ref[...].astype(F32), ext_scr, cw_ref) + cb_ref[...])

    ri = _iota2((L, L), 0)
    ci = _iota2((L, L), 1)
    tril = (ri >= ci).astype(F32)
    causal = ri >= ci
    a_col = -jnp.exp(al_ref[...])
    a_row = -jnp.exp(alt_ref[...])
    expand = (_iota2((LANES, SSM_INNER), 1) // P == _iota2((LANES, SSM_INNER), 0)).astype(F32)
    hpg = SSM_HEADS // SSM_GROUPS
    gw = hpg * P

    for c in range(tb // L):
        rows = slice(c * L, (c + 1) * L)
        dt = _softplus(dt_ref[rows, :] + dtb_ref[...])
        dtt = _softplus(dtt_ref[:, rows] + dtbt_ref[...])
        acum = _dot_sel(tril, dt * a_col)
        acum_t = _dot_sel_r(dtt * a_row, tril.T)
        per_head = jnp.concatenate([dt, jnp.exp(acum[L - 1:L, :] - acum), jnp.exp(acum)], axis=0)
        per_chan = _dot_sel_r(per_head, expand)
        dt_e, to_end_e, from_start_e = per_chan[:L], per_chan[L:2 * L], per_chan[2 * L:]
        act = act_scr[rows, :]
        xs = act[:, :SSM_INNER]
        xdt = xs * dt_e
        xdt_b = _bf(xdt)
        xdt_end_b = _bf(xdt * to_end_e)
        off_c = SSM_INNER + SSM_GROUPS * SSM_STATE
        y_in, y_st = [], []
        for g in range(SSM_GROUPS):
            bm = _bf(act[:, SSM_INNER + g * SSM_STATE:SSM_INNER + (g + 1) * SSM_STATE])
            cm = _bf(act[:, off_c + g * SSM_STATE:off_c + (g + 1) * SSM_STATE])
            gmat = _dot_nt(cm, bm)
            st = st_scr[g]
            y_st.append(_dot_nt(cm, _bf(st)))
            new = _dot_tn(xdt_end_b[:, g * gw:(g + 1) * gw], bm)
            for hh in range(hpg):
                h = g * hpg + hh
                seg = jnp.exp(jnp.where(causal, acum[:, h:h + 1] - acum_t[h:h + 1, :], -jnp.inf))
                y_in.append(_dot(_bf(gmat * seg), xdt_b[:, h * P:(h + 1) * P]))
                sl = slice(hh * P, (hh + 1) * P)
                st_scr[g, sl, :] = st[sl] * jnp.exp(acum_t[h:h + 1, L - 1:L]) + new[sl]
        y_scr[rows, :] = (jnp.concatenate(y_in, axis=-1) + jnp.concatenate(y_st, axis=-1) * from_start_e
                          + xs * dsk_ref[...])

    y = y_scr[...] * _silu(z_ref[...].astype(F32))
    gsz = SSM_INNER // SSM_GROUPS
    outs = []
    for g in range(SSM_GROUPS):
        yg = y[:, g * gsz:(g + 1) * gsz]
        yn = yg * lax.rsqrt(jnp.mean(yg * yg, axis=-1, keepdims=True) + EPS)
        outs.append(yn * ng_ref[:, g * gsz:(g + 1) * gsz])
    o_ref[...] = jnp.concatenate(outs, axis=-1)


def _ssd(proj, narrow, dtt, conv_w, conv_b, dt_bias, a_log, d_skip, norm_g):
    s = proj.shape[0]
    tb = min(512, s)
    pad8 = lambda v: jnp.pad(v, (0, LANES - v.shape[0])).reshape(1, LANES)
    colv = lambda v: v.reshape(SSM_HEADS, 1)
    full = lambda shape: pl.BlockSpec(shape, lambda i: (0,) * len(shape))
    return pl.pallas_call(
        _ssd_kernel,
        grid=(s // tb,),
        in_specs=[
            _col_spec(tb, SSM_INNER, COL_Z),
            _col_spec(tb, SSM_XBC, COL_XBC),
            _col_spec(tb, LANES, NCOL_DT),
            pl.BlockSpec((SSM_HEADS, tb), lambda i: (0, i)),
            full((4, SSM_XBC)), full((1, SSM_XBC)), full((1, LANES)), full((SSM_HEADS, 1)),
            full((1, LANES)), full((SSM_HEADS, 1)), full((1, SSM_INNER)), full((1, SSM_INNER)),
        ],
        out_specs=pl.BlockSpec((tb, SSM_INNER), lambda i: (i, 0)),
        out_shape=jax.ShapeDtypeStruct((s, SSM_INNER), F32),
        scratch_shapes=[
            pltpu.VMEM((tb + 8, SSM_XBC), F32),
            pltpu.VMEM((tb, SSM_XBC), F32),
            pltpu.VMEM((tb, SSM_INNER), F32),
            pltpu.VMEM((SSM_GROUPS, SSM_HEADS // SSM_GROUPS * SSM_HEAD_DIM, SSM_STATE), F32),
        ],
        compiler_params=_params("arbitrary"),
        name="ssd",
    )(proj, proj, narrow, dtt, conv_w, conv_b.reshape(1, -1), pad8(dt_bias), colv(dt_bias),
      pad8(a_log), colv(a_log), jnp.repeat(d_skip, SSM_HEAD_DIM).reshape(1, -1), norm_g.reshape(1, -1))


def _layout_w_in(w):
    d = w.shape[0]
    z, xbc, dt, sb, gqkv, gab, ggate, br = jnp.split(w, [512, 1536, 1544, 3080, 4616, 4624, 5136], axis=1)
    pad = jnp.zeros((d, LANES - 8), w.dtype)
    wide = jnp.concatenate([br, sb, gqkv, xbc, z, ggate], axis=1).astype(BF16)
    narrow = jnp.concatenate([dt, pad, gab, pad], axis=1).astype(BF16)
    return wide, narrow


def _head_rms(x, g):
    return (x * lax.rsqrt(jnp.mean(x * x, axis=-1, keepdims=True) + EPS)) * g


def _sb_blocks(qs, kns, vs, accs, suffix, masked):
    blk = qs[0].shape[0]
    strict = _iota2((blk, blk), 1) < _iota2((blk, blk), 0)
    zs = [_dot_nt(q, kn) for q, kn in zip(qs, kns)]
    sps = [jnp.maximum(z, 0.0) + jnp.log(1.0 + jnp.exp(-jnp.abs(z))) for z in zs]
    log_keeps = [jnp.where(strict, -sp, 0.0) if masked else -sp for sp in sps]
    splits = [_split2(lk) for lk in log_keeps]
    afters = [(_dot(hi, suffix) + _dot(lo, suffix)) + acc for (hi, lo), acc in zip(splits, accs)]
    atts = [jnp.exp((z - sp) + after) for z, sp, after in zip(zs, sps, afters)]
    if masked:
        atts = [jnp.where(strict, att, 0.0) for att in atts]
    outs = [_dot(_bf(att), v) for att, v in zip(atts, vs)]
    return outs, [jnp.sum(lk, axis=-1, keepdims=True) for lk in log_keeps]


SB_QBLOCKS = 2


def _sb_kernel(q_ref, k_ref, v_ref, o_ref, acc_scr):
    blk = SB_BLOCK
    dh = SB_HEAD_DIM
    first = pl.program_id(0) * SB_QBLOCKS
    pairs = [(b, h) for b in range(SB_QBLOCKS) for h in range(SB_HEADS)]
    suffix = (_iota2((blk, blk), 0) > _iota2((blk, blk), 1)).astype(BF16)
    qs = [q_ref[b * blk:(b + 1) * blk, h * dh:(h + 1) * dh] for b, h in pairs]

    def load_kv(offset):
        rows = [pl.ds(pl.multiple_of(jnp.maximum(first + b - offset, 0) * blk, blk), blk) for b in range(SB_QBLOCKS)]
        return ([k_ref[rows[b], h * dh:(h + 1) * dh] for b, h in pairs],
                [v_ref[rows[b], h * dh:(h + 1) * dh] for b, h in pairs])

    def live(accs):
        top = functools.reduce(jnp.maximum, accs)
        return (jnp.max(top) > SB_SKIP_LOG).astype(jnp.int32)

    def out_slice(n):
        b, h = pairs[n]
        return (slice(b * blk, (b + 1) * blk), slice(h * dh, (h + 1) * dh))

    kns, vs = load_kv(0)
    outs, sums = _sb_blocks(qs, kns, vs, [jnp.zeros((blk, 1), F32)] * len(pairs), suffix, True)
    for n in range(len(pairs)):
        o_ref[out_slice(n)] = outs[n]
        acc_scr[n] = sums[n]

    def cond(carry):
        offset, alive = carry
        return jnp.logical_and(offset <= first + SB_QBLOCKS - 1, alive > 0)

    def body(carry):
        offset, _ = carry
        kns, vs = load_kv(offset)
        accs = [acc_scr[n] for n in range(len(pairs))]
        outs, sums = _sb_blocks(qs, kns, vs, accs, suffix, False)
        valid = [jnp.where(first + b - offset >= 0, 1.0, 0.0) for b in range(SB_QBLOCKS)]
        accs = [acc + rs * valid[b] for acc, rs, (b, _) in zip(accs, sums, pairs)]
        for n in range(len(pairs)):
            o_ref[out_slice(n)] += outs[n] * valid[pairs[n][0]]
            acc_scr[n] = accs[n]
        return offset + 1, live(accs)

    lax.while_loop(cond, body, (jnp.int32(1), live(sums)))


def _stick_breaking(proj):
    s = proj.shape[0]
    tq = SB_QBLOCKS * SB_BLOCK
    dh = SB_HEAD_DIM
    width = SB_HEADS * dh
    resident = lambda col: pl.BlockSpec((s, width), lambda i: (0, col // width), pipeline_mode=pl.Buffered(1))
    return pl.pallas_call(
        _sb_kernel,
        grid=(s // tq,),
        in_specs=[_col_spec(tq, width, COL_SB), resident(COL_SB + width), resident(COL_SB + 2 * width)],
        out_specs=pl.BlockSpec((tq, width), lambda i: (i, 0)),
        out_shape=jax.ShapeDtypeStruct((s, width), F32),
        scratch_shapes=[pltpu.VMEM((SB_QBLOCKS * SB_HEADS, SB_BLOCK, 1), F32)],
        compiler_params=_params("arbitrary"),
        name="stick_breaking",
    )(proj, proj, proj)


GDN_SUB = 128


def _dot3_nt(a, b):
    ah, al = _split2(a)
    bh, bl = _split2(b)
    return _dot_nt(ah, bh) + (_dot_nt(ah, bl) + _dot_nt(al, bh))


def _chunk_lower_inverses(ms, chunk):
    n = ms[0].shape[0]
    eye = (_iota2((n, n), 0) == _iota2((n, n), 1)).astype(F32)
    ps = [-m for m in ms]
    invs = [eye + p for p in ps]
    p_parts = [_split2(p) for p in ps]
    for j in range((chunk - 1).bit_length() - 1):
        if j == 0:
            ps = [_dot(ph, ph) + (_dot(ph, pl_) + _dot(pl_, ph)) for ph, pl_ in p_parts]
        else:
            ps = [_dot(ph, ph) for ph, _ in p_parts]
        p_parts = [_split2(p) for p in ps]
        inv_parts = [_split2(inv) for inv in invs]
        invs = [inv + (_dot(ih, ph) + (_dot(ih, pl_) + _dot(il, ph)))
                for inv, (ih, il), (ph, pl_) in zip(invs, inv_parts, p_parts)]
    return invs


def _gdn_kernel(qkv_ref, gab_ref, gabt_ref, gate_ref, cw_ref, al_ref, alt_ref, dtb_ref, dtbt_ref, ng_ref,
                o_ref, ext_scr, act_scr, st_scr):
    tb = qkv_ref.shape[0]
    C = GDN_CHUNK
    dh = GDN_HEAD_DIM
    inner = GDN_HEADS * dh
    heads = range(GDN_HEADS)

    @pl.when(pl.program_id(0) == 0)
    def _():
        ext_scr[0:8, :] = jnp.zeros((8, ext_scr.shape[1]), F32)
        st_scr[...] = jnp.zeros_like(st_scr)

    act_scr[...] = _silu(_causal_conv4(qkv_ref[...].astype(F32), ext_scr, cw_ref))

    sub = min(GDN_SUB, tb)
    cps = sub // C
    ri = _iota2((sub, sub), 0)
    ci = _iota2((sub, sub), 1)
    same = (ri // C) == (ci // C)
    incl = jnp.logical_and(same, ri >= ci)
    strict = jnp.logical_and(same, ri > ci)
    tril = incl.astype(F32)
    al_col = -jnp.exp(al_ref[...])
    al_row = -jnp.exp(alt_ref[...])
    units = [(b, h) for b in range(tb // sub) for h in heads]

    gc_cols, gc_rows, betas = [], [], []
    for b in range(tb // sub):
        rows = slice(b * sub, (b + 1) * sub)
        gab = gab_ref[rows, :]
        g_col = al_col * _softplus(gab + dtb_ref[...])
        g_row = al_row * _softplus(gabt_ref[:, rows] + dtbt_ref[...])
        gc_cols.append(_dot_sel(tril, g_col))
        gc_rows.append(_dot_sel_r(g_row, tril.T))
        betas.append(jax.nn.sigmoid(gab))

    qs, ks, kbs, gcs, decays, rhss = [], [], [], [], [], []
    for b, h in units:
        rows = slice(b * sub, (b + 1) * sub)
        q = act_scr[rows, h * dh:(h + 1) * dh]
        k = act_scr[rows, inner + h * dh:inner + (h + 1) * dh]
        v = act_scr[rows, 2 * inner + h * dh:2 * inner + (h + 1) * dh]
        q = q * lax.rsqrt(jnp.sum(q * q, axis=-1, keepdims=True) + EPS) * (dh ** -0.5)
        k = k * lax.rsqrt(jnp.sum(k * k, axis=-1, keepdims=True) + EPS)
        beta = betas[b][:, GDN_HEADS + h:GDN_HEADS + h + 1]
        gc = gc_cols[b][:, h:h + 1]
        kb = k * beta
        qs.append(q)
        ks.append(k)
        kbs.append(kb)
        gcs.append(gc)
        decays.append(jnp.exp(jnp.where(incl, gc - gc_rows[b][h:h + 1, :], -jnp.inf)))
        rhss.append(jnp.concatenate([v * beta, kb * jnp.exp(gc)], axis=-1))

    n_units = range(len(units))
    ms = [jnp.where(strict, _dot3_nt(kbs[n], ks[n]) * decays[n], 0.0) for n in n_units]
    invs = _chunk_lower_inverses(ms, C)
    sols = [_dot3(invs[n], rhss[n]) for n in n_units]
    attns = [_bf(jnp.where(incl, _dot_nt(_bf(qs[n]), _bf(ks[n])) * decays[n], 0.0)) for n in n_units]
    q_decs = [_bf(qs[n] * jnp.exp(gcs[n])) for n in n_units]

    sts = [st_scr[h] for h in heads]
    v_news = [[] for _ in n_units]
    o_inters = [[] for _ in n_units]
    for c in range(tb // C):
        b = c // cps
        rows = slice((c % cps) * C, (c % cps + 1) * C)
        for h in heads:
            n = b * GDN_HEADS + h
            st_b = _bf(sts[h])
            g_last = gc_rows[b][h:h + 1, rows.stop - 1:rows.stop]
            v_new = sols[n][rows, :dh] - _dot(_bf(sols[n][rows, dh:]), st_b)
            v_new_b = _bf(v_new)
            o_inters[n].append(_dot(q_decs[n][rows, :], st_b))
            k_dec = ks[n][rows, :] * jnp.exp(g_last - gcs[n][rows, :])
            sts[h] = sts[h] * jnp.exp(g_last) + _dot_tn(_bf(k_dec), v_new_b)
            v_news[n].append(v_new_b)
    for h in heads:
        st_scr[h] = sts[h]
    for n, (b, h) in enumerate(units):
        rows = slice(b * sub, (b + 1) * sub)
        o = jnp.concatenate(o_inters[n], axis=0) + _dot(attns[n], jnp.concatenate(v_news[n], axis=0))
        o = _head_rms(o, ng_ref[...]) * _silu(gate_ref[rows, h * dh:(h + 1) * dh].astype(F32))
        o_ref[rows, h * dh:(h + 1) * dh] = o


def _gdn(proj, narrow, gabt, conv_w, a_log, dt_bias, norm_g):
    s = proj.shape[0]
    tb = min(512, s)
    inner = GDN_HEADS * GDN_HEAD_DIM
    pad_lane = lambda v: jnp.pad(v, (0, LANES - v.shape[0])).reshape(1, LANES)
    pad_col = lambda v: jnp.pad(v, (0, 8 - v.shape[0])).reshape(8, 1)
    full = lambda shape: pl.BlockSpec(shape, lambda i: (0,) * len(shape))
    return pl.pallas_call(
        _gdn_kernel,
        grid=(s // tb,),
        in_specs=[
            _col_spec(tb, 3 * inner, COL_GQKV),
            _col_spec(tb, LANES, NCOL_GAB),
            pl.BlockSpec((8, tb), lambda i: (0, i)),
            _col_spec(tb, inner, COL_GGATE),
            full((4, 3 * inner)), full((1, LANES)), full((8, 1)), full((1, LANES)), full((8, 1)),
            full((1, GDN_HEAD_DIM)),
        ],
        out_specs=pl.BlockSpec((tb, inner), lambda i: (i, 0)),
        out_shape=jax.ShapeDtypeStruct((s, inner), F32),
        scratch_shapes=[
            pltpu.VMEM((tb + 8, 3 * inner), F32),
            pltpu.VMEM((tb, 3 * inner), F32),
            pltpu.VMEM((GDN_HEADS, GDN_HEAD_DIM, GDN_HEAD_DIM), F32),
        ],
        compiler_params=_params("arbitrary"),
        name="gdn",
    )(proj, narrow, gabt, proj, conv_w, pad_lane(a_log), pad_col(a_log), pad_lane(dt_bias), pad_col(dt_bias),
      norm_g.reshape(1, -1))


def _merge_kernel(ya_ref, yb_ref, yc_ref, br_ref, x_ref, wbr_ref, wout_ref, gm_ref, g_ref, sc_ref, sh_ref,
                  wrt_ref, brt_ref, xo_ref, h_ref, ids_ref, wts_ref, cnt_ref, seen_scr):
    d = x_ref.shape[1]

    @pl.when(pl.program_id(0) == 0)
    def _():
        seen_scr[...] = jnp.zeros_like(seen_scr)

    merged = None
    for i, y_ref in enumerate((ya_ref, yb_ref, yc_ref)):
        gate = jax.nn.sigmoid(br_ref[:, i * d:(i + 1) * d].astype(F32))
        term = gate * _dot(_bf(y_ref[...]), wbr_ref[i])
        merged = term if merged is None else merged + term
    x_new = x_ref[...] + gm_ref[...] * _dot(_bf(merged), wout_ref[...])
    xo_ref[...] = x_new
    h = _norm_mod(x_new, g_ref[...], sc_ref[...], sh_ref[...])
    _store_tile_rows(h_ref, h)
    ids, wts, seen = _route_rows(_dot3(h, wrt_ref[...]) + brt_ref[...], seen_scr[...])
    ids_ref[...] = ids
    wts_ref[...] = wts
    seen_scr[...] = seen
    cnt_ref[...] = jnp.broadcast_to(seen, cnt_ref.shape)


def _merge(ya, yb, yc, proj, x2, wbr_bf, wout_bf, gate_m, g, scale, shift, w_rt, b_rt):
    s, d = x2.shape
    tb = min(512, s)
    bw = ya.shape[1]
    row = lambda a: a.reshape(1, -1)
    vec = pl.BlockSpec((1, d), lambda i: (0, 0))
    blk = lambda w: pl.BlockSpec((tb, w), lambda i: (i, 0))
    return pl.pallas_call(
        _merge_kernel,
        grid=(s // tb,),
        in_specs=[blk(bw), blk(bw), blk(bw), _col_spec(tb, 3 * d, COL_BR), blk(d),
                  pl.BlockSpec((3, bw, d), lambda i: (0, 0, 0)), pl.BlockSpec((d, d), lambda i: (0, 0)),
                  vec, vec, vec, vec,
                  pl.BlockSpec((d, LANES), lambda i: (0, 0)), pl.BlockSpec((1, LANES), lambda i: (0, 0))],
        out_specs=[blk(d), pl.BlockSpec((tb * ROW_SUBLANES, LANES), lambda i: (i, 0)), blk(LANES), blk(LANES),
                   pl.BlockSpec((8, LANES), lambda i: (0, 0))],
        out_shape=[jax.ShapeDtypeStruct((s, d), F32), jax.ShapeDtypeStruct((s * ROW_SUBLANES, LANES), F32),
                   jax.ShapeDtypeStruct((s, LANES), jnp.int32), jax.ShapeDtypeStruct((s, LANES), F32),
                   jax.ShapeDtypeStruct((8, LANES), F32)],
        scratch_shapes=[pltpu.VMEM((1, LANES), F32)],
        compiler_params=_params("arbitrary"),
        name="merge",
    )(ya, yb, yc, proj, x2, wbr_bf, wout_bf, row(gate_m), row(g), row(scale), row(shift), w_rt, row(b_rt))


ROUTE_E0 = MOE_GROUPS


def _route_rows(lg, seen_before):
    tb = lg.shape[0]
    lane = _iota2((tb, LANES), 1)
    big = jnp.int32(LANES)
    neg = -jnp.inf
    gl = jnp.where(lane < MOE_GROUPS, lg, neg)
    gmax = jnp.max(gl, axis=-1, keepdims=True)
    g_sel = jnp.min(jnp.where(gl == gmax, lane, big), axis=-1, keepdims=True)
    p_group = 1.0 / jnp.sum(jnp.exp(gl - gmax), axis=-1, keepdims=True)
    lo = ROUTE_E0 + EXPERTS_PER_GROUP * g_sel
    el = jnp.where(jnp.logical_and(lane >= lo, lane < lo + EXPERTS_PER_GROUP), lg, neg)
    m1 = jnp.max(el, axis=-1, keepdims=True)
    i1 = jnp.min(jnp.where(el == m1, lane, big), axis=-1, keepdims=True)
    esum = jnp.sum(jnp.exp(el - m1), axis=-1, keepdims=True)
    el2 = jnp.where(lane == i1, neg, el)
    m2 = jnp.max(el2, axis=-1, keepdims=True)
    i2 = jnp.min(jnp.where(el2 == m2, lane, big), axis=-1, keepdims=True)
    p1 = 1.0 / esum
    p2 = jnp.exp(m2 - m1) / esum
    w1 = p_group * p1 / (p1 + p2)
    w2 = p_group * p2 / (p1 + p2)

    sel1 = lane == i1
    sel2 = lane == i2
    onehot = jnp.where(jnp.logical_or(sel1, sel2), 1.0, 0.0)
    before = (_iota2((tb, tb), 0) > _iota2((tb, tb), 1)).astype(BF16)
    seen = _dot(before, _bf(onehot)) + seen_before
    r1 = jnp.sum(jnp.where(sel1, seen, 0.0), axis=-1, keepdims=True)
    r2 = jnp.sum(jnp.where(sel2, seen, 0.0), axis=-1, keepdims=True)
    ids = jnp.where(lane == 0, i1 - ROUTE_E0, jnp.where(lane == 1, i2 - ROUTE_E0,
          jnp.where(lane == 2, r1.astype(jnp.int32), jnp.where(lane == 3, r2.astype(jnp.int32), 0))))
    wts = jnp.where(lane == 0, w1, jnp.where(lane == 1, w2, 0.0))
    return ids, wts, seen_before + jnp.sum(onehot, axis=0, keepdims=True)


EXPERT_BLOCK = 512
ROW_TB = 512


def _tile_row(ref, row):
    if not isinstance(row, int):
        row = pl.multiple_of(row * ROW_SUBLANES, ROW_SUBLANES)
    else:
        row = row * ROW_SUBLANES
    return ref.at[pl.ds(row, ROW_SUBLANES), :]


def _dispatch_kernel(dest_ref, pend_ref, padded_ref, nused_ref, h_ref, xb_ref, zero_scr, sem, zero_sem):
    tb = h_ref.shape[0] // ROW_SUBLANES
    blk_rows = EXPERT_BLOCK * ROW_SUBLANES
    base = pl.program_id(0) * tb * MOE_TOP_K

    @pl.when(pl.program_id(0) == 0)
    def _():
        zero_scr[...] = jnp.zeros_like(zero_scr)

        def zero_block(block):
            start = block * blk_rows
            if not isinstance(start, int):
                start = pl.multiple_of(start, blk_rows)
            return pltpu.make_async_copy(zero_scr, xb_ref.at[pl.ds(start, blk_rows), :], zero_sem)

        def for_each_zero_block(action):
            for e in range(N_EXPERTS):
                @pl.when(padded_ref[e] > 0)
                def _():
                    action(zero_block(pend_ref[e] // EXPERT_BLOCK - 1))
            for b in range(xb_ref.shape[0] // blk_rows):
                @pl.when(b >= nused_ref[0])
                def _():
                    action(zero_block(b))

        for_each_zero_block(lambda copy: copy.start())
        for_each_zero_block(lambda copy: copy.wait())

    def row_copy(t, k, d):
        return pltpu.make_async_copy(_tile_row(h_ref, t), _tile_row(xb_ref, d), sem)

    def issue(t, carry):
        for k in range(MOE_TOP_K):
            row_copy(t, k, dest_ref[base + t * MOE_TOP_K + k]).start(priority=k % 2)
        return carry

    for t in range(tb):
        issue(t, 0)
    for k in range(MOE_TOP_K):
        pltpu.make_async_copy(h_ref, xb_ref.at[pl.ds(0, tb * ROW_SUBLANES), :], sem).wait()


def _dispatch(dest, pend, padded, n_used, h_tiles, n_slots):
    s = h_tiles.shape[0] // ROW_SUBLANES
    tb = min(ROW_TB, s)
    return pl.pallas_call(
        _dispatch_kernel,
        grid_spec=pltpu.PrefetchScalarGridSpec(
            num_scalar_prefetch=4,
            grid=(s // tb,),
            in_specs=[pl.BlockSpec((tb * ROW_SUBLANES, LANES), lambda i, *_: (i, 0))],
            out_specs=pl.BlockSpec(memory_space=pl.ANY),
            scratch_shapes=[pltpu.VMEM((EXPERT_BLOCK * ROW_SUBLANES, LANES), F32), pltpu.SemaphoreType.DMA(()),
                            pltpu.SemaphoreType.DMA(())],
        ),
        out_shape=jax.ShapeDtypeStruct((n_slots * ROW_SUBLANES, LANES), F32),
        compiler_params=_params("arbitrary"),
        name="dispatch",
    )(dest, pend, padded, n_used, h_tiles)


def _expert_kernel(be_ref, nused_ref, x_ref, wg_ref, wu_ref, wd_ref, o_ref, wg_b, wu_b, wd_b):
    b = pl.program_id(0)

    @pl.when(jnp.logical_or(b == 0, be_ref[b] != be_ref[jnp.maximum(b - 1, 0)]))
    def _():
        wg_b[...] = _bf(wg_ref[0, 0])
        wu_b[...] = _bf(wu_ref[0, 0])
        wd_b[...] = _bf(wd_ref[0, 0])

    @pl.when(b < nused_ref[0])
    def _():
        x = _bf(_load_tile_rows(x_ref, EXPERT_BLOCK))
        hid = _silu(_dot(x, wg_b[...])) * _dot(x, wu_b[...])
        _store_tile_rows(o_ref, _dot(_bf(hid), wd_b[...]))

    @pl.when(b >= nused_ref[0])
    def _():
        o_ref[...] = jnp.zeros_like(o_ref)


def _experts(block_expert, n_used, xb, layer, w_gate, w_up, w_down):
    d, ff = w_gate.shape[2:]
    blk_rows = EXPERT_BLOCK * ROW_SUBLANES
    nb = xb.shape[0] // blk_rows
    return pl.pallas_call(
        _expert_kernel,
        grid_spec=pltpu.PrefetchScalarGridSpec(
            num_scalar_prefetch=2,
            grid=(nb,),
            in_specs=[
                pl.BlockSpec((blk_rows, LANES), lambda b, be, nu: (b, 0)),
                pl.BlockSpec((1, 1, d, ff), lambda b, be, nu: (layer, be[b], 0, 0)),
                pl.BlockSpec((1, 1, d, ff), lambda b, be, nu: (layer, be[b], 0, 0)),
                pl.BlockSpec((1, 1, ff, d), lambda b, be, nu: (layer, be[b], 0, 0)),
            ],
            out_specs=pl.BlockSpec((blk_rows, LANES), lambda b, be, nu: (b, 0)),
            scratch_shapes=[pltpu.VMEM((d, ff), BF16), pltpu.VMEM((d, ff), BF16), pltpu.VMEM((ff, d), BF16)],
        ),
        out_shape=jax.ShapeDtypeStruct(xb.shape, F32),
        compiler_params=_params("arbitrary"),
        name="experts",
    )(block_expert, n_used, xb, w_gate, w_up, w_down)


def _combine_kernel(dest_ref, yb_ref, wts_ref, x_ref, gf_ref, o_ref, buf, sem):
    tb = x_ref.shape[0]
    base = pl.program_id(0) * tb * MOE_TOP_K

    def row_copy(t, k, d):
        return pltpu.make_async_copy(_tile_row(yb_ref, d), _tile_row(buf.at[k], t), sem)

    def issue(t, carry):
        for k in range(MOE_TOP_K):
            row_copy(t, k, dest_ref[base + t * MOE_TOP_K + k]).start(priority=k % 2)
        return carry

    for t in range(tb):
        issue(t, 0)
    for k in range(MOE_TOP_K):
        pltpu.make_async_copy(yb_ref.at[pl.ds(0, tb * ROW_SUBLANES), :], buf.at[k], sem).wait()
    wts = wts_ref[...]
    y = wts[:, 0:1] * _load_tile_rows(buf.at[0], tb) + wts[:, 1:2] * _load_tile_rows(buf.at[1], tb)
    o_ref[...] = x_ref[...] + gf_ref[...] * y


def _combine(dest, yb, wts, x2, gate_f):
    s, d = x2.shape
    tb = min(ROW_TB, s)
    return pl.pallas_call(
        _combine_kernel,
        grid_spec=pltpu.PrefetchScalarGridSpec(
            num_scalar_prefetch=1,
            grid=(s // tb,),
            in_specs=[pl.BlockSpec(memory_space=pl.ANY),
                      pl.BlockSpec((tb, LANES), lambda i, dest: (i, 0)),
                      pl.BlockSpec((tb, d), lambda i, dest: (i, 0)),
                      pl.BlockSpec((1, d), lambda i, dest: (0, 0))],
            out_specs=pl.BlockSpec((tb, d), lambda i, dest: (i, 0)),
            scratch_shapes=[pltpu.VMEM((MOE_TOP_K, tb * ROW_SUBLANES, LANES), F32), pltpu.SemaphoreType.DMA(())],
        ),
        out_shape=jax.ShapeDtypeStruct((s, d), F32),
        compiler_params=_params("arbitrary"),
        name="combine",
    )(dest, yb, wts, x2, gate_f.reshape(1, d))


def _moe(h, ids, wts, cnt, x2, gate_f, layer, w_gate, w_up, w_down):
    s = x2.shape[0]
    counts = cnt[0, ROUTE_E0:ROUTE_E0 + N_EXPERTS].astype(jnp.int32)
    padded = (counts + EXPERT_BLOCK - 1) // EXPERT_BLOCK * EXPERT_BLOCK
    pend = jnp.cumsum(padded)
    pstart = pend - padded
    is_expert = ids[:, 0:MOE_TOP_K, None] == jnp.arange(N_EXPERTS, dtype=jnp.int32)
    slot0 = jnp.sum(jnp.where(is_expert, pstart, 0), axis=-1)
    dest = (slot0 + ids[:, MOE_TOP_K:2 * MOE_TOP_K]).reshape(s * MOE_TOP_K)
    nb = (s * MOE_TOP_K) // EXPERT_BLOCK + N_EXPERTS
    block_start = jnp.arange(nb, dtype=jnp.int32) * EXPERT_BLOCK
    block_expert = jnp.sum((pend[None, :] <= block_start[:, None]).astype(jnp.int32), axis=1)
    block_expert = jnp.minimum(block_expert, N_EXPERTS - 1)
    n_used = (pend[-1:] // EXPERT_BLOCK).astype(jnp.int32)
    xb = _dispatch(dest, pend.astype(jnp.int32), padded.astype(jnp.int32), n_used, h, nb * EXPERT_BLOCK)
    yb = _experts(block_expert, n_used, xb, layer, w_gate, w_up, w_down)
    return _combine(dest, yb, wts, x2, gate_f)


def kernel(x, c, w_ada, b_ada, norm_mix, norm_ffn, w_in, ssm_conv_w, ssm_conv_b, ssm_dt_bias, ssm_a_log, ssm_d,
           ssm_norm, sb_q_norm, sb_k_norm, gdn_conv_w, gdn_a_log, gdn_dt_bias, gdn_norm, w_branch, w_out,
           w_group, b_group, w_router, b_router, w_gate, w_up, w_down):
    bsz, s, d = x.shape
    assert bsz == 1 and d == D_MODEL
    depth = w_in.shape[0]
    mod = _adaln_mod(c, w_ada, b_ada)
    x2 = x.reshape(s, d)
    for l in range(depth):
        shift_m, scale_m, gate_m, shift_f, scale_f, gate_f = jnp.split(mod[l], 6)
        proj, narrow = _inproj(x2, norm_mix[l], scale_m, shift_m, *_layout_w_in(w_in[l]), sb_q_norm[l], sb_k_norm[l])
        dtt = narrow[:, NCOL_DT:NCOL_DT + 8].T
        gabt = narrow[:, NCOL_GAB:NCOL_GAB + 8].T
        ya = _ssd(proj, narrow, dtt, ssm_conv_w[l], ssm_conv_b[l], ssm_dt_bias[l], ssm_a_log[l], ssm_d[l], ssm_norm[l])
        yb = _stick_breaking(proj)
        yc = _gdn(proj, narrow, gabt, gdn_conv_w[l], gdn_a_log[l], gdn_dt_bias[l], gdn_norm[l])
        pad = jnp.zeros((d, LANES - MOE_GROUPS - N_EXPERTS), F32)
        w_rt = jnp.concatenate([w_group[l], w_router[l], pad], axis=1)
        b_rt = jnp.concatenate([b_group[l], b_router[l], pad[0]])
        x2, h, ids, wts, cnt = _merge(ya, yb, yc, proj, x2, _bf(w_branch[l]), _bf(w_out[l]), gate_m,
                               norm_ffn[l], scale_f, shift_f, w_rt, b_rt)
        x2 = _moe(h, ids, wts, cnt, x2, gate_f, l, w_gate, w_up, w_down)
    return x2.reshape(bsz, s, d)
```

```python
import functools

import jax
import jax.numpy as jnp
from jax import lax
from jax.experimental import pallas as pl
from jax.experimental.pallas import tpu as pltpu

F32 = jnp.float32
BF16 = jnp.bfloat16
EPS = 1e-6

D_MODEL = 1024
SSM_HEADS = 8
SSM_HEAD_DIM = 64
SSM_INNER = 512
SSM_GROUPS = 2
SSM_STATE = 128
SSM_XBC = 1024
SSD_CHUNK = 128
SB_HEADS = 4
SB_HEAD_DIM = 128
SB_BLOCK = 128
GDN_HEADS = 4
GDN_HEAD_DIM = 128
GDN_CHUNK = 64
MOE_GROUPS = 4
EXPERTS_PER_GROUP = 8
N_EXPERTS = 32
MOE_TOP_K = 2
EXPERT_FF = 512

LANES = 128
COL_BR = 0
COL_SB = 3072
COL_GQKV = 4608
COL_XBC = 6144
COL_Z = 7168
COL_GGATE = 7680
WIDE_COLS = 8192
NCOL_DT = 0
NCOL_GAB = 128
NARROW_COLS = 256


def _col_spec(tb, width, col):
    assert col % width == 0
    return pl.BlockSpec((tb, width), lambda i: (i, col // width))

VMEM_LIMIT = 48 * 1024 * 1024
SB_SKIP_LOG = -110.0


def _bf(x):
    return x.astype(BF16)


def _dot(a, b):
    return jnp.dot(a, b, preferred_element_type=F32)


def _dot_nt(a, b):
    return lax.dot_general(a, b, (((1,), (1,)), ((), ())), preferred_element_type=F32)


def _dot_tn(a, b):
    return lax.dot_general(a, b, (((0,), (0,)), ((), ())), preferred_element_type=F32)


def _split3(x):
    hi = _bf(x)
    r = x - hi.astype(F32)
    mid = _bf(r)
    return hi, mid, _bf(r - mid.astype(F32))


def _dot_sel(sel, x):
    sel_b = _bf(sel)
    hi, mid, lo = _split3(x)
    return _dot(sel_b, hi) + (_dot(sel_b, mid) + _dot(sel_b, lo))


def _dot_sel_r(x, sel):
    sel_b = _bf(sel)
    hi, mid, lo = _split3(x)
    return _dot(hi, sel_b) + (_dot(mid, sel_b) + _dot(lo, sel_b))


def _split2(x):
    hi = _bf(x)
    lo = _bf(x - hi.astype(F32))
    return hi, lo


def _dot3(a, b):
    ah, al = _split2(a)
    bh, bl = _split2(b)
    return _dot(ah, bh) + (_dot(ah, bl) + _dot(al, bh))


def _silu(x):
    return x * jax.nn.sigmoid(x)


def _softplus(x):
    return jnp.maximum(x, 0.0) + jnp.log1p(jnp.exp(-jnp.abs(x)))


def _iota2(shape, dim):
    return lax.broadcasted_iota(jnp.int32, shape, dim)


ROW_SUBLANES = D_MODEL // LANES
assert ROW_SUBLANES == 8


def _store_tile_rows(ref, x):
    n = x.shape[0]
    for sl in range(ROW_SUBLANES):
        ref[pl.ds(sl, n, stride=ROW_SUBLANES), :] = x[:, sl * LANES:(sl + 1) * LANES]


def _load_tile_rows(ref, n):
    return jnp.concatenate([ref[pl.ds(sl, n, stride=ROW_SUBLANES), :] for sl in range(ROW_SUBLANES)], axis=-1)


def _params(*sem):
    return pltpu.CompilerParams(dimension_semantics=sem, vmem_limit_bytes=VMEM_LIMIT)


def _mod_kernel(c_ref, w_ref, b_ref, o_ref):
    c = _silu(c_ref[...])
    o_ref[0] = _dot3(c, w_ref[0]) + b_ref[0]


def _adaln_mod(c, w_ada, b_ada):
    depth, d, cols = w_ada.shape
    tn = 1024
    c8 = jnp.broadcast_to(c, (8, d))
    out = pl.pallas_call(
        _mod_kernel,
        grid=(depth, cols // tn),
        in_specs=[
            pl.BlockSpec((8, d), lambda l, j: (0, 0)),
            pl.BlockSpec((1, d, tn), lambda l, j: (l, 0, j)),
            pl.BlockSpec((1, 1, tn), lambda l, j: (l, 0, j)),
        ],
        out_specs=pl.BlockSpec((1, 8, tn), lambda l, j: (l, 0, j)),
        out_shape=jax.ShapeDtypeStruct((depth, 8, cols), F32),
        compiler_params=_params("arbitrary", "arbitrary"),
        name="adaln_mod",
    )(c8, w_ada, b_ada.reshape(depth, 1, cols))
    return out[:, 0, :]


def _norm_mod(x, g, scale, shift):
    y = x * lax.rsqrt(jnp.mean(x * x, axis=-1, keepdims=True) + EPS)
    return (y * g) * (1.0 + scale) + shift


INPROJ_TN = 1024
assert COL_SB % INPROJ_TN == 0 and 2 * SB_HEADS * SB_HEAD_DIM == INPROJ_TN


def _inproj_kernel(x_ref, g_ref, sc_ref, sh_ref, w_ref, wn_ref, qkg_ref, qks_ref, o_ref, on_ref, h_scr):
    j = pl.program_id(1)

    @pl.when(j == 0)
    def _():
        h = _bf(_norm_mod(x_ref[...], g_ref[...], sc_ref[...], sh_ref[...]))
        h_scr[...] = h
        on_ref[...] = _dot(h, wn_ref[...])

    @pl.when(j != COL_SB // INPROJ_TN)
    def _():
        o_ref[...] = _bf(_dot(h_scr[...], w_ref[...]))

    @pl.when(j == COL_SB // INPROJ_TN)
    def _():
        acc = _dot(h_scr[...], w_ref[...])
        dh = SB_HEAD_DIM
        for n in range(INPROJ_TN // dh):
            cols = slice(n * dh, (n + 1) * dh)
            o_ref[:, cols] = _bf(_head_rms(acc[:, cols], qkg_ref[:, cols]) * qks_ref[:, cols])


def _inproj(x2, g, scale, shift, w_wide, w_narrow, q_g, k_g):
    s, d = x2.shape
    tm = min(1024, s)
    tn = INPROJ_TN
    row = lambda a: a.reshape(1, d)
    vec = pl.BlockSpec((1, d), lambda i, j: (0, 0))
    qk_gain = jnp.concatenate([jnp.tile(q_g, SB_HEADS), jnp.tile(k_g, SB_HEADS)])
    qk_scale = jnp.concatenate([jnp.full((tn // 2,), SB_HEAD_DIM ** -0.5, F32), jnp.ones((tn // 2,), F32)])
    return pl.pallas_call(
        _inproj_kernel,
        grid=(s // tm, WIDE_COLS // tn),
        in_specs=[pl.BlockSpec((tm, d), lambda i, j: (i, 0)), vec, vec, vec,
                  pl.BlockSpec((d, tn), lambda i, j: (0, j)),
                  pl.BlockSpec((d, NARROW_COLS), lambda i, j: (0, 0)),
                  pl.BlockSpec((1, tn), lambda i, j: (0, 0)), pl.BlockSpec((1, tn), lambda i, j: (0, 0))],
        out_specs=[pl.BlockSpec((tm, tn), lambda i, j: (i, j)),
                   pl.BlockSpec((tm, NARROW_COLS), lambda i, j: (i, 0))],
        out_shape=[jax.ShapeDtypeStruct((s, WIDE_COLS), BF16), jax.ShapeDtypeStruct((s, NARROW_COLS), F32)],
        scratch_shapes=[pltpu.VMEM((tm, d), BF16)],
        compiler_params=_params("arbitrary", "arbitrary"),
        name="inproj",
    )(x2, row(g), row(scale), row(shift), w_wide, w_narrow, qk_gain.reshape(1, tn), qk_scale.reshape(1, tn))


def _causal_conv4(x, ext_scr, w_ref):
    tb = x.shape[0]
    ext_scr[8:8 + tb, :] = x
    y = x * w_ref[3:4, :]
    for k in (1, 2, 3):
        y = y + ext_scr[8 - k:8 - k + tb, :] * w_ref[3 - k:4 - k, :]
    ext_scr[0:8, :] = x[tb - 8:tb]
    return y


def _ssd_kernel(z_ref, xbc_ref, dt_ref, dtt_ref, cw_ref, cb_ref, dtb_ref, dtbt_ref, al_ref, alt_ref,
                dsk_ref, ng_ref, o_ref, ext_scr, act_scr, y_scr, st_scr):
    tb = xbc_ref.shape[0]
    L = SSD_CHUNK
    P = SSM_HEAD_DIM

    @pl.when(pl.program_id(0) == 0)
    def _():
        ext_scr[0:8, :] = jnp.zeros((8, ext_scr.shape[1]), F32)
        st_scr[...] = jnp.zeros_like(st_scr)

    act_scr[...] = _silu(_causal_conv4(xbc_ref[...].astype(F32), ext_scr, cw_ref) + cb_ref[...])

    ri = _iota2((L, L), 0)
    ci = _iota2((L, L), 1)
    tril = (ri >= ci).astype(F32)
    causal = ri >= ci
    a_col = -jnp.exp(al_ref[...])
    a_row = -jnp.exp(alt_ref[...])
    expand = (_iota2((LANES, SSM_INNER), 1) // P == _iota2((LANES, SSM_INNER), 0)).astype(F32)
    hpg = SSM_HEADS // SSM_GROUPS
    gw = hpg * P

    for c in range(tb // L):
        rows = slice(c * L, (c + 1) * L)
        dt = _softplus(dt_ref[rows, :] + dtb_ref[...])
        dtt = _softplus(dtt_ref[:, rows] + dtbt_ref[...])
        acum = _dot_sel(tril, dt * a_col)
        acum_t = _dot_sel_r(dtt * a_row, tril.T)
        per_head = jnp.concatenate([dt, jnp.exp(acum[L - 1:L, :] - acum), jnp.exp(acum)], axis=0)
        per_chan = _dot_sel_r(per_head, expand)
        dt_e, to_end_e, from_start_e = per_chan[:L], per_chan[L:2 * L], per_chan[2 * L:]
        act = act_scr[rows, :]
        xs = act[:, :SSM_INNER]
        xdt = xs * dt_e
        xdt_b = _bf(xdt)
        xdt_end_b = _bf(xdt * to_end_e)
        off_c = SSM_INNER + SSM_GROUPS * SSM_STATE
        y_in, y_st = [], []
        for g in range(SSM_GROUPS):
            bm = _bf(act[:, SSM_INNER + g * SSM_STATE:SSM_INNER + (g + 1) * SSM_STATE])
            cm = _bf(act[:, off_c + g * SSM_STATE:off_c + (g + 1) * SSM_STATE])
            gmat = _dot_nt(cm, bm)
            st = st_scr[g]
            y_st.append(_dot_nt(cm, _bf(st)))
            new = _dot_tn(xdt_end_b[:, g * gw:(g + 1) * gw], bm)
            for hh in range(hpg):
                h = g * hpg + hh
                seg = jnp.exp(jnp.where(causal, acum[:, h:h + 1] - acum_t[h:h + 1, :], -jnp.inf))
                y_in.append(_dot(_bf(gmat * seg), xdt_b[:, h * P:(h + 1) * P]))
                sl = slice(hh * P, (hh + 1) * P)
                st_scr[g, sl, :] = st[sl] * jnp.exp(acum_t[h:h + 1, L - 1:L]) + new[sl]
        y_scr[rows, :] = (jnp.concatenate(y_in, axis=-1) + jnp.concatenate(y_st, axis=-1) * from_start_e
                          + xs * dsk_ref[...])

    y = y_scr[...] * _silu(z_ref[...].astype(F32))
    gsz = SSM_INNER // SSM_GROUPS
    outs = []
    for g in range(SSM_GROUPS):
        yg = y[:, g * gsz:(g + 1) * gsz]
        yn = yg * lax.rsqrt(jnp.mean(yg * yg, axis=-1, keepdims=True) + EPS)
        outs.append(yn * ng_ref[:, g * gsz:(g + 1) * gsz])
    o_ref[...] = jnp.concatenate(outs, axis=-1)


def _ssd(proj, narrow, dtt, conv_w, conv_b, dt_bias, a_log, d_skip, norm_g):
    s = proj.shape[0]
    tb = min(512, s)
    pad8 = lambda v: jnp.pad(v, (0, LANES - v.shape[0])).reshape(1, LANES)
    colv = lambda v: v.reshape(SSM_HEADS, 1)
    full = lambda shape: pl.BlockSpec(shape, lambda i: (0,) * len(shape))
    return pl.pallas_call(
        _ssd_kernel,
        grid=(s // tb,),
        in_specs=[
            _col_spec(tb, SSM_INNER, COL_Z),
            _col_spec(tb, SSM_XBC, COL_XBC),
            _col_spec(tb, LANES, NCOL_DT),
            pl.BlockSpec((SSM_HEADS, tb), lambda i: (0, i)),
            full((4, SSM_XBC)), full((1, SSM_XBC)), full((1, LANES)), full((SSM_HEADS, 1)),
            full((1, LANES)), full((SSM_HEADS, 1)), full((1, SSM_INNER)), full((1, SSM_INNER)),
        ],
        out_specs=pl.BlockSpec((tb, SSM_INNER), lambda i: (i, 0)),
        out_shape=jax.ShapeDtypeStruct((s, SSM_INNER), F32),
        scratch_shapes=[
            pltpu.VMEM((tb + 8, SSM_XBC), F32),
            pltpu.VMEM((tb, SSM_XBC), F32),
            pltpu.VMEM((tb, SSM_INNER), F32),
            pltpu.VMEM((SSM_GROUPS, SSM_HEADS // SSM_GROUPS * SSM_HEAD_DIM, SSM_STATE), F32),
        ],
        compiler_params=_params("arbitrary"),
        name="ssd",
    )(proj, proj, narrow, dtt, conv_w, conv_b.reshape(1, -1), pad8(dt_bias), colv(dt_bias),
      pad8(a_log), colv(a_log), jnp.repeat(d_skip, SSM_HEAD_DIM).reshape(1, -1), norm_g.reshape(1, -1))


def _layout_w_in(w):
    d = w.shape[0]
    z, xbc, dt, sb, gqkv, gab, ggate, br = jnp.split(w, [512, 1536, 1544, 3080, 4616, 4624, 5136], axis=1)
    pad = jnp.zeros((d, LANES - 8), w.dtype)
    wide = jnp.concatenate([br, sb, gqkv, xbc, z, ggate], axis=1).astype(BF16)
    narrow = jnp.concatenate([dt, pad, gab, pad], axis=1).astype(BF16)
    return wide, narrow


def _head_rms(x, g):
    return (x * lax.rsqrt(jnp.mean(x * x, axis=-1, keepdims=True) + EPS)) * g


def _sb_blocks(qs, kns, vs, accs, suffix, masked):
    blk = qs[0].shape[0]
    strict = _iota2((blk, blk), 1) < _iota2((blk, blk), 0)
    zs = [_dot_nt(q, kn) for q, kn in zip(qs, kns)]
    sps = [jnp.maximum(z, 0.0) + jnp.log(1.0 + jnp.exp(-jnp.abs(z))) for z in zs]
    log_keeps = [jnp.where(strict, -sp, 0.0) if masked else -sp for sp in sps]
    splits = [_split2(lk) for lk in log_keeps]
    afters = [(_dot(hi, suffix) + _dot(lo, suffix)) + acc for (hi, lo), acc in zip(splits, accs)]
    atts = [jnp.exp((z - sp) + after) for z, sp, after in zip(zs, sps, afters)]
    if masked:
        atts = [jnp.where(strict, att, 0.0) for att in atts]
    outs = [_dot(_bf(att), v) for att, v in zip(atts, vs)]
    return outs, [jnp.sum(lk, axis=-1, keepdims=True) for lk in log_keeps]


SB_QBLOCKS = 2


def _sb_kernel(q_ref, k_ref, v_ref, o_ref, acc_scr):
    blk = SB_BLOCK
    dh = SB_HEAD_DIM
    first = pl.program_id(0) * SB_QBLOCKS
    pairs = [(b, h) for b in range(SB_QBLOCKS) for h in range(SB_HEADS)]
    suffix = (_iota2((blk, blk), 0) > _iota2((blk, blk), 1)).astype(BF16)
    qs = [q_ref[b * blk:(b + 1) * blk, h * dh:(h + 1) * dh] for b, h in pairs]

    def load_kv(offset):
        rows = [pl.ds(pl.multiple_of(jnp.maximum(first + b - offset, 0) * blk, blk), blk) for b in range(SB_QBLOCKS)]
        return ([k_ref[rows[b], h * dh:(h + 1) * dh] for b, h in pairs],
                [v_ref[rows[b], h * dh:(h + 1) * dh] for b, h in pairs])

    def live(accs):
        top = functools.reduce(jnp.maximum, accs)
        return (jnp.max(top) > SB_SKIP_LOG).astype(jnp.int32)

    def out_slice(n):
        b, h = pairs[n]
        return (slice(b * blk, (b + 1) * blk), slice(h * dh, (h + 1) * dh))

    kns, vs = load_kv(0)
    outs, sums = _sb_blocks(qs, kns, vs, [jnp.zeros((blk, 1), F32)] * len(pairs), suffix, True)
    for n in range(len(pairs)):
        o_ref[out_slice(n)] = outs[n]
        acc_scr[n] = sums[n]

    def cond(carry):
        offset, alive = carry
        return jnp.logical_and(offset <= first + SB_QBLOCKS - 1, alive > 0)

    def body(carry):
        offset, _ = carry
        kns, vs = load_kv(offset)
        accs = [acc_scr[n] for n in range(len(pairs))]
        outs, sums = _sb_blocks(qs, kns, vs, accs, suffix, False)
        valid = [jnp.where(first + b - offset >= 0, 1.0, 0.0) for b in range(SB_QBLOCKS)]
        accs = [acc + rs * valid[b] for acc, rs, (b, _) in zip(accs, sums, pairs)]
        for n in range(len(pairs)):
            o_ref[out_slice(n)] += outs[n] * valid[pairs[n][0]]
            acc_scr[n] = accs[n]
        return offset + 1, live(accs)

    lax.while_loop(cond, body, (jnp.int32(1), live(sums)))


def _stick_breaking(proj):
    s = proj.shape[0]
    tq = SB_QBLOCKS * SB_BLOCK
    dh = SB_HEAD_DIM
    width = SB_HEADS * dh
    resident = lambda col: pl.BlockSpec((s, width), lambda i: (0, col // width), pipeline_mode=pl.Buffered(1))
    return pl.pallas_call(
        _sb_kernel,
        grid=(s // tq,),
        in_specs=[_col_spec(tq, width, COL_SB), resident(COL_SB + width), resident(COL_SB + 2 * width)],
        out_specs=pl.BlockSpec((tq, width), lambda i: (i, 0)),
        out_shape=jax.ShapeDtypeStruct((s, width), F32),
        scratch_shapes=[pltpu.VMEM((SB_QBLOCKS * SB_HEADS, SB_BLOCK, 1), F32)],
        compiler_params=_params("arbitrary"),
        name="stick_breaking",
    )(proj, proj, proj)


GDN_SUB = 128


def _dot3_nt(a, b):
    ah, al = _split2(a)
    bh, bl = _split2(b)
    return _dot_nt(ah, bh) + (_dot_nt(ah, bl) + _dot_nt(al, bh))


def _chunk_lower_inverses(ms, chunk):
    n = ms[0].shape[0]
    eye = (_iota2((n, n), 0) == _iota2((n, n), 1)).astype(F32)
    ps = [-m for m in ms]
    invs = [eye + p for p in ps]
    p_parts = [_split2(p) for p in ps]
    for j in range((chunk - 1).bit_length() - 1):
        if j == 0:
            ps = [_dot(ph, ph) + (_dot(ph, pl_) + _dot(pl_, ph)) for ph, pl_ in p_parts]
        else:
            ps = [_dot(ph, ph) for ph, _ in p_parts]
        p_parts = [_split2(p) for p in ps]
        inv_parts = [_split2(inv) for inv in invs]
        invs = [inv + (_dot(ih, ph) + (_dot(ih, pl_) + _dot(il, ph)))
                for inv, (ih, il), (ph, pl_) in zip(invs, inv_parts, p_parts)]
    return invs


def _gdn_kernel(qkv_ref, gab_ref, gabt_ref, gate_ref, cw_ref, al_ref, alt_ref, dtb_ref, dtbt_ref, ng_ref,
                o_ref, ext_scr, act_scr, st_scr):
    tb = qkv_ref.shape[0]
    C = GDN_CHUNK
    dh = GDN_HEAD_DIM
    inner = GDN_HEADS * dh
    heads = range(GDN_HEADS)

    @pl.when(pl.program_id(0) == 0)
    def _():
        ext_scr[0:8, :] = jnp.zeros((8, ext_scr.shape[1]), F32)
        st_scr[...] = jnp.zeros_like(st_scr)

    act_scr[...] = _silu(_causal_conv4(qkv_ref[...].astype(F32), ext_scr, cw_ref))

    sub = min(GDN_SUB, tb)
    cps = sub // C
    ri = _iota2((sub, sub), 0)
    ci = _iota2((sub, sub), 1)
    same = (ri // C) == (ci // C)
    incl = jnp.logical_and(same, ri >= ci)
    strict = jnp.logical_and(same, ri > ci)
    tril = incl.astype(F32)
    al_col = -jnp.exp(al_ref[...])
    al_row = -jnp.exp(alt_ref[...])
    units = [(b, h) for b in range(tb // sub) for h in heads]

    gc_cols, gc_rows, betas = [], [], []
    for b in range(tb // sub):
        rows = slice(b * sub, (b + 1) * sub)
        gab = gab_ref[rows, :]
        g_col = al_col * _softplus(gab + dtb_ref[...])
        g_row = al_row * _softplus(gabt_ref[:, rows] + dtbt_ref[...])
        gc_cols.append(_dot_sel(tril, g_col))
        gc_rows.append(_dot_sel_r(g_row, tril.T))
        betas.append(jax.nn.sigmoid(gab))

    qs, ks, kbs, gcs, decays, rhss = [], [], [], [], [], []
    for b, h in units:
        rows = slice(b * sub, (b + 1) * sub)
        q = act_scr[rows, h * dh:(h + 1) * dh]
        k = act_scr[rows, inner + h * dh:inner + (h + 1) * dh]
        v = act_scr[rows, 2 * inner + h * dh:2 * inner + (h + 1) * dh]
        q = q * lax.rsqrt(jnp.sum(q * q, axis=-1, keepdims=True) + EPS) * (dh ** -0.5)
        k = k * lax.rsqrt(jnp.sum(k * k, axis=-1, keepdims=True) + EPS)
        beta = betas[b][:, GDN_HEADS + h:GDN_HEADS + h + 1]
        gc = gc_cols[b][:, h:h + 1]
        kb = k * beta
        qs.append(q)
        ks.append(k)
        kbs.append(kb)
        gcs.append(gc)
        decays.append(jnp.exp(jnp.where(incl, gc - gc_rows[b][h:h + 1, :], -jnp.inf)))
        rhss.append(jnp.concatenate([v * beta, kb * jnp.exp(gc)], axis=-1))

    n_units = range(len(units))
    ms = [jnp.where(strict, _dot3_nt(kbs[n], ks[n]) * decays[n], 0.0) for n in n_units]
    invs = _chunk_lower_inverses(ms, C)
    sols = [_dot3(invs[n], rhss[n]) for n in n_units]
    attns = [_bf(jnp.where(incl, _dot_nt(_bf(qs[n]), _bf(ks[n])) * decays[n], 0.0)) for n in n_units]
    q_decs = [_bf(qs[n] * jnp.exp(gcs[n])) for n in n_units]

    sts = [st_scr[h] for h in heads]
    v_news = [[] for _ in n_units]
    o_inters = [[] for _ in n_units]
    for c in range(tb // C):
        b = c // cps
        rows = slice((c % cps) * C, (c % cps + 1) * C)
        for h in heads:
            n = b * GDN_HEADS + h
            st_b = _bf(sts[h])
            g_last = gc_rows[b][h:h + 1, rows.stop - 1:rows.stop]
            v_new = sols[n][rows, :dh] - _dot(_bf(sols[n][rows, dh:]), st_b)
            v_new_b = _bf(v_new)
            o_inters[n].append(_dot(q_decs[n][rows, :], st_b))
            k_dec = ks[n][rows, :] * jnp.exp(g_last - gcs[n][rows, :])
            sts[h] = sts[h] * jnp.exp(g_last) + _dot_tn(_bf(k_dec), v_new_b)
            v_news[n].append(v_new_b)
    for h in heads:
        st_scr[h] = sts[h]
    for n, (b, h) in enumerate(units):
        rows = slice(b * sub, (b + 1) * sub)
        o = jnp.concatenate(o_inters[n], axis=0) + _dot(attns[n], jnp.concatenate(v_news[n], axis=0))
        o = _head_rms(o, ng_ref[...]) * _silu(gate_ref[rows, h * dh:(h + 1) * dh].astype(F32))
        o_ref[rows, h * dh:(h + 1) * dh] = o


def _gdn(proj, narrow, gabt, conv_w, a_log, dt_bias, norm_g):
    s = proj.shape[0]
    tb = min(512, s)
    inner = GDN_HEADS * GDN_HEAD_DIM
    pad_lane = lambda v: jnp.pad(v, (0, LANES - v.shape[0])).reshape(1, LANES)
    pad_col = lambda v: jnp.pad(v, (0, 8 - v.shape[0])).reshape(8, 1)
    full = lambda shape: pl.BlockSpec(shape, lambda i: (0,) * len(shape))
    return pl.pallas_call(
        _gdn_kernel,
        grid=(s // tb,),
        in_specs=[
            _col_spec(tb, 3 * inner, COL_GQKV),
            _col_spec(tb, LANES, NCOL_GAB),
            pl.BlockSpec((8, tb), lambda i: (0, i)),
            _col_spec(tb, inner, COL_GGATE),
            full((4, 3 * inner)), full((1, LANES)), full((8, 1)), full((1, LANES)), full((8, 1)),
            full((1, GDN_HEAD_DIM)),
        ],
        out_specs=pl.BlockSpec((tb, inner), lambda i: (i, 0)),
        out_shape=jax.ShapeDtypeStruct((s, inner), F32),
        scratch_shapes=[
            pltpu.VMEM((tb + 8, 3 * inner), F32),
            pltpu.VMEM((tb, 3 * inner), F32),
            pltpu.VMEM((GDN_HEADS, GDN_HEAD_DIM, GDN_HEAD_DIM), F32),
        ],
        compiler_params=_params("arbitrary"),
        name="gdn",
    )(proj, narrow, gabt, proj, conv_w, pad_lane(a_log), pad_col(a_log), pad_lane(dt_bias), pad_col(dt_bias),
      norm_g.reshape(1, -1))


def _merge_kernel(ya_ref, yb_ref, yc_ref, br_ref, x_ref, wbr_ref, wout_ref, gm_ref, g_ref, sc_ref, sh_ref,
                  wrt_ref, brt_ref, xo_ref, h_ref, ids_ref, wts_ref, cnt_ref, seen_scr):
    d = x_ref.shape[1]

    @pl.when(pl.program_id(0) == 0)
    def _():
        seen_scr[...] = jnp.zeros_like(seen_scr)

    merged = None
    for i, y_ref in enumerate((ya_ref, yb_ref, yc_ref)):
        gate = jax.nn.sigmoid(br_ref[:, i * d:(i + 1) * d].astype(F32))
        term = gate * _dot(_bf(y_ref[...]), wbr_ref[i])
        merged = term if merged is None else merged + term
    x_new = x_ref[...] + gm_ref[...] * _dot(_bf(merged), wout_ref[...])
    xo_ref[...] = x_new
    h = _norm_mod(x_new, g_ref[...], sc_ref[...], sh_ref[...])
    _store_tile_rows(h_ref, h)
    ids, wts, seen = _route_rows(_dot3(h, wrt_ref[...]) + brt_ref[...], seen_scr[...])
    ids_ref[...] = ids
    wts_ref[...] = wts
    seen_scr[...] = seen
    cnt_ref[...] = jnp.broadcast_to(seen, cnt_ref.shape)


def _merge(ya, yb, yc, proj, x2, wbr_bf, wout_bf, gate_m, g, scale, shift, w_rt, b_rt):
    s, d = x2.shape
    tb = min(512, s)
    bw = ya.shape[1]
    row = lambda a: a.reshape(1, -1)
    vec = pl.BlockSpec((1, d), lambda i: (0, 0))
    blk = lambda w: pl.BlockSpec((tb, w), lambda i: (i, 0))
    return pl.pallas_call(
        _merge_kernel,
        grid=(s // tb,),
        in_specs=[blk(bw), blk(bw), blk(bw), _col_spec(tb, 3 * d, COL_BR), blk(d),
                  pl.BlockSpec((3, bw, d), lambda i: (0, 0, 0)), pl.BlockSpec((d, d), lambda i: (0, 0)),
                  vec, vec, vec, vec,
                  pl.BlockSpec((d, LANES), lambda i: (0, 0)), pl.BlockSpec((1, LANES), lambda i: (0, 0))],
        out_specs=[blk(d), pl.BlockSpec((tb * ROW_SUBLANES, LANES), lambda i: (i, 0)), blk(LANES), blk(LANES),
                   pl.BlockSpec((8, LANES), lambda i: (0, 0))],
        out_shape=[jax.ShapeDtypeStruct((s, d), F32), jax.ShapeDtypeStruct((s * ROW_SUBLANES, LANES), F32),
                   jax.ShapeDtypeStruct((s, LANES), jnp.int32), jax.ShapeDtypeStruct((s, LANES), F32),
                   jax.ShapeDtypeStruct((8, LANES), F32)],
        scratch_shapes=[pltpu.VMEM((1, LANES), F32)],
        compiler_params=_params("arbitrary"),
        name="merge",
    )(ya, yb, yc, proj, x2, wbr_bf, wout_bf, row(gate_m), row(g), row(scale), row(shift), w_rt, row(b_rt))


ROUTE_E0 = MOE_GROUPS


def _route_rows(lg, seen_before):
    tb = lg.shape[0]
    lane = _iota2((tb, LANES), 1)
    big = jnp.int32(LANES)
    neg = -jnp.inf
    gl = jnp.where(lane < MOE_GROUPS, lg, neg)
    gmax = jnp.max(gl, axis=-1, keepdims=True)
    g_sel = jnp.min(jnp.where(gl == gmax, lane, big), axis=-1, keepdims=True)
    p_group = 1.0 / jnp.sum(jnp.exp(gl - gmax), axis=-1, keepdims=True)
    lo = ROUTE_E0 + EXPERTS_PER_GROUP * g_sel
    el = jnp.where(jnp.logical_and(lane >= lo, lane < lo + EXPERTS_PER_GROUP), lg, neg)
    m1 = jnp.max(el, axis=-1, keepdims=True)
    i1 = jnp.min(jnp.where(el == m1, lane, big), axis=-1, keepdims=True)
    esum = jnp.sum(jnp.exp(el - m1), axis=-1, keepdims=True)
    el2 = jnp.where(lane == i1, neg, el)
    m2 = jnp.max(el2, axis=-1, keepdims=True)
    i2 = jnp.min(jnp.where(el2 == m2, lane, big), axis=-1, keepdims=True)
    p1 = 1.0 / esum
    p2 = jnp.exp(m2 - m1) / esum
    w1 = p_group * p1 / (p1 + p2)
    w2 = p_group * p2 / (p1 + p2)

    sel1 = lane == i1
    sel2 = lane == i2
    onehot = jnp.where(jnp.logical_or(sel1, sel2), 1.0, 0.0)
    before = (_iota2((tb, tb), 0) > _iota2((tb, tb), 1)).astype(BF16)
    seen = _dot(before, _bf(onehot)) + seen_before
    r1 = jnp.sum(jnp.where(sel1, seen, 0.0), axis=-1, keepdims=True)
    r2 = jnp.sum(jnp.where(sel2, seen, 0.0), axis=-1, keepdims=True)
    ids = jnp.where(lane == 0, i1 - ROUTE_E0, jnp.where(lane == 1, i2 - ROUTE_E0,
          jnp.where(lane == 2, r1.astype(jnp.int32), jnp.where(lane == 3, r2.astype(jnp.int32), 0))))
    wts = jnp.where(lane == 0, w1, jnp.where(lane == 1, w2, 0.0))
    return ids, wts, seen_before + jnp.sum(onehot, axis=0, keepdims=True)


EXPERT_BLOCK = 512
ROW_TB = 1024


def _tile_row(ref, row):
    if not isinstance(row, int):
        row = pl.multiple_of(row * ROW_SUBLANES, ROW_SUBLANES)
    else:
        row = row * ROW_SUBLANES
    return ref.at[pl.ds(row, ROW_SUBLANES), :]


def _dispatch_kernel(dest_ref, pend_ref, padded_ref, nused_ref, h_ref, xb_ref, zero_scr, sem, zero_sem):
    tb = h_ref.shape[0] // ROW_SUBLANES
    blk_rows = EXPERT_BLOCK * ROW_SUBLANES
    base = pl.program_id(0) * tb * MOE_TOP_K

    @pl.when(pl.program_id(0) == 0)
    def _():
        zero_scr[...] = jnp.zeros_like(zero_scr)

        def zero_block(block):
            start = block * blk_rows
            if not isinstance(start, int):
                start = pl.multiple_of(start, blk_rows)
            return pltpu.make_async_copy(zero_scr, xb_ref.at[pl.ds(start, blk_rows), :], zero_sem)

        def for_each_zero_block(action):
            for e in range(N_EXPERTS):
                @pl.when(padded_ref[e] > 0)
                def _():
                    action(zero_block(pend_ref[e] // EXPERT_BLOCK - 1))
            for b in range(xb_ref.shape[0] // blk_rows):
                @pl.when(b >= nused_ref[0])
                def _():
                    action(zero_block(b))

        for_each_zero_block(lambda copy: copy.start())
        for_each_zero_block(lambda copy: copy.wait())

    def row_copy(t, k, d):
        return pltpu.make_async_copy(_tile_row(h_ref, t), _tile_row(xb_ref, d), sem)

    def issue(t, carry):
        for k in range(MOE_TOP_K):
            row_copy(t, k, dest_ref[base + t * MOE_TOP_K + k]).start(priority=k % 2)
        return carry

    for t in range(tb):
        issue(t, 0)
    for k in range(MOE_TOP_K):
        pltpu.make_async_copy(h_ref, xb_ref.at[pl.ds(0, tb * ROW_SUBLANES), :], sem).wait()


def _dispatch(dest, pend, padded, n_used, h_tiles, n_slots):
    s = h_tiles.shape[0] // ROW_SUBLANES
    tb = min(ROW_TB, s)
    return pl.pallas_call(
        _dispatch_kernel,
        grid_spec=pltpu.PrefetchScalarGridSpec(
            num_scalar_prefetch=4,
            grid=(s // tb,),
            in_specs=[pl.BlockSpec((tb * ROW_SUBLANES, LANES), lambda i, *_: (i, 0))],
            out_specs=pl.BlockSpec(memory_space=pl.ANY),
            scratch_shapes=[pltpu.VMEM((EXPERT_BLOCK * ROW_SUBLANES, LANES), F32), pltpu.SemaphoreType.DMA(()),
                            pltpu.SemaphoreType.DMA(())],
        ),
        out_shape=jax.ShapeDtypeStruct((n_slots * ROW_SUBLANES, LANES), F32),
        compiler_params=_params("arbitrary"),
        name="dispatch",
    )(dest, pend, padded, n_used, h_tiles)


def _expert_kernel(be_ref, nused_ref, x_ref, wg_ref, wu_ref, wd_ref, o_ref, wg_b, wu_b, wd_b):
    b = pl.program_id(0)

    @pl.when(jnp.logical_or(b == 0, be_ref[b] != be_ref[jnp.maximum(b - 1, 0)]))
    def _():
        wg_b[...] = _bf(wg_ref[0, 0])
        wu_b[...] = _bf(wu_ref[0, 0])
        wd_b[...] = _bf(wd_ref[0, 0])

    @pl.when(b < nused_ref[0])
    def _():
        x = _bf(_load_tile_rows(x_ref, EXPERT_BLOCK))
        hid = _silu(_dot(x, wg_b[...])) * _dot(x, wu_b[...])
        _store_tile_rows(o_ref, _dot(_bf(hid), wd_b[...]))

    @pl.when(b >= nused_ref[0])
    def _():
        o_ref[...] = jnp.zeros_like(o_ref)


def _experts(block_expert, n_used, xb, layer, w_gate, w_up, w_down):
    d, ff = w_gate.shape[2:]
    blk_rows = EXPERT_BLOCK * ROW_SUBLANES
    nb = xb.shape[0] // blk_rows
    return pl.pallas_call(
        _expert_kernel,
        grid_spec=pltpu.PrefetchScalarGridSpec(
            num_scalar_prefetch=2,
            grid=(nb,),
            in_specs=[
                pl.BlockSpec((blk_rows, LANES), lambda b, be, nu: (b, 0)),
                pl.BlockSpec((1, 1, d, ff), lambda b, be, nu: (layer, be[b], 0, 0)),
                pl.BlockSpec((1, 1, d, ff), lambda b, be, nu: (layer, be[b], 0, 0)),
                pl.BlockSpec((1, 1, ff, d), lambda b, be, nu: (layer, be[b], 0, 0)),
            ],
            out_specs=pl.BlockSpec((blk_rows, LANES), lambda b, be, nu: (b, 0)),
            scratch_shapes=[pltpu.VMEM((d, ff), BF16), pltpu.VMEM((d, ff), BF16), pltpu.VMEM((ff, d), BF16)],
        ),
        out_shape=jax.ShapeDtypeStruct(xb.shape, F32),
        compiler_params=_params("arbitrary"),
        name="experts",
    )(block_expert, n_used, xb, w_gate, w_up, w_down)


def _combine_kernel(dest_ref, yb_ref, wts_ref, x_ref, gf_ref, o_ref, buf, sem):
    tb = x_ref.shape[0]
    base = pl.program_id(0) * tb * MOE_TOP_K

    def row_copy(t, k, d):
        return pltpu.make_async_copy(_tile_row(yb_ref, d), _tile_row(buf.at[k], t), sem)

    def issue(t, carry):
        for k in range(MOE_TOP_K):
            row_copy(t, k, dest_ref[base + t * MOE_TOP_K + k]).start(priority=k % 2)
        return carry

    for t in range(tb):
        issue(t, 0)
    for k in range(MOE_TOP_K):
        pltpu.make_async_copy(yb_ref.at[pl.ds(0, tb * ROW_SUBLANES), :], buf.at[k], sem).wait()
    wts = wts_ref[...]
    y = wts[:, 0:1] * _load_tile_rows(buf.at[0], tb) + wts[:, 1:2] * _load_tile_rows(buf.at[1], tb)
    o_ref[...] = x_ref[...] + gf_ref[...] * y


def _combine(dest, yb, wts, x2, gate_f):
    s, d = x2.shape
    tb = min(ROW_TB, s)
    return pl.pallas_call(
        _combine_kernel,
        grid_spec=pltpu.PrefetchScalarGridSpec(
            num_scalar_prefetch=1,
            grid=(s // tb,),
            in_specs=[pl.BlockSpec(memory_space=pl.ANY),
                      pl.BlockSpec((tb, LANES), lambda i, dest: (i, 0)),
                      pl.BlockSpec((tb, d), lambda i, dest: (i, 0)),
                      pl.BlockSpec((1, d), lambda i, dest: (0, 0))],
            out_specs=pl.BlockSpec((tb, d), lambda i, dest: (i, 0)),
            scratch_shapes=[pltpu.VMEM((MOE_TOP_K, tb * ROW_SUBLANES, LANES), F32), pltpu.SemaphoreType.DMA(())],
        ),
        out_shape=jax.ShapeDtypeStruct((s, d), F32),
        compiler_params=_params("arbitrary"),
        name="combine",
    )(dest, yb, wts, x2, gate_f.reshape(1, d))


def _moe(h, ids, wts, cnt, x2, gate_f, layer, w_gate, w_up, w_down):
    s = x2.shape[0]
    counts = cnt[0, ROUTE_E0:ROUTE_E0 + N_EXPERTS].astype(jnp.int32)
    padded = (counts + EXPERT_BLOCK - 1) // EXPERT_BLOCK * EXPERT_BLOCK
    pend = jnp.cumsum(padded)
    pstart = pend - padded
    is_expert = ids[:, 0:MOE_TOP_K, None] == jnp.arange(N_EXPERTS, dtype=jnp.int32)
    slot0 = jnp.sum(jnp.where(is_expert, pstart, 0), axis=-1)
    dest = (slot0 + ids[:, MOE_TOP_K:2 * MOE_TOP_K]).reshape(s * MOE_TOP_K)
    nb = (s * MOE_TOP_K) // EXPERT_BLOCK + N_EXPERTS
    block_start = jnp.arange(nb, dtype=jnp.int32) * EXPERT_BLOCK
    block_expert = jnp.sum((pend[None, :] <= block_start[:, None]).astype(jnp.int32), axis=1)
    block_expert = jnp.minimum(block_expert, N_EXPERTS - 1)
    n_used = (pend[-1:] // EXPERT_BLOCK).astype(jnp.int32)
    xb = _dispatch(dest, pend.astype(jnp.int32), padded.astype(jnp.int32), n_used, h, nb * EXPERT_BLOCK)
    yb = _experts(block_expert, n_used, xb, layer, w_gate, w_up, w_down)
    return _combine(dest, yb, wts, x2, gate_f)


def kernel(x, c, w_ada, b_ada, norm_mix, norm_ffn, w_in, ssm_conv_w, ssm_conv_b, ssm_dt_bias, ssm_a_log, ssm_d,
           ssm_norm, sb_q_norm, sb_k_norm, gdn_conv_w, gdn_a_log, gdn_dt_bias, gdn_norm, w_branch, w_out,
           w_group, b_group, w_router, b_router, w_gate, w_up, w_down):
    bsz, s, d = x.shape
    assert bsz == 1 and d == D_MODEL
    depth = w_in.shape[0]
    mod = _adaln_mod(c, w_ada, b_ada)
    x2 = x.reshape(s, d)
    for l in range(depth):
        shift_m, scale_m, gate_m, shift_f, scale_f, gate_f = jnp.split(mod[l], 6)
        proj, narrow = _inproj(x2, norm_mix[l], scale_m, shift_m, *_layout_w_in(w_in[l]), sb_q_norm[l], sb_k_norm[l])
        dtt = narrow[:, NCOL_DT:NCOL_DT + 8].T
        gabt = narrow[:, NCOL_GAB:NCOL_GAB + 8].T
        ya = _ssd(proj, narrow, dtt, ssm_conv_w[l], ssm_conv_b[l], ssm_dt_bias[l], ssm_a_log[l], ssm_d[l], ssm_norm[l])
        yb = _stick_breaking(proj)
        yc = _gdn(proj, narrow, gabt, gdn_conv_w[l], gdn_a_log[l], gdn_dt_bias[l], gdn_norm[l])
        pad = jnp.zeros((d, LANES - MOE_GROUPS - N_EXPERTS), F32)
        w_rt = jnp.concatenate([w_group[l], w_router[l], pad], axis=1)
        b_rt = jnp.concatenate([b_group[l], b_router[l], pad[0]])
        x2, h, ids, wts, cnt = _merge(ya, yb, yc, proj, x2, _bf(w_branch[l]), _bf(w_out[l]), gate_m,
                               norm_ffn[l], scale_f, shift_f, w_rt, b_rt)
        x2 = _moe(h, ids, wts, cnt, x2, gate_f, l, w_gate, w_up, w_down)
    return x2.reshape(bsz, s, d)
```
